```python
import jax, jax.numpy as jnp
from jax import lax
import numpy as np

D_MODEL = 1024
BATCH = 4
SEQ = 4096
DEPTH = 1
DEC_BATCH = 16
DEC_SEQ = 2048
PAST_LEN = 128

HEAD_DIM = 64
DIL_PAIRS = ((128, 1), (512, 4), (2048, 16))
HEADS_PER_DIL = 4
N_ATTN_HEADS = HEADS_PER_DIL * len(DIL_PAIRS)
ATTN_WIDTH = N_ATTN_HEADS * HEAD_DIM
N_GMLP_GROUPS = 4
GMLP_GROUP_DIM = 64
GMLP_WIDTH = N_GMLP_GROUPS * GMLP_GROUP_DIM
GMLP_CHUNK = 128
MIX_WIDTH = ATTN_WIDTH + GMLP_WIDTH
IN_WIDTH = 3 * ATTN_WIDTH + 2 * GMLP_WIDTH
D_FF = ((-(-8 * D_MODEL // 3) + 255) // 256) * 256
N_MOD = 6
RMS_EPS = 1e-6
LN_EPS = 1e-5
NEG_INF = -1e30

kernel_name = "hybrid_dilated_attn_gmlp_encoder"


def rms_norm(x, g):
    x32 = x.astype(jnp.float32)
    y = x32 * lax.rsqrt(jnp.mean(x32 * x32, axis=-1, keepdims=True) + RMS_EPS)
    return (y * g.astype(jnp.float32)).astype(x.dtype)


def alibi_slopes():
    return 2.0 ** (-8.0 * jnp.arange(1, N_ATTN_HEADS + 1, dtype=jnp.float32) / N_ATTN_HEADS)


def dilated_window_attention(q, k, v, dil, n_side, slopes):
    B, S, H, E = q.shape
    L = S // dil
    nb = -(-L // n_side)
    Lp = nb * n_side

    def to_classes(t):
        return t.reshape(B, L, dil, H, E).transpose(0, 2, 1, 3, 4)

    qc, kc, vc = to_classes(q), to_classes(k), to_classes(v)
    qb = jnp.pad(qc, ((0, 0), (0, 0), (0, Lp - L), (0, 0), (0, 0))).reshape(B, dil, nb, n_side, H, E)

    def windows(t):
        tp = jnp.pad(t, ((0, 0), (0, 0), (n_side, Lp - L + n_side), (0, 0), (0, 0)))
        tp = tp.reshape(B, dil, nb + 2, n_side, H, E)
        return jnp.concatenate([tp[:, :, :-2], tp[:, :, 1:-1], tp[:, :, 2:]], axis=3)

    kw, vw = windows(kc), windows(vc)
    a = jnp.arange(n_side)
    c = jnp.arange(3 * n_side)
    blk = jnp.arange(nb)
    rel = c[None, :] - n_side - a[:, None]
    key_idx = blk[:, None] * n_side - n_side + c[None, :]
    valid = (jnp.abs(rel)[None] <= n_side) & ((key_idx >= 0) & (key_idx < L))[:, None, :]
    bias = -slopes[:, None, None] * (dil * jnp.abs(rel)).astype(jnp.float32)[None]

    s = jnp.einsum('bdnqhe,bdnkhe->bdnhqk', qb, kw, preferred_element_type=jnp.float32) * (E ** -0.5)
    s = jnp.where(valid[None, None, :, None], s + bias[None, None, None], NEG_INF)
    lse = jax.nn.logsumexp(s, axis=-1)
    p = jnp.exp(s - lse[..., None])
    o = jnp.einsum('bdnhqk,bdnkhe->bdnqhe', p.astype(v.dtype), vw)
    o = o.reshape(B, dil, Lp, H, E)[:, :, :L].transpose(0, 2, 1, 3, 4).reshape(B, S, H, E)
    lse = lse.transpose(0, 1, 2, 4, 3).reshape(B, dil, Lp, H)[:, :, :L].transpose(0, 2, 1, 3).reshape(B, S, H)
    return o, lse


def chunked_spatial_gating(u, v, w_s, b_s, ln_g):
    B, S, _ = u.shape
    u = jax.nn.gelu(u)
    v = jax.nn.gelu(v).reshape(B, S // GMLP_CHUNK, GMLP_CHUNK, N_GMLP_GROUPS, GMLP_GROUP_DIM)
    v32 = v.astype(jnp.float32)
    mu = jnp.mean(v32, axis=-1, keepdims=True)
    var = jnp.mean(jnp.square(v32 - mu), axis=-1, keepdims=True)
    vn = (v32 - mu) * lax.rsqrt(var + LN_EPS) * ln_g.astype(jnp.float32).reshape(N_GMLP_GROUPS, GMLP_GROUP_DIM)
    sv = jnp.einsum('gts,bnsgc->bntgc', w_s.astype(jnp.float32), vn) \
        + b_s.astype(jnp.float32).T[None, None, :, :, None]
    return u * sv.reshape(B, S, GMLP_WIDTH).astype(u.dtype)


def encoder_layer(x, c, w_ada, b_ada, g_pre_mix, w_in, w_s, b_s, g_gmlp, w_out,
                  g_post_mix, g_pre_ffn, w_gu, w_down, g_post_ffn):
    B, S, D = x.shape
    mod = (jnp.dot(jax.nn.silu(c), w_ada) + b_ada)[:, None, :]
    sh1, sc1, gt1, sh2, sc2, gt2 = jnp.split(mod, N_MOD, axis=-1)

    h = rms_norm(x, g_pre_mix) * (1.0 + sc1) + sh1
    proj = jnp.dot(h, w_in)
    A = ATTN_WIDTH
    q = proj[..., :A].reshape(B, S, N_ATTN_HEADS, HEAD_DIM)
    k = proj[..., A:2 * A].reshape(B, S, N_ATTN_HEADS, HEAD_DIM)
    v = proj[..., 2 * A:3 * A].reshape(B, S, N_ATTN_HEADS, HEAD_DIM)
    gu = proj[..., 3 * A:3 * A + GMLP_WIDTH]
    gv = proj[..., 3 * A + GMLP_WIDTH:]

    slopes = alibi_slopes()
    outs, lses = [], []
    for gi, (window, dil) in enumerate(DIL_PAIRS):
        hs = slice(gi * HEADS_PER_DIL, (gi + 1) * HEADS_PER_DIL)
        o, l = dilated_window_attention(q[:, :, hs], k[:, :, hs], v[:, :, hs], dil,
                                        window // (2 * dil), slopes[hs])
        outs.append(o)
        lses.append(l)
    wts = jax.nn.softmax(jnp.stack(lses, axis=0), axis=0)
    attn = jnp.concatenate([(wts[gi][..., None] * outs[gi]).astype(x.dtype) for gi in range(len(DIL_PAIRS))],
                           axis=2).reshape(B, S, ATTN_WIDTH)
    gm = chunked_spatial_gating(gu, gv, w_s, b_s, g_gmlp)
    mix = jnp.dot(jnp.concatenate([attn, gm], axis=-1), w_out)
    x = x + gt1 * rms_norm(mix, g_post_mix)

    h = rms_norm(x, g_pre_ffn) * (1.0 + sc2) + sh2
    gate, up = jnp.split(jnp.dot(h, w_gu), 2, axis=-1)
    f = jnp.dot(jax.nn.silu(gate) * up, w_down)
    return x + gt2 * rms_norm(f, g_post_ffn)


def setup_inputs(seed: int = 0) -> dict:
    key = jax.random.key(seed)
    ks = jax.random.split(key, 18)
    f32 = jnp.float32
    nrm = lambda k, shape, s: jax.random.normal(k, shape, f32) * s
    gain = lambda k, shape: 1.0 + 0.02 * jax.random.normal(k, shape, f32)
    return {
        "x_prompt": jax.random.normal(ks[0], (BATCH, SEQ, D_MODEL), f32),
        "x_sample": jax.random.normal(ks[1], (DEC_BATCH, DEC_SEQ, D_MODEL), f32),
        "c_prompt": jax.random.normal(ks[2], (BATCH, D_MODEL), f32),
        "c_sample": jax.random.normal(ks[3], (DEC_BATCH, D_MODEL), f32),
        "w_ada": nrm(ks[4], (DEPTH, D_MODEL, N_MOD * D_MODEL), 0.5 * D_MODEL ** -0.5),
        "b_ada": nrm(ks[5], (DEPTH, N_MOD * D_MODEL), 0.02),
        "g_pre_mix": gain(ks[6], (DEPTH, D_MODEL)),
        "w_in": nrm(ks[7], (DEPTH, D_MODEL, IN_WIDTH), D_MODEL ** -0.5),
        "w_s": nrm(ks[8], (DEPTH, N_GMLP_GROUPS, GMLP_CHUNK, GMLP_CHUNK), GMLP_CHUNK ** -0.5),
        "b_s": gain(ks[9], (DEPTH, N_GMLP_GROUPS, GMLP_CHUNK)),
        "g_gmlp": gain(ks[10], (DEPTH, GMLP_WIDTH)),
        "w_out": nrm(ks[11], (DEPTH, MIX_WIDTH, D_MODEL), MIX_WIDTH ** -0.5),
        "g_post_mix": gain(ks[12], (DEPTH, D_MODEL)),
        "g_pre_ffn": gain(ks[13], (DEPTH, D_MODEL)),
        "w_gu": nrm(ks[14], (DEPTH, D_MODEL, 2 * D_FF), D_MODEL ** -0.5),
        "w_down": nrm(ks[15], (DEPTH, D_FF, D_MODEL), D_FF ** -0.5),
        "g_post_ffn": gain(ks[16], (DEPTH, D_MODEL)),
    }


def reference(x_prompt, x_sample, c_prompt, c_sample, w_ada, b_ada, g_pre_mix, w_in, w_s, b_s,
              g_gmlp, w_out, g_post_mix, g_pre_ffn, w_gu, w_down, g_post_ffn):
    y_prompt, y_sample = x_prompt, x_sample
    for l in range(DEPTH):
        params = (w_ada[l], b_ada[l], g_pre_mix[l], w_in[l], w_s[l], b_s[l], g_gmlp[l], w_out[l],
                  g_post_mix[l], g_pre_ffn[l], w_gu[l], w_down[l], g_post_ffn[l])
        y_prompt = encoder_layer(y_prompt, c_prompt, *params)
        y_sample = encoder_layer(y_sample, c_sample, *params)
    return (y_prompt, y_sample)
```

```python
import functools

import numpy as np
import jax
import jax.numpy as jnp
from jax import lax
from jax.experimental import pallas as pl
from jax.experimental.pallas import tpu as pltpu

F32 = jnp.float32
BF16 = jnp.bfloat16

D_MODEL = 1024
HEAD_DIM = 64
HEADS_PER_GROUP = 4
GROUP_W = HEADS_PER_GROUP * HEAD_DIM
DILATIONS = (1, 4, 16)
N_SIDE = 64
N_GROUPS = len(DILATIONS)
ATTN_W = N_GROUPS * GROUP_W
GMLP_W = 256
GMLP_GROUP = 64
GMLP_CHUNK = 128
IN_W = 3 * ATTN_W + 2 * GMLP_W
D_FF = 2816
N_MOD = 6
RMS_EPS = 1e-6
LN_EPS = 1e-5
NEG_INF = -1e30

LANES = 128
Q_BLOCK = 128
FF_CHUNK = 256
VMEM_LIMIT = 56 * 1024 * 1024


def _cparams(n_axes):
    return pltpu.CompilerParams(
        dimension_semantics=("arbitrary",) * n_axes, vmem_limit_bytes=VMEM_LIMIT)


def _adaln_kernel(c_ref, w_ref, b_ref, o_ref):
    c = c_ref[...]
    a = (c * jax.nn.sigmoid(c)).astype(BF16)
    o_ref[...] = jnp.dot(a, w_ref[...].astype(BF16), preferred_element_type=F32) + b_ref[...]


def _adaln(c, w_ada, b_ada):
    n, d = c.shape
    nout = w_ada.shape[1]
    tn = 1024
    return pl.pallas_call(
        _adaln_kernel,
        grid=(nout // tn,),
        in_specs=[
            pl.BlockSpec((n, d), lambda j: (0, 0)),
            pl.BlockSpec((d, tn), lambda j: (0, j)),
            pl.BlockSpec((1, tn), lambda j: (0, j)),
        ],
        out_specs=pl.BlockSpec((n, tn), lambda j: (0, j)),
        out_shape=jax.ShapeDtypeStruct((n, nout), F32),
        compiler_params=_cparams(1),
        name="adaln",
    )(c, w_ada, b_ada.reshape(1, nout))


def _rms(x):
    return x * lax.rsqrt(jnp.mean(x * x, axis=-1, keepdims=True) + RMS_EPS)


def _inproj_kernel(x_ref, mod_ref, g_ref, w_ref, mavg_ref, lng_ref,
                   q0_ref, q1_ref, q2_ref, u_ref, vn_ref, slab_ref):
    t = x_ref.shape[0]
    sh1 = mod_ref[0:1, :]
    sc1 = mod_ref[1:2, :]
    h = (_rms(x_ref[...]) * (g_ref[...] * (1.0 + sc1)) + sh1).astype(BF16)

    qkv_w = 3 * GROUP_W
    q0_ref[...] = jnp.dot(h, w_ref[:, 0:qkv_w], preferred_element_type=F32).astype(BF16)
    n_slab = qkv_w // LANES
    for gi, out_ref in ((1, q1_ref), (2, q2_ref)):
        d = DILATIONS[gi]
        p = jnp.dot(h, w_ref[:, gi * qkv_w:(gi + 1) * qkv_w], preferred_element_type=F32)
        for j in range(n_slab):
            slab_ref[j] = p[:, j * LANES:(j + 1) * LANES]
        for r in range(d):
            for j in range(n_slab):
                out_ref[r, :, j * LANES:(j + 1) * LANES] = (
                    slab_ref[j, pl.ds(r, t // d, stride=d), :].astype(BF16))

    base = N_GROUPS * qkv_w
    gu = jnp.dot(h, w_ref[:, base:base + GMLP_W], preferred_element_type=F32)
    gv = jnp.dot(h, w_ref[:, base + GMLP_W:base + 2 * GMLP_W], preferred_element_type=F32)
    u_ref[...] = jax.nn.gelu(gu).astype(BF16)
    v = jax.nn.gelu(gv)
    mu = jnp.dot(v.astype(BF16), mavg_ref[...], preferred_element_type=F32)
    vc = v - mu
    var = jnp.dot((vc * vc).astype(BF16), mavg_ref[...], preferred_element_type=F32)
    vn_ref[...] = (vc * lax.rsqrt(var + LN_EPS) * lng_ref[...]).astype(BF16)


def _inproj(x, mod3, g_pre, w_in_r, mavg, lng, *, tile):
    b, s, d = x.shape
    qkv_w = 3 * GROUP_W
    d1, d2 = DILATIONS[1], DILATIONS[2]
    grid = (b, s // tile)
    const = lambda bi, i: (0, 0)
    return pl.pallas_call(
        _inproj_kernel,
        grid=grid,
        in_specs=[
            pl.BlockSpec((None, tile, d), lambda bi, i: (bi, i, 0)),
            pl.BlockSpec((None, N_MOD, d), lambda bi, i: (bi, 0, 0)),
            pl.BlockSpec((1, d), const),
            pl.BlockSpec((d, IN_W), const),
            pl.BlockSpec((GMLP_W, GMLP_W), const),
            pl.BlockSpec((1, GMLP_W), const),
        ],
        out_specs=[
            pl.BlockSpec((None, tile, qkv_w), lambda bi, i: (bi, i, 0)),
            pl.BlockSpec((None, d1, tile // d1, qkv_w), lambda bi, i: (bi, 0, i, 0)),
            pl.BlockSpec((None, d2, tile // d2, qkv_w), lambda bi, i: (bi, 0, i, 0)),
            pl.BlockSpec((None, tile, GMLP_W), lambda bi, i: (bi, i, 0)),
            pl.BlockSpec((None, tile, GMLP_W), lambda bi, i: (bi, i, 0)),
        ],
        out_shape=[
            jax.ShapeDtypeStruct((b, s, qkv_w), BF16),
            jax.ShapeDtypeStruct((b, d1, s // d1, qkv_w), BF16),
            jax.ShapeDtypeStruct((b, d2, s // d2, qkv_w), BF16),
            jax.ShapeDtypeStruct((b, s, GMLP_W), BF16),
            jax.ShapeDtypeStruct((b, s, GMLP_W), BF16),
        ],
        scratch_shapes=[pltpu.VMEM((qkv_w // LANES, tile, LANES), F32)],
        compiler_params=_cparams(2),
        name="inproj",
    )(x, mod3, g_pre, w_in_r, mavg, lng)


def _attn_bias(dil, kw):
    group = DILATIONS.index(dil)
    n_heads = N_GROUPS * HEADS_PER_GROUP
    slopes = 2.0 ** (-8.0 * np.arange(1, n_heads + 1, dtype=np.float32) / n_heads)
    slopes = slopes[group * HEADS_PER_GROUP:(group + 1) * HEADS_PER_GROUP].astype(np.float32)
    n_var = 3 if kw == 2 * Q_BLOCK else 1
    i = np.arange(Q_BLOCK)[:, None]
    c = np.arange(kw)[None, :]
    out = np.empty((n_var, HEADS_PER_GROUP, Q_BLOCK, kw), np.float32)
    for v in range(n_var):
        rel = np.abs(c - N_SIDE * v - i)
        dist = (dil * rel).astype(np.float32)
        for h in range(HEADS_PER_GROUP):
            out[v, h] = np.where(rel <= N_SIDE, -slopes[h] * dist, np.float32(NEG_INF))
    return out


def _attn_kernel(qkv_ref, bias_ref, o_ref, l_ref, *, dil, nq, seq, kw):
    t = pl.program_id(1)

    def body(idx, carry):
        r = idx // nq
        j = idx % nq
        qs = pl.multiple_of((t * nq + j) * Q_BLOCK, Q_BLOCK)
        ks = pl.multiple_of(jnp.clip(qs - N_SIDE, 0, seq - kw), N_SIDE)
        var = (qs - ks) // N_SIDE
        q = qkv_ref[r, pl.ds(qs, Q_BLOCK), 0:GROUP_W]
        k = qkv_ref[r, pl.ds(ks, kw), GROUP_W:2 * GROUP_W]
        v = qkv_ref[r, pl.ds(ks, kw), 2 * GROUP_W:3 * GROUP_W]
        outs, lses = [], []
        for h in range(HEADS_PER_GROUP):
            sl = slice(h * HEAD_DIM, (h + 1) * HEAD_DIM)
            s = lax.dot_general(q[:, sl], k[:, sl], (((1,), (1,)), ((), ())),
                                preferred_element_type=F32)
            bias = bias_ref[var, h]
            s = jnp.where(bias > 0.5 * NEG_INF, s + bias, NEG_INF)
            m = jnp.max(s, axis=-1, keepdims=True)
            p = jnp.exp(s - m)
            l = jnp.sum(p, axis=-1, keepdims=True)
            o = jnp.dot(p.astype(BF16), v[:, sl], preferred_element_type=F32)
            outs.append(o * (1.0 / l))
            lses.append(jnp.broadcast_to(m + jnp.log(l), (Q_BLOCK, HEAD_DIM)))
        o_cat = jnp.concatenate(outs, axis=-1)
        l_cat = jnp.concatenate(lses, axis=-1)
        row0 = j * (Q_BLOCK * dil) + r
        rows = pl.ds(row0, Q_BLOCK) if dil == 1 else pl.ds(row0, Q_BLOCK, stride=dil)
        for sb in range(GROUP_W // LANES):
            o_ref[sb, rows, :] = o_cat[:, sb * LANES:(sb + 1) * LANES]
            l_ref[sb, rows, :] = l_cat[:, sb * LANES:(sb + 1) * LANES]
        return carry

    lax.fori_loop(0, dil * nq, body, 0)


def _attn(qkv, dil, *, tile):
    b, d, seq, w = qkv.shape
    s = d * seq
    kw = min(2 * Q_BLOCK, seq)
    nq = tile // (Q_BLOCK * dil)
    bias = jnp.asarray(_attn_bias(dil, kw))
    n_slab = GROUP_W // LANES
    out_sds = jax.ShapeDtypeStruct((b, n_slab, s, LANES), F32)
    out_spec = pl.BlockSpec((None, n_slab, tile, LANES), lambda bi, i: (bi, 0, i, 0))
    return pl.pallas_call(
        functools.partial(_attn_kernel, dil=dil, nq=nq, seq=seq, kw=kw),
        grid=(b, s // tile),
        in_specs=[
            pl.BlockSpec((None, d, seq, w), lambda bi, i: (bi, 0, 0, 0)),
            pl.BlockSpec(bias.shape, lambda bi, i: (0, 0, 0, 0)),
        ],
        out_specs=[out_spec, out_spec],
        out_shape=[out_sds, out_sds],
        compiler_params=_cparams(2),
        name=f"attn_d{dil}",
    )(qkv, bias)


def _mix_kernel(x_ref, mod_ref, o0_ref, l0_ref, o1_ref, l1_ref, o2_ref, l2_ref, u_ref, vn_ref,
                wcat_ref, bsm_ref, wout_ref, g_ref, y_ref):
    t = x_ref.shape[0]

    def slabs(ref):
        return jnp.concatenate([ref[sb] for sb in range(GROUP_W // LANES)], axis=-1)

    lses = [slabs(r) for r in (l0_ref, l1_ref, l2_ref)]
    m = jnp.maximum(jnp.maximum(lses[0], lses[1]), lses[2])
    es = [jnp.exp(l - m) for l in lses]
    inv = 1.0 / (es[0] + es[1] + es[2])
    parts = [(es[gi] * inv * slabs(r)).astype(BF16)
             for gi, r in enumerate((o0_ref, o1_ref, o2_ref))]

    lane_group = lax.broadcasted_iota(jnp.int32, (GMLP_CHUNK, GMLP_W), 1) // GMLP_GROUP
    n_gm = GMLP_W // GMLP_GROUP
    gms = []
    for c in range(t // GMLP_CHUNK):
        rows = slice(c * GMLP_CHUNK, (c + 1) * GMLP_CHUNK)
        vn = vn_ref[rows, :]
        stacked = jnp.concatenate(
            [jnp.where(lane_group == g, vn, jnp.zeros_like(vn)) for g in range(n_gm)], axis=0)
        sv = jnp.dot(wcat_ref[...], stacked, preferred_element_type=F32) + bsm_ref[...]
        gms.append((u_ref[rows, :].astype(F32) * sv).astype(BF16))
    parts.append(jnp.concatenate(gms, axis=0))

    mix = jnp.dot(jnp.concatenate(parts, axis=-1), wout_ref[...], preferred_element_type=F32)
    gt1 = mod_ref[2:3, :]
    y_ref[...] = x_ref[...] + gt1 * (_rms(mix) * g_ref[...])


def _mix(x, mod3, o0, l0, o1, l1, o2, l2, u, vn, wcat, bsm, w_out, g_post, *, tile):
    b, s, d = x.shape
    n_slab = GROUP_W // LANES
    tok = lambda w: pl.BlockSpec((None, tile, w), lambda bi, i: (bi, i, 0))
    slab = pl.BlockSpec((None, n_slab, tile, LANES), lambda bi, i: (bi, 0, i, 0))
    const = lambda bi, i: (0, 0)
    return pl.pallas_call(
        _mix_kernel,
        grid=(b, s // tile),
        in_specs=[
            tok(d),
            pl.BlockSpec((None, N_MOD, d), lambda bi, i: (bi, 0, 0)),
            slab, slab, slab, slab, slab, slab,
            tok(GMLP_W), tok(GMLP_W),
            pl.BlockSpec(wcat.shape, const),
            pl.BlockSpec(bsm.shape, const),
            pl.BlockSpec(w_out.shape, const),
            pl.BlockSpec((1, d), const),
        ],
        out_specs=tok(d),
        out_shape=jax.ShapeDtypeStruct((b, s, d), F32),
        compiler_params=_cparams(2),
        name="mix",
    )(x, mod3, o0, l0, o1, l1, o2, l2, u, vn, wcat, bsm, w_out, g_post)


def _ffn_kernel(x_ref, mod_ref, gpre_ref, wgu_ref, wdown_ref, gpost_ref, y_ref):
    x = x_ref[...]
    sh2 = mod_ref[3:4, :]
    sc2 = mod_ref[4:5, :]
    gt2 = mod_ref[5:6, :]
    h = (_rms(x) * (gpre_ref[...] * (1.0 + sc2)) + sh2).astype(BF16)
    acc = None
    for c in range(D_FF // FF_CHUNK):
        gu = jnp.dot(h, wgu_ref[:, 2 * FF_CHUNK * c:2 * FF_CHUNK * (c + 1)],
                     preferred_element_type=F32)
        gate = gu[:, :FF_CHUNK]
        up = gu[:, FF_CHUNK:]
        act = (gate * jax.nn.sigmoid(gate) * up).astype(BF16)
        part = jnp.dot(act, wdown_ref[FF_CHUNK * c:FF_CHUNK * (c + 1), :],
                       preferred_element_type=F32)
        acc = part if acc is None else acc + part
    y_ref[...] = x + gt2 * (_rms(acc) * gpost_ref[...])


def _ffn(x, mod3, g_pre, w_gu_r, w_down, g_post, *, tile):
    b, s, d = x.shape
    tok = pl.BlockSpec((None, tile, d), lambda bi, i: (bi, i, 0))
    const = lambda bi, i: (0, 0)
    return pl.pallas_call(
        _ffn_kernel,
        grid=(b, s // tile),
        in_specs=[
            tok,
            pl.BlockSpec((None, N_MOD, d), lambda bi, i: (bi, 0, 0)),
            pl.BlockSpec((1, d), const),
            pl.BlockSpec(w_gu_r.shape, const),
            pl.BlockSpec(w_down.shape, const),
            pl.BlockSpec((1, d), const),
        ],
        out_specs=tok,
        out_shape=jax.ShapeDtypeStruct((b, s, d), F32),
        compiler_params=_cparams(2),
        name="ffn",
    )(x, mod3, g_pre, w_gu_r, w_down, g_post)


def _prep_layer(w_in, w_s, b_s, g_gmlp, w_out, w_gu, w_down):
    cols = []
    for gi in range(N_GROUPS):
        lo = gi * GROUP_W
        cols.append(w_in[:, lo:lo + GROUP_W] * (HEAD_DIM ** -0.5))
        cols.append(w_in[:, ATTN_W + lo:ATTN_W + lo + GROUP_W])
        cols.append(w_in[:, 2 * ATTN_W + lo:2 * ATTN_W + lo + GROUP_W])
    cols.append(w_in[:, 3 * ATTN_W:])
    w_in_r = jnp.concatenate(cols, axis=1).astype(BF16)
    n_gm = GMLP_W // GMLP_GROUP
    wcat = jnp.transpose(w_s, (1, 0, 2)).reshape(GMLP_CHUNK, n_gm * GMLP_CHUNK).astype(BF16)
    bsm = jnp.repeat(b_s.T, GMLP_GROUP, axis=1).astype(F32)
    grp = np.arange(GMLP_W) // GMLP_GROUP
    mavg = jnp.asarray((grp[:, None] == grp[None, :]).astype(np.float32) / GMLP_GROUP, BF16)
    n_ch = D_FF // FF_CHUNK
    w_gu_r = jnp.concatenate(
        [w_gu[:, :D_FF].reshape(D_MODEL, n_ch, FF_CHUNK),
         w_gu[:, D_FF:].reshape(D_MODEL, n_ch, FF_CHUNK)], axis=2,
    ).reshape(D_MODEL, 2 * D_FF).astype(BF16)
    return dict(w_in_r=w_in_r, wcat=wcat, bsm=bsm, mavg=mavg, lng=g_gmlp.reshape(1, GMLP_W),
                w_out=w_out.astype(BF16), w_gu_r=w_gu_r, w_down=w_down.astype(BF16))


def _encoder_layer(x, mod3, p, g_pre_mix, g_post_mix, g_pre_ffn, g_post_ffn):
    s = x.shape[1]
    row = lambda g: g.reshape(1, D_MODEL)
    q0, q1, q2, u, vn = _inproj(x, mod3, row(g_pre_mix), p["w_in_r"], p["mavg"], p["lng"], tile=512)
    attn_tile = min(s, Q_BLOCK * DILATIONS[-1])
    q0 = q0.reshape(q0.shape[0], 1, s, q0.shape[-1])
    o0, l0 = _attn(q0, 1, tile=attn_tile)
    o1, l1 = _attn(q1, DILATIONS[1], tile=attn_tile)
    o2, l2 = _attn(q2, DILATIONS[2], tile=attn_tile)
    x1 = _mix(x, mod3, o0, l0, o1, l1, o2, l2, u, vn, p["wcat"], p["bsm"], p["w_out"],
              row(g_post_mix), tile=512)
    return _ffn(x1, mod3, row(g_pre_ffn), p["w_gu_r"], p["w_down"], row(g_post_ffn), tile=512)


def kernel(x_prompt, x_sample, c_prompt, c_sample, w_ada, b_ada, g_pre_mix, w_in, w_s, b_s, g_gmlp, w_out, g_post_mix, g_pre_ffn, w_gu, w_down, g_post_ffn):
    n_p = c_prompt.shape[0]
    c_all = jnp.concatenate([c_prompt, c_sample], axis=0)
    y_prompt, y_sample = x_prompt, x_sample
    for l in range(w_ada.shape[0]):
        mod3 = _adaln(c_all, w_ada[l], b_ada[l]).reshape(c_all.shape[0], N_MOD, D_MODEL)
        p = _prep_layer(w_in[l], w_s[l], b_s[l], g_gmlp[l], w_out[l], w_gu[l], w_down[l])
        gains = (g_pre_mix[l], g_post_mix[l], g_pre_ffn[l], g_post_ffn[l])
        y_prompt = _encoder_layer(y_prompt, mod3[:n_p], p, *gains)
        y_sample = _encoder_layer(y_sample, mod3[n_p:], p, *gains)
    return (y_prompt, y_sample)
```

```python
import functools

import numpy as np
import jax
import jax.numpy as jnp
from jax import lax
from jax.experimental import pallas as pl
from jax.experimental.pallas import tpu as pltpu

F32 = jnp.float32
BF16 = jnp.bfloat16

D_MODEL = 1024
HEAD_DIM = 64
HEADS_PER_GROUP = 4
GROUP_W = HEADS_PER_GROUP * HEAD_DIM
DILATIONS = (1, 4, 16)
N_SIDE = 64
N_GROUPS = len(DILATIONS)
ATTN_W = N_GROUPS * GROUP_W
GMLP_W = 256
GMLP_GROUP = 64
GMLP_CHUNK = 128
IN_W = 3 * ATTN_W + 2 * GMLP_W
D_FF = 2816
N_MOD = 6
RMS_EPS = 1e-6
LN_EPS = 1e-5
NEG_INF = -1e30

LANES = 128
Q_BLOCK = 128
FF_CHUNK = 256
VMEM_LIMIT = 56 * 1024 * 1024


def _cparams(n_axes):
    return pltpu.CompilerParams(
        dimension_semantics=("arbitrary",) * n_axes, vmem_limit_bytes=VMEM_LIMIT)


def _adaln_kernel(c_ref, w_ref, b_ref, o_ref):
    c = c_ref[...]
    a = (c * jax.nn.sigmoid(c)).astype(BF16)
    o_ref[...] = jnp.dot(a, w_ref[...].astype(BF16), preferred_element_type=F32) + b_ref[...]


def _adaln(c, w_ada, b_ada):
    n, d = c.shape
    nout = w_ada.shape[1]
    tn = 1024
    return pl.pallas_call(
        _adaln_kernel,
        grid=(nout // tn,),
        in_specs=[
            pl.BlockSpec((n, d), lambda j: (0, 0)),
            pl.BlockSpec((d, tn), lambda j: (0, j)),
            pl.BlockSpec((1, tn), lambda j: (0, j)),
        ],
        out_specs=pl.BlockSpec((n, tn), lambda j: (0, j)),
        out_shape=jax.ShapeDtypeStruct((n, nout), F32),
        compiler_params=_cparams(1),
        name="adaln",
    )(c, w_ada, b_ada.reshape(1, nout))


def _rms(x):
    return x * lax.rsqrt(jnp.mean(x * x, axis=-1, keepdims=True) + RMS_EPS)


def _inproj_kernel(x_ref, mod_ref, g_ref, w_ref, mavg_ref, lng_ref,
                   q0_ref, q1_ref, q2_ref, u_ref, vn_ref, slab_ref):
    t = x_ref.shape[0]
    sh1 = mod_ref[0:1, :]
    sc1 = mod_ref[1:2, :]
    h = (_rms(x_ref[...]) * (g_ref[...] * (1.0 + sc1)) + sh1).astype(BF16)

    qkv_w = 3 * GROUP_W
    q0_ref[...] = jnp.dot(h, w_ref[:, 0:qkv_w], preferred_element_type=F32).astype(BF16)
    n_slab = qkv_w // LANES
    for gi, out_ref in ((1, q1_ref), (2, q2_ref)):
        d = DILATIONS[gi]
        p = jnp.dot(h, w_ref[:, gi * qkv_w:(gi + 1) * qkv_w], preferred_element_type=F32)
        for j in range(n_slab):
            slab_ref[j] = p[:, j * LANES:(j + 1) * LANES]
        for r in range(d):
            for j in range(n_slab):
                out_ref[r, :, j * LANES:(j + 1) * LANES] = (
                    slab_ref[j, pl.ds(r, t // d, stride=d), :].astype(BF16))

    base = N_GROUPS * qkv_w
    gu = jnp.dot(h, w_ref[:, base:base + GMLP_W], preferred_element_type=F32)
    gv = jnp.dot(h, w_ref[:, base + GMLP_W:base + 2 * GMLP_W], preferred_element_type=F32)
    u_ref[...] = jax.nn.gelu(gu).astype(BF16)
    v = jax.nn.gelu(gv)
    mu = jnp.dot(v.astype(BF16), mavg_ref[...], preferred_element_type=F32)
    vc = v - mu
    var = jnp.dot((vc * vc).astype(BF16), mavg_ref[...], preferred_element_type=F32)
    vn_ref[...] = (vc * lax.rsqrt(var + LN_EPS) * lng_ref[...]).astype(BF16)


def _inproj(x, mod3, g_pre, w_in_r, mavg, lng, *, tile):
    b, s, d = x.shape
    qkv_w = 3 * GROUP_W
    d1, d2 = DILATIONS[1], DILATIONS[2]
    grid = (b, s // tile)
    const = lambda bi, i: (0, 0)
    return pl.pallas_call(
        _inproj_kernel,
        grid=grid,
        in_specs=[
            pl.BlockSpec((None, tile, d), lambda bi, i: (bi, i, 0)),
            pl.BlockSpec((None, N_MOD, d), lambda bi, i: (bi, 0, 0)),
            pl.BlockSpec((1, d), const),
            pl.BlockSpec((d, IN_W), const),
            pl.BlockSpec((GMLP_W, GMLP_W), const),
            pl.BlockSpec((1, GMLP_W), const),
        ],
        out_specs=[
            pl.BlockSpec((None, tile, qkv_w), lambda bi, i: (bi, i, 0)),
            pl.BlockSpec((None, d1, tile // d1, qkv_w), lambda bi, i: (bi, 0, i, 0)),
            pl.BlockSpec((None, d2, tile // d2, qkv_w), lambda bi, i: (bi, 0, i, 0)),
            pl.BlockSpec((None, tile, GMLP_W), lambda bi, i: (bi, i, 0)),
            pl.BlockSpec((None, tile, GMLP_W), lambda bi, i: (bi, i, 0)),
        ],
        out_shape=[
            jax.ShapeDtypeStruct((b, s, qkv_w), BF16),
            jax.ShapeDtypeStruct((b, d1, s // d1, qkv_w), BF16),
            jax.ShapeDtypeStruct((b, d2, s // d2, qkv_w), BF16),
            jax.ShapeDtypeStruct((b, s, GMLP_W), BF16),
            jax.ShapeDtypeStruct((b, s, GMLP_W), BF16),
        ],
        scratch_shapes=[pltpu.VMEM((qkv_w // LANES, tile, LANES), F32)],
        compiler_params=_cparams(2),
        name="inproj",
    )(x, mod3, g_pre, w_in_r, mavg, lng)


def _attn_bias(dil, kw):
    group = DILATIONS.index(dil)
    n_heads = N_GROUPS * HEADS_PER_GROUP
    slopes = 2.0 ** (-8.0 * np.arange(1, n_heads + 1, dtype=np.float32) / n_heads)
    slopes = slopes[group * HEADS_PER_GROUP:(group + 1) * HEADS_PER_GROUP].astype(np.float32)
    n_var = 3 if kw == 2 * Q_BLOCK else 1
    i = np.arange(Q_BLOCK)[:, None]
    c = np.arange(kw)[None, :]
    out = np.empty((n_var, HEADS_PER_GROUP, Q_BLOCK, kw), np.float32)
    for v in range(n_var):
        rel = np.abs(c - N_SIDE * v - i)
        dist = (dil * rel).astype(np.float32)
        for h in range(HEADS_PER_GROUP):
            out[v, h] = np.where(rel <= N_SIDE, -slopes[h] * dist, np.float32(NEG_INF))
    return out


def _attn_kernel(qkv_ref, bias_ref, o_ref, l_ref, s_buf0, s_buf1, m_buf0, m_buf1,
                 *, dil, nq, seq, kw):
    t = pl.program_id(1)
    n_blocks = dil * nq
    assert n_blocks % 2 == 0
    s_bufs = (s_buf0, s_buf1)
    m_bufs = (m_buf0, m_buf1)
    heads_per_tile = LANES // HEAD_DIM
    half = lax.broadcasted_iota(jnp.int32, (Q_BLOCK, LANES), 1) // HEAD_DIM

    def coords(n):
        r = n // nq
        j = n % nq
        qs = pl.multiple_of((t * nq + j) * Q_BLOCK, Q_BLOCK)
        ks = pl.multiple_of(jnp.clip(qs - N_SIDE, 0, seq - kw), N_SIDE)
        return r, j, qs, ks

    def scores(n, slot):
        r, _, qs, ks = coords(n)
        s_buf, m_buf = s_bufs[slot], m_bufs[slot]
        var = (qs - ks) // N_SIDE
        for h in range(HEADS_PER_GROUP):
            tile = slice((h // heads_per_tile) * LANES, (h // heads_per_tile + 1) * LANES)
            q = qkv_ref[r, pl.ds(qs, Q_BLOCK), tile]
            q = jnp.where(half == h % heads_per_tile, q, jnp.zeros_like(q))
            k = qkv_ref[r, pl.ds(ks, kw), GROUP_W + tile.start:GROUP_W + tile.stop]
            s = lax.dot_general(q, k, (((1,), (1,)), ((), ())), preferred_element_type=F32)
            bias = bias_ref[var, h]
            s = jnp.where(bias > 0.5 * NEG_INF, s + bias, NEG_INF)
            s_buf[h] = s
            m_buf[h] = jnp.broadcast_to(jnp.max(s, axis=-1, keepdims=True), (Q_BLOCK, LANES))

    def output(n, slot):
        r, j, _, ks = coords(n)
        s_buf, m_buf = s_bufs[slot], m_bufs[slot]
        row0 = j * (Q_BLOCK * dil) + r
        rows = pl.ds(row0, Q_BLOCK) if dil == 1 else pl.ds(row0, Q_BLOCK, stride=dil)
        for tl in range(GROUP_W // LANES):
            o_pair, lse_pair = [], []
            for hp in range(heads_per_tile):
                h = tl * heads_per_tile + hp
                m = m_buf[h]
                p = [jnp.exp(s_buf[h, :, c * LANES:(c + 1) * LANES] - m)
                     for c in range(kw // LANES)]
                psum = p[0] if len(p) == 1 else p[0] + p[1]
                l = jnp.sum(psum, axis=-1, keepdims=True)
                pb = jnp.concatenate(p, axis=-1).astype(BF16)
                v = qkv_ref[r, pl.ds(ks, kw), 2 * GROUP_W + tl * LANES:2 * GROUP_W + (tl + 1) * LANES]
                o_pair.append(jnp.dot(pb, v, preferred_element_type=F32) * (1.0 / l))
                lse_pair.append(m + jnp.log(l))
            o_ref[tl, rows, :] = jnp.where(half == 0, o_pair[0], o_pair[1])
            l_ref[tl, rows, :] = jnp.where(half == 0, lse_pair[0], lse_pair[1])

    scores(0, 0)

    def body(i, carry):
        n = 2 * i
        scores(n + 1, 1)
        output(n, 0)
        scores(n + 2, 0)
        output(n + 1, 1)
        return carry

    lax.fori_loop(0, n_blocks // 2 - 1, body, 0)
    scores(n_blocks - 1, 1)
    output(n_blocks - 2, 0)
    output(n_blocks - 1, 1)


def _attn(qkv, dil, *, tile):
    b, d, seq, w = qkv.shape
    s = d * seq
    kw = min(2 * Q_BLOCK, seq)
    nq = tile // (Q_BLOCK * dil)
    bias = jnp.asarray(_attn_bias(dil, kw))
    n_slab = GROUP_W // LANES
    out_sds = jax.ShapeDtypeStruct((b, n_slab, s, LANES), F32)
    out_spec = pl.BlockSpec((None, n_slab, tile, LANES), lambda bi, i: (bi, 0, i, 0))
    return pl.pallas_call(
        functools.partial(_attn_kernel, dil=dil, nq=nq, seq=seq, kw=kw),
        grid=(b, s // tile),
        in_specs=[
            pl.BlockSpec((None, d, seq, w), lambda bi, i: (bi, 0, 0, 0)),
            pl.BlockSpec(bias.shape, lambda bi, i: (0, 0, 0, 0)),
        ],
        out_specs=[out_spec, out_spec],
        out_shape=[out_sds, out_sds],
        scratch_shapes=[
            pltpu.VMEM((HEADS_PER_GROUP, Q_BLOCK, kw), F32),
            pltpu.VMEM((HEADS_PER_GROUP, Q_BLOCK, kw), F32),
            pltpu.VMEM((HEADS_PER_GROUP, Q_BLOCK, LANES), F32),
            pltpu.VMEM((HEADS_PER_GROUP, Q_BLOCK, LANES), F32),
        ],
        compiler_params=_cparams(2),
        name=f"attn_d{dil}",
    )(qkv, bias)


def _mix_kernel(x_ref, mod_ref, o0_ref, l0_ref, o1_ref, l1_ref, o2_ref, l2_ref, u_ref, vn_ref,
                wcat_ref, bsm_ref, wout_ref, g_ref, y_ref):
    t = x_ref.shape[0]

    def slabs(ref):
        return jnp.concatenate([ref[sb] for sb in range(GROUP_W // LANES)], axis=-1)

    lses = [slabs(r) for r in (l0_ref, l1_ref, l2_ref)]
    m = jnp.maximum(jnp.maximum(lses[0], lses[1]), lses[2])
    es = [jnp.exp(l - m) for l in lses]
    inv = 1.0 / (es[0] + es[1] + es[2])
    parts = [(es[gi] * inv * slabs(r)).astype(BF16)
             for gi, r in enumerate((o0_ref, o1_ref, o2_ref))]

    lane_group = lax.broadcasted_iota(jnp.int32, (GMLP_CHUNK, GMLP_W), 1) // GMLP_GROUP
    n_gm = GMLP_W // GMLP_GROUP
    gms = []
    for c in range(t // GMLP_CHUNK):
        rows = slice(c * GMLP_CHUNK, (c + 1) * GMLP_CHUNK)
        vn = vn_ref[rows, :]
        stacked = jnp.concatenate(
            [jnp.where(lane_group == g, vn, jnp.zeros_like(vn)) for g in range(n_gm)], axis=0)
        sv = jnp.dot(wcat_ref[...], stacked, preferred_element_type=F32) + bsm_ref[...]
        gms.append((u_ref[rows, :].astype(F32) * sv).astype(BF16))
    parts.append(jnp.concatenate(gms, axis=0))

    mix = jnp.dot(jnp.concatenate(parts, axis=-1), wout_ref[...], preferred_element_type=F32)
    gt1 = mod_ref[2:3, :]
    y_ref[...] = x_ref[...] + gt1 * (_rms(mix) * g_ref[...])


def _mix(x, mod3, o0, l0, o1, l1, o2, l2, u, vn, wcat, bsm, w_out, g_post, *, tile):
    b, s, d = x.shape
    n_slab = GROUP_W // LANES
    tok = lambda w: pl.BlockSpec((None, tile, w), lambda bi, i: (bi, i, 0))
    slab = pl.BlockSpec((None, n_slab, tile, LANES), lambda bi, i: (bi, 0, i, 0))
    const = lambda bi, i: (0, 0)
    return pl.pallas_call(
        _mix_kernel,
        grid=(b, s // tile),
        in_specs=[
            tok(d),
            pl.BlockSpec((None, N_MOD, d), lambda bi, i: (bi, 0, 0)),
            slab, slab, slab, slab, slab, slab,
            tok(GMLP_W), tok(GMLP_W),
            pl.BlockSpec(wcat.shape, const),
            pl.BlockSpec(bsm.shape, const),
            pl.BlockSpec(w_out.shape, const),
            pl.BlockSpec((1, d), const),
        ],
        out_specs=tok(d),
        out_shape=jax.ShapeDtypeStruct((b, s, d), F32),
        compiler_params=_cparams(2),
        name="mix",
    )(x, mod3, o0, l0, o1, l1, o2, l2, u, vn, wcat, bsm, w_out, g_post)


def _ffn_kernel(x_ref, mod_ref, gpre_ref, wgu_ref, wdown_ref, gpost_ref, y_ref):
    x = x_ref[...]
    sh2 = mod_ref[3:4, :]
    sc2 = mod_ref[4:5, :]
    gt2 = mod_ref[5:6, :]
    h = (_rms(x) * (gpre_ref[...] * (1.0 + sc2)) + sh2).astype(BF16)
    acc = None
    for c in range(D_FF // FF_CHUNK):
        gu = jnp.dot(h, wgu_ref[:, 2 * FF_CHUNK * c:2 * FF_CHUNK * (c + 1)],
                     preferred_element_type=F32)
        gate = gu[:, :FF_CHUNK]
        up = gu[:, FF_CHUNK:]
        act = (gate * jax.nn.sigmoid(gate) * up).astype(BF16)
        part = jnp.dot(act, wdown_ref[FF_CHUNK * c:FF_CHUNK * (c + 1), :],
                       preferred_element_type=F32)
        acc = part if acc is None else acc + part
    y_ref[...] = x + gt2 * (_rms(acc) * gpost_ref[...])


def _ffn(x, mod3, g_pre, w_gu_r, w_down, g_post, *, tile):
    b, s, d = x.shape
    tok = pl.BlockSpec((None, tile, d), lambda bi, i: (bi, i, 0))
    const = lambda bi, i: (0, 0)
    return pl.pallas_call(
        _ffn_kernel,
        grid=(b, s // tile),
        in_specs=[
            tok,
            pl.BlockSpec((None, N_MOD, d), lambda bi, i: (bi, 0, 0)),
            pl.BlockSpec((1, d), const),
            pl.BlockSpec(w_gu_r.shape, const),
            pl.BlockSpec(w_down.shape, const),
            pl.BlockSpec((1, d), const),
        ],
        out_specs=tok,
        out_shape=jax.ShapeDtypeStruct((b, s, d), F32),
        compiler_params=_cparams(2),
        name="ffn",
    )(x, mod3, g_pre, w_gu_r, w_down, g_post)


def _prep_layer(w_in, w_s, b_s, g_gmlp, w_out, w_gu, w_down):
    cols = []
    for gi in range(N_GROUPS):
        lo = gi * GROUP_W
        cols.append(w_in[:, lo:lo + GROUP_W] * (HEAD_DIM ** -0.5))
        cols.append(w_in[:, ATTN_W + lo:ATTN_W + lo + GROUP_W])
        cols.append(w_in[:, 2 * ATTN_W + lo:2 * ATTN_W + lo + GROUP_W])
    cols.append(w_in[:, 3 * ATTN_W:])
    w_in_r = jnp.concatenate(cols, axis=1).astype(BF16)
    n_gm = GMLP_W // GMLP_GROUP
    wcat = jnp.transpose(w_s, (1, 0, 2)).reshape(GMLP_CHUNK, n_gm * GMLP_CHUNK).astype(BF16)
    bsm = jnp.repeat(b_s.T, GMLP_GROUP, axis=1).astype(F32)
    grp = np.arange(GMLP_W) // GMLP_GROUP
    mavg = jnp.asarray((grp[:, None] == grp[None, :]).astype(np.float32) / GMLP_GROUP, BF16)
    n_ch = D_FF // FF_CHUNK
    w_gu_r = jnp.concatenate(
        [w_gu[:, :D_FF].reshape(D_MODEL, n_ch, FF_CHUNK),
         w_gu[:, D_FF:].reshape(D_MODEL, n_ch, FF_CHUNK)], axis=2,
    ).reshape(D_MODEL, 2 * D_FF).astype(BF16)
    return dict(w_in_r=w_in_r, wcat=wcat, bsm=bsm, mavg=mavg, lng=g_gmlp.reshape(1, GMLP_W),
                w_out=w_out.astype(BF16), w_gu_r=w_gu_r, w_down=w_down.astype(BF16))


def _encoder_layer(x, mod3, p, g_pre_mix, g_post_mix, g_pre_ffn, g_post_ffn):
    s = x.shape[1]
    row = lambda g: g.reshape(1, D_MODEL)
    q0, q1, q2, u, vn = _inproj(x, mod3, row(g_pre_mix), p["w_in_r"], p["mavg"], p["lng"], tile=512)
    attn_tile = min(s, Q_BLOCK * DILATIONS[-1])
    q0 = q0.reshape(q0.shape[0], 1, s, q0.shape[-1])
    o0, l0 = _attn(q0, 1, tile=attn_tile)
    o1, l1 = _attn(q1, DILATIONS[1], tile=attn_tile)
    o2, l2 = _attn(q2, DILATIONS[2], tile=attn_tile)
    x1 = _mix(x, mod3, o0, l0, o1, l1, o2, l2, u, vn, p["wcat"], p["bsm"], p["w_out"],
              row(g_post_mix), tile=512)
    return _ffn(x1, mod3, row(g_pre_ffn), p["w_gu_r"], p["w_down"], row(g_post_ffn), tile=512)


def kernel(x_prompt, x_sample, c_prompt, c_sample, w_ada, b_ada, g_pre_mix, w_in, w_s, b_s, g_gmlp, w_out, g_post_mix, g_pre_ffn, w_gu, w_down, g_post_ffn):
    n_p = c_prompt.shape[0]
    c_all = jnp.concatenate([c_prompt, c_sample], axis=0)
    y_prompt, y_sample = x_prompt, x_sample
    for l in range(w_ada.shape[0]):
        mod3 = _adaln(c_all, w_ada[l], b_ada[l]).reshape(c_all.shape[0], N_MOD, D_MODEL)
        p = _prep_layer(w_in[l], w_s[l], b_s[l], g_gmlp[l], w_out[l], w_gu[l], w_down[l])
        gains = (g_pre_mix[l], g_post_mix[l], g_pre_ffn[l], g_post_ffn[l])
        y_prompt = _encoder_layer(y_prompt, mod3[:n_p], p, *gains)
        y_sample = _encoder_layer(y_sample, mod3[n_p:], p, *gains)
    return (y_prompt, y_sample)
```

```python
import functools

import numpy as np
import jax
import jax.numpy as jnp
from jax import lax
from jax.experimental import pallas as pl
from jax.experimental.pallas import tpu as pltpu

F32 = jnp.float32
BF16 = jnp.bfloat16

D_MODEL = 1024
HEAD_DIM = 64
HEADS_PER_GROUP = 4
GROUP_W = HEADS_PER_GROUP * HEAD_DIM
DILATIONS = (1, 4, 16)
N_SIDE = 64
N_GROUPS = len(DILATIONS)
ATTN_W = N_GROUPS * GROUP_W
GMLP_W = 256
GMLP_GROUP = 64
GMLP_CHUNK = 128
IN_W = 3 * ATTN_W + 2 * GMLP_W
D_FF = 2816
N_MOD = 6
RMS_EPS = 1e-6
LN_EPS = 1e-5
NEG_INF = -1e30

LANES = 128
Q_BLOCK = 128
FF_CHUNK = 256
VMEM_LIMIT = 56 * 1024 * 1024


def _cparams(n_axes):
    return pltpu.CompilerParams(
        dimension_semantics=("arbitrary",) * n_axes, vmem_limit_bytes=VMEM_LIMIT)


def _adaln_kernel(c_ref, w_ref, b_ref, o_ref):
    c = c_ref[...]
    a = (c * jax.nn.sigmoid(c)).astype(BF16)
    o_ref[...] = jnp.dot(a, w_ref[...].astype(BF16), preferred_element_type=F32) + b_ref[...]


def _adaln(c, w_ada, b_ada):
    n, d = c.shape
    nout = w_ada.shape[1]
    tn = 1024
    return pl.pallas_call(
        _adaln_kernel,
        grid=(nout // tn,),
        in_specs=[
            pl.BlockSpec((n, d), lambda j: (0, 0)),
            pl.BlockSpec((d, tn), lambda j: (0, j)),
            pl.BlockSpec((1, tn), lambda j: (0, j)),
        ],
        out_specs=pl.BlockSpec((n, tn), lambda j: (0, j)),
        out_shape=jax.ShapeDtypeStruct((n, nout), F32),
        compiler_params=_cparams(1),
        name="adaln",
    )(c, w_ada, b_ada.reshape(1, nout))


def _rms(x):
    return x * lax.rsqrt(jnp.mean(x * x, axis=-1, keepdims=True) + RMS_EPS)


def _inproj_kernel(x_ref, mod_ref, g_ref, w_ref, mavg_ref, lng_ref,
                   q0_ref, q1_ref, q2_ref, u_ref, vn_ref, slab_ref):
    t = x_ref.shape[0]
    sh1 = mod_ref[0:1, :]
    sc1 = mod_ref[1:2, :]
    h = (_rms(x_ref[...]) * (g_ref[...] * (1.0 + sc1)) + sh1).astype(BF16)

    qkv_w = 3 * GROUP_W
    q0_ref[...] = jnp.dot(h, w_ref[:, 0:qkv_w], preferred_element_type=F32).astype(BF16)
    n_slab = qkv_w // LANES
    for gi, out_ref in ((1, q1_ref), (2, q2_ref)):
        d = DILATIONS[gi]
        p = jnp.dot(h, w_ref[:, gi * qkv_w:(gi + 1) * qkv_w], preferred_element_type=F32)
        for j in range(n_slab):
            slab_ref[j] = p[:, j * LANES:(j + 1) * LANES]
        for r in range(d):
            for j in range(n_slab):
                out_ref[r, :, j * LANES:(j + 1) * LANES] = (
                    slab_ref[j, pl.ds(r, t // d, stride=d), :].astype(BF16))

    base = N_GROUPS * qkv_w
    gu = jnp.dot(h, w_ref[:, base:base + GMLP_W], preferred_element_type=F32)
    gv = jnp.dot(h, w_ref[:, base + GMLP_W:base + 2 * GMLP_W], preferred_element_type=F32)
    u_ref[...] = jax.nn.gelu(gu).astype(BF16)
    v = jax.nn.gelu(gv)
    mu = jnp.dot(v.astype(BF16), mavg_ref[...], preferred_element_type=F32)
    vc = v - mu
    var = jnp.dot((vc * vc).astype(BF16), mavg_ref[...], preferred_element_type=F32)
    vn_ref[...] = (vc * lax.rsqrt(var + LN_EPS) * lng_ref[...]).astype(BF16)


def _inproj(x, mod3, g_pre, w_in_r, mavg, lng, *, tile):
    b, s, d = x.shape
    qkv_w = 3 * GROUP_W
    d1, d2 = DILATIONS[1], DILATIONS[2]
    grid = (b, s // tile)
    const = lambda bi, i: (0, 0)
    return pl.pallas_call(
        _inproj_kernel,
        grid=grid,
        in_specs=[
            pl.BlockSpec((None, tile, d), lambda bi, i: (bi, i, 0)),
            pl.BlockSpec((None, N_MOD, d), lambda bi, i: (bi, 0, 0)),
            pl.BlockSpec((1, d), const),
            pl.BlockSpec((d, IN_W), const),
            pl.BlockSpec((GMLP_W, GMLP_W), const),
            pl.BlockSpec((1, GMLP_W), const),
        ],
        out_specs=[
            pl.BlockSpec((None, tile, qkv_w), lambda bi, i: (bi, i, 0)),
            pl.BlockSpec((None, d1, tile // d1, qkv_w), lambda bi, i: (bi, 0, i, 0)),
            pl.BlockSpec((None, d2, tile // d2, qkv_w), lambda bi, i: (bi, 0, i, 0)),
            pl.BlockSpec((None, tile, GMLP_W), lambda bi, i: (bi, i, 0)),
            pl.BlockSpec((None, tile, GMLP_W), lambda bi, i: (bi, i, 0)),
        ],
        out_shape=[
            jax.ShapeDtypeStruct((b, s, qkv_w), BF16),
            jax.ShapeDtypeStruct((b, d1, s // d1, qkv_w), BF16),
            jax.ShapeDtypeStruct((b, d2, s // d2, qkv_w), BF16),
            jax.ShapeDtypeStruct((b, s, GMLP_W), BF16),
            jax.ShapeDtypeStruct((b, s, GMLP_W), BF16),
        ],
        scratch_shapes=[pltpu.VMEM((qkv_w // LANES, tile, LANES), F32)],
        compiler_params=_cparams(2),
        name="inproj",
    )(x, mod3, g_pre, w_in_r, mavg, lng)


def _attn_bias(dil, kw):
    group = DILATIONS.index(dil)
    n_heads = N_GROUPS * HEADS_PER_GROUP
    slopes = 2.0 ** (-8.0 * np.arange(1, n_heads + 1, dtype=np.float32) / n_heads)
    slopes = slopes[group * HEADS_PER_GROUP:(group + 1) * HEADS_PER_GROUP].astype(np.float32)
    n_var = 3 if kw == 2 * Q_BLOCK else 1
    i = np.arange(Q_BLOCK)[:, None]
    c = np.arange(kw)[None, :]
    out = np.empty((n_var, HEADS_PER_GROUP, Q_BLOCK, kw), np.float32)
    for v in range(n_var):
        rel = np.abs(c - N_SIDE * v - i)
        dist = (dil * rel).astype(np.float32)
        for h in range(HEADS_PER_GROUP):
            out[v, h] = np.where(rel <= N_SIDE, -slopes[h] * dist, np.float32(NEG_INF))
    return out


def _attn_kernel(qkv_ref, bias_ref, o_ref, l_ref, s_buf0, s_buf1, m_buf0, m_buf1,
                 *, dil, nq, seq, kw):
    t = pl.program_id(1)
    n_blocks = dil * nq
    assert n_blocks % 2 == 0
    s_bufs = (s_buf0, s_buf1)
    m_bufs = (m_buf0, m_buf1)
    heads_per_tile = LANES // HEAD_DIM
    half = lax.broadcasted_iota(jnp.int32, (Q_BLOCK, LANES), 1) // HEAD_DIM

    def coords(n):
        r = n // nq
        j = n % nq
        qs = pl.multiple_of((t * nq + j) * Q_BLOCK, Q_BLOCK)
        ks = pl.multiple_of(jnp.clip(qs - N_SIDE, 0, seq - kw), N_SIDE)
        return r, j, qs, ks

    def scores(n, slot):
        r, _, qs, ks = coords(n)
        s_buf, m_buf = s_bufs[slot], m_bufs[slot]
        var = (qs - ks) // N_SIDE
        for h in range(HEADS_PER_GROUP):
            tile = slice((h // heads_per_tile) * LANES, (h // heads_per_tile + 1) * LANES)
            q = qkv_ref[r, pl.ds(qs, Q_BLOCK), tile]
            q = jnp.where(half == h % heads_per_tile, q, jnp.zeros_like(q))
            k = qkv_ref[r, pl.ds(ks, kw), GROUP_W + tile.start:GROUP_W + tile.stop]
            s = lax.dot_general(q, k, (((1,), (1,)), ((), ())), preferred_element_type=F32)
            bias = bias_ref[var, h]
            s = jnp.where(bias > 0.5 * NEG_INF, s + bias, NEG_INF)
            s_buf[h] = s
            m_buf[h] = jnp.broadcast_to(jnp.max(s, axis=-1, keepdims=True), (Q_BLOCK, LANES))

    def output(n, slot):
        r, j, _, ks = coords(n)
        s_buf, m_buf = s_bufs[slot], m_bufs[slot]
        row0 = j * (Q_BLOCK * dil) + r
        rows = pl.ds(row0, Q_BLOCK) if dil == 1 else pl.ds(row0, Q_BLOCK, stride=dil)
        for tl in range(GROUP_W // LANES):
            o_pair, lse_pair = [], []
            for hp in range(heads_per_tile):
                h = tl * heads_per_tile + hp
                m = m_buf[h]
                p = [jnp.exp(s_buf[h, :, c * LANES:(c + 1) * LANES] - m)
                     for c in range(kw // LANES)]
                psum = p[0] if len(p) == 1 else p[0] + p[1]
                l = jnp.sum(psum, axis=-1, keepdims=True)
                pb = jnp.concatenate(p, axis=-1).astype(BF16)
                v = qkv_ref[r, pl.ds(ks, kw), 2 * GROUP_W + tl * LANES:2 * GROUP_W + (tl + 1) * LANES]
                o_pair.append(jnp.dot(pb, v, preferred_element_type=F32) * (1.0 / l))
                lse_pair.append(m + jnp.log(l))
            o_ref[tl, rows, :] = jnp.where(half == 0, o_pair[0], o_pair[1])
            l_ref[tl, rows, :] = jnp.where(half == 0, lse_pair[0], lse_pair[1])

    scores(0, 0)

    def body(i, carry):
        n = 2 * i
        scores(n + 1, 1)
        output(n, 0)
        scores(n + 2, 0)
        output(n + 1, 1)
        return carry

    lax.fori_loop(0, n_blocks // 2 - 1, body, 0)
    scores(n_blocks - 1, 1)
    output(n_blocks - 2, 0)
    output(n_blocks - 1, 1)


def _attn(qkv, dil, *, tile):
    b, d, seq, w = qkv.shape
    s = d * seq
    kw = min(2 * Q_BLOCK, seq)
    nq = tile // (Q_BLOCK * dil)
    bias = jnp.asarray(_attn_bias(dil, kw))
    n_slab = GROUP_W // LANES
    out_sds = jax.ShapeDtypeStruct((b, n_slab, s, LANES), F32)
    out_spec = pl.BlockSpec((None, n_slab, tile, LANES), lambda bi, i: (bi, 0, i, 0))
    return pl.pallas_call(
        functools.partial(_attn_kernel, dil=dil, nq=nq, seq=seq, kw=kw),
        grid=(b, s // tile),
        in_specs=[
            pl.BlockSpec((None, d, seq, w), lambda bi, i: (bi, 0, 0, 0)),
            pl.BlockSpec(bias.shape, lambda bi, i: (0, 0, 0, 0)),
        ],
        out_specs=[out_spec, out_spec],
        out_shape=[out_sds, out_sds],
        scratch_shapes=[
            pltpu.VMEM((HEADS_PER_GROUP, Q_BLOCK, kw), F32),
            pltpu.VMEM((HEADS_PER_GROUP, Q_BLOCK, kw), F32),
            pltpu.VMEM((HEADS_PER_GROUP, Q_BLOCK, LANES), F32),
            pltpu.VMEM((HEADS_PER_GROUP, Q_BLOCK, LANES), F32),
        ],
        compiler_params=_cparams(2),
        name=f"attn_d{dil}",
    )(qkv, bias)


def _mix_tile(x, gt1, o_refs, l_refs, u_ref, vn_ref, wcat_ref, bsm_ref, wout_ref, g_ref):
    t = x.shape[0]

    def slabs(ref):
        return jnp.concatenate([ref[sb] for sb in range(GROUP_W // LANES)], axis=-1)

    lses = [slabs(r) for r in l_refs]
    m = jnp.maximum(jnp.maximum(lses[0], lses[1]), lses[2])
    es = [jnp.exp(l - m) for l in lses]
    inv = 1.0 / (es[0] + es[1] + es[2])
    parts = [(es[gi] * inv * slabs(r)).astype(BF16) for gi, r in enumerate(o_refs)]

    lane_group = lax.broadcasted_iota(jnp.int32, (GMLP_CHUNK, GMLP_W), 1) // GMLP_GROUP
    n_gm = GMLP_W // GMLP_GROUP
    gms = []
    for c in range(t // GMLP_CHUNK):
        rows = slice(c * GMLP_CHUNK, (c + 1) * GMLP_CHUNK)
        vn = vn_ref[rows, :]
        stacked = jnp.concatenate(
            [jnp.where(lane_group == g, vn, jnp.zeros_like(vn)) for g in range(n_gm)], axis=0)
        sv = jnp.dot(wcat_ref[...], stacked, preferred_element_type=F32) + bsm_ref[...]
        gms.append((u_ref[rows, :].astype(F32) * sv).astype(BF16))
    parts.append(jnp.concatenate(gms, axis=0))

    mix = jnp.dot(jnp.concatenate(parts, axis=-1), wout_ref[...], preferred_element_type=F32)
    return x + gt1 * (_rms(mix) * g_ref[...])


def _ffn_tile(x, sh2, sc2, gt2, gpre_ref, wgu_ref, wdown_ref, gpost_ref):
    h = (_rms(x) * (gpre_ref[...] * (1.0 + sc2)) + sh2).astype(BF16)
    acc = None
    for c in range(D_FF // FF_CHUNK):
        gu = jnp.dot(h, wgu_ref[:, 2 * FF_CHUNK * c:2 * FF_CHUNK * (c + 1)],
                     preferred_element_type=F32)
        gate = gu[:, :FF_CHUNK]
        up = gu[:, FF_CHUNK:]
        act = (gate * jax.nn.sigmoid(gate) * up).astype(BF16)
        part = jnp.dot(act, wdown_ref[FF_CHUNK * c:FF_CHUNK * (c + 1), :],
                       preferred_element_type=F32)
        acc = part if acc is None else acc + part
    return x + gt2 * (_rms(acc) * gpost_ref[...])


def _post_kernel(x_ref, mod_ref, o0_ref, l0_ref, o1_ref, l1_ref, o2_ref, l2_ref, u_ref, vn_ref,
                 wcat_ref, bsm_ref, wout_ref, gmix_ref, gpre_ref, wgu_ref, wdown_ref, gpost_ref,
                 y_ref):
    x1 = _mix_tile(x_ref[...], mod_ref[2:3, :], (o0_ref, o1_ref, o2_ref),
                   (l0_ref, l1_ref, l2_ref), u_ref, vn_ref, wcat_ref, bsm_ref, wout_ref, gmix_ref)
    y_ref[...] = _ffn_tile(x1, mod_ref[3:4, :], mod_ref[4:5, :], mod_ref[5:6, :],
                           gpre_ref, wgu_ref, wdown_ref, gpost_ref)


def _post(x, mod3, o0, l0, o1, l1, o2, l2, u, vn, p, g_post_mix, g_pre_ffn, g_post_ffn, *, tile):
    b, s, d = x.shape
    n_slab = GROUP_W // LANES
    tok = lambda w: pl.BlockSpec((None, tile, w), lambda bi, i: (bi, i, 0))
    slab = pl.BlockSpec((None, n_slab, tile, LANES), lambda bi, i: (bi, 0, i, 0))
    const = lambda a: pl.BlockSpec(a.shape, lambda bi, i: (0,) * a.ndim)
    consts = (p["wcat"], p["bsm"], p["w_out"], g_post_mix, g_pre_ffn, p["w_gu_r"], p["w_down"],
              g_post_ffn)
    return pl.pallas_call(
        _post_kernel,
        grid=(b, s // tile),
        in_specs=[
            tok(d),
            pl.BlockSpec((None, N_MOD, d), lambda bi, i: (bi, 0, 0)),
            slab, slab, slab, slab, slab, slab,
            tok(GMLP_W), tok(GMLP_W),
        ] + [const(a) for a in consts],
        out_specs=tok(d),
        out_shape=jax.ShapeDtypeStruct((b, s, d), F32),
        compiler_params=_cparams(2),
        name="post",
    )(x, mod3, o0, l0, o1, l1, o2, l2, u, vn, *consts)


def _prep_layer(w_in, w_s, b_s, g_gmlp, w_out, w_gu, w_down):
    cols = []
    for gi in range(N_GROUPS):
        lo = gi * GROUP_W
        cols.append(w_in[:, lo:lo + GROUP_W] * (HEAD_DIM ** -0.5))
        cols.append(w_in[:, ATTN_W + lo:ATTN_W + lo + GROUP_W])
        cols.append(w_in[:, 2 * ATTN_W + lo:2 * ATTN_W + lo + GROUP_W])
    cols.append(w_in[:, 3 * ATTN_W:])
    w_in_r = jnp.concatenate(cols, axis=1).astype(BF16)
    n_gm = GMLP_W // GMLP_GROUP
    wcat = jnp.transpose(w_s, (1, 0, 2)).reshape(GMLP_CHUNK, n_gm * GMLP_CHUNK).astype(BF16)
    bsm = jnp.repeat(b_s.T, GMLP_GROUP, axis=1).astype(F32)
    grp = np.arange(GMLP_W) // GMLP_GROUP
    mavg = jnp.asarray((grp[:, None] == grp[None, :]).astype(np.float32) / GMLP_GROUP, BF16)
    n_ch = D_FF // FF_CHUNK
    w_gu_r = jnp.concatenate(
        [w_gu[:, :D_FF].reshape(D_MODEL, n_ch, FF_CHUNK),
         w_gu[:, D_FF:].reshape(D_MODEL, n_ch, FF_CHUNK)], axis=2,
    ).reshape(D_MODEL, 2 * D_FF).astype(BF16)
    return dict(w_in_r=w_in_r, wcat=wcat, bsm=bsm, mavg=mavg, lng=g_gmlp.reshape(1, GMLP_W),
                w_out=w_out.astype(BF16), w_gu_r=w_gu_r, w_down=w_down.astype(BF16))


def _encoder_layer(x, mod3, p, g_pre_mix, g_post_mix, g_pre_ffn, g_post_ffn):
    s = x.shape[1]
    row = lambda g: g.reshape(1, D_MODEL)
    q0, q1, q2, u, vn = _inproj(x, mod3, row(g_pre_mix), p["w_in_r"], p["mavg"], p["lng"], tile=512)
    attn_tile = min(s, Q_BLOCK * DILATIONS[-1])
    q0 = q0.reshape(q0.shape[0], 1, s, q0.shape[-1])
    o0, l0 = _attn(q0, 1, tile=attn_tile)
    o1, l1 = _attn(q1, DILATIONS[1], tile=attn_tile)
    o2, l2 = _attn(q2, DILATIONS[2], tile=attn_tile)
    return _post(x, mod3, o0, l0, o1, l1, o2, l2, u, vn, p, row(g_post_mix), row(g_pre_ffn),
                 row(g_post_ffn), tile=512)


def kernel(x_prompt, x_sample, c_prompt, c_sample, w_ada, b_ada, g_pre_mix, w_in, w_s, b_s, g_gmlp, w_out, g_post_mix, g_pre_ffn, w_gu, w_down, g_post_ffn):
    n_p = c_prompt.shape[0]
    c_all = jnp.concatenate([c_prompt, c_sample], axis=0)
    y_prompt, y_sample = x_prompt, x_sample
    for l in range(w_ada.shape[0]):
        mod3 = _adaln(c_all, w_ada[l], b_ada[l]).reshape(c_all.shape[0], N_MOD, D_MODEL)
        p = _prep_layer(w_in[l], w_s[l], b_s[l], g_gmlp[l], w_out[l], w_gu[l], w_down[l])
        gains = (g_pre_mix[l], g_post_mix[l], g_pre_ffn[l], g_post_ffn[l])
        y_prompt = _encoder_layer(y_prompt, mod3[:n_p], p, *gains)
        y_sample = _encoder_layer(y_sample, mod3[n_p:], p, *gains)
    return (y_prompt, y_sample)
```

```python
import functools

import numpy as np
import jax
import jax.numpy as jnp
from jax import lax
from jax.experimental import pallas as pl
from jax.experimental.pallas import tpu as pltpu

F32 = jnp.float32
BF16 = jnp.bfloat16

D_MODEL = 1024
HEAD_DIM = 64
HEADS_PER_GROUP = 4
GROUP_W = HEADS_PER_GROUP * HEAD_DIM
DILATIONS = (1, 4, 16)
N_SIDE = 64
N_GROUPS = len(DILATIONS)
ATTN_W = N_GROUPS * GROUP_W
GMLP_W = 256
GMLP_GROUP = 64
GMLP_CHUNK = 128
IN_W = 3 * ATTN_W + 2 * GMLP_W
D_FF = 2816
N_MOD = 6
RMS_EPS = 1e-6
LN_EPS = 1e-5
NEG_INF = -1e30

LANES = 128
Q_BLOCK = 128
FF_CHUNK = 256
VMEM_LIMIT = 56 * 1024 * 1024


def _cparams(n_axes):
    return pltpu.CompilerParams(
        dimension_semantics=("arbitrary",) * n_axes, vmem_limit_bytes=VMEM_LIMIT)


def _adaln_kernel(c_ref, w_ref, b_ref, o_ref):
    c = c_ref[...]
    a = (c * jax.nn.sigmoid(c)).astype(BF16)
    o_ref[...] = jnp.dot(a, w_ref[...].astype(BF16), preferred_element_type=F32) + b_ref[...]


def _adaln(c, w_ada, b_ada):
    n, d = c.shape
    nout = w_ada.shape[1]
    tn = 1024
    return pl.pallas_call(
        _adaln_kernel,
        grid=(nout // tn,),
        in_specs=[
            pl.BlockSpec((n, d), lambda j: (0, 0)),
            pl.BlockSpec((d, tn), lambda j: (0, j)),
            pl.BlockSpec((1, tn), lambda j: (0, j)),
        ],
        out_specs=pl.BlockSpec((n, tn), lambda j: (0, j)),
        out_shape=jax.ShapeDtypeStruct((n, nout), F32),
        compiler_params=_cparams(1),
        name="adaln",
    )(c, w_ada, b_ada.reshape(1, nout))


def _rms(x):
    return x * lax.rsqrt(jnp.mean(x * x, axis=-1, keepdims=True) + RMS_EPS)


def _inproj_kernel(x_ref, mod_ref, g_ref, w_ref, mavg_ref, lng_ref,
                   q0_ref, q1_ref, q2_ref, u_ref, vn_ref, slab_ref):
    t = x_ref.shape[0]
    sh1 = mod_ref[0:1, :]
    sc1 = mod_ref[1:2, :]
    h = (_rms(x_ref[...]) * (g_ref[...] * (1.0 + sc1)) + sh1).astype(BF16)

    qkv_w = 3 * GROUP_W
    n_slab = qkv_w // LANES

    def qkv(gi):
        return jnp.concatenate(
            [jnp.dot(h, w_ref[:, part * ATTN_W + gi * GROUP_W:part * ATTN_W + (gi + 1) * GROUP_W],
                     preferred_element_type=F32) for part in range(3)], axis=-1)

    q0_ref[...] = qkv(0).astype(BF16)
    for gi, out_ref in ((1, q1_ref), (2, q2_ref)):
        d = DILATIONS[gi]
        p = qkv(gi)
        for j in range(n_slab):
            slab_ref[j] = p[:, j * LANES:(j + 1) * LANES]
        for r in range(d):
            for j in range(n_slab):
                out_ref[r, :, j * LANES:(j + 1) * LANES] = (
                    slab_ref[j, pl.ds(r, t // d, stride=d), :].astype(BF16))

    base = 3 * ATTN_W
    gu = jnp.dot(h, w_ref[:, base:base + GMLP_W], preferred_element_type=F32)
    gv = jnp.dot(h, w_ref[:, base + GMLP_W:base + 2 * GMLP_W], preferred_element_type=F32)
    u_ref[...] = jax.nn.gelu(gu).astype(BF16)
    v = jax.nn.gelu(gv)
    mu = jnp.dot(v.astype(BF16), mavg_ref[...], preferred_element_type=F32)
    vc = v - mu
    var = jnp.dot((vc * vc).astype(BF16), mavg_ref[...], preferred_element_type=F32)
    vn_ref[...] = (vc * lax.rsqrt(var + LN_EPS) * lng_ref[...]).astype(BF16)


def _inproj(x, mod3, g_pre, w_in_r, mavg, lng, *, tile):
    b, s, d = x.shape
    qkv_w = 3 * GROUP_W
    d1, d2 = DILATIONS[1], DILATIONS[2]
    grid = (b, s // tile)
    const = lambda bi, i: (0, 0)
    return pl.pallas_call(
        _inproj_kernel,
        grid=grid,
        in_specs=[
            pl.BlockSpec((None, tile, d), lambda bi, i: (bi, i, 0)),
            pl.BlockSpec((None, N_MOD, d), lambda bi, i: (bi, 0, 0)),
            pl.BlockSpec((1, d), const),
            pl.BlockSpec((d, IN_W), const),
            pl.BlockSpec((GMLP_W, GMLP_W), const),
            pl.BlockSpec((1, GMLP_W), const),
        ],
        out_specs=[
            pl.BlockSpec((None, tile, qkv_w), lambda bi, i: (bi, i, 0)),
            pl.BlockSpec((None, d1, tile // d1, qkv_w), lambda bi, i: (bi, 0, i, 0)),
            pl.BlockSpec((None, d2, tile // d2, qkv_w), lambda bi, i: (bi, 0, i, 0)),
            pl.BlockSpec((None, tile, GMLP_W), lambda bi, i: (bi, i, 0)),
            pl.BlockSpec((None, tile, GMLP_W), lambda bi, i: (bi, i, 0)),
        ],
        out_shape=[
            jax.ShapeDtypeStruct((b, s, qkv_w), BF16),
            jax.ShapeDtypeStruct((b, d1, s // d1, qkv_w), BF16),
            jax.ShapeDtypeStruct((b, d2, s // d2, qkv_w), BF16),
            jax.ShapeDtypeStruct((b, s, GMLP_W), BF16),
            jax.ShapeDtypeStruct((b, s, GMLP_W), BF16),
        ],
        scratch_shapes=[pltpu.VMEM((qkv_w // LANES, tile, LANES), F32)],
        compiler_params=_cparams(2),
        name="inproj",
    )(x, mod3, g_pre, w_in_r, mavg, lng)


def _attn_bias(dil, kw):
    group = DILATIONS.index(dil)
    n_heads = N_GROUPS * HEADS_PER_GROUP
    slopes = 2.0 ** (-8.0 * np.arange(1, n_heads + 1, dtype=np.float32) / n_heads)
    slopes = slopes[group * HEADS_PER_GROUP:(group + 1) * HEADS_PER_GROUP].astype(np.float32)
    n_var = 3 if kw == 2 * Q_BLOCK else 1
    i = np.arange(Q_BLOCK)[:, None]
    c = np.arange(kw)[None, :]
    out = np.empty((n_var, HEADS_PER_GROUP, Q_BLOCK, kw), np.float32)
    for v in range(n_var):
        rel = np.abs(c - N_SIDE * v - i)
        dist = (dil * rel).astype(np.float32)
        for h in range(HEADS_PER_GROUP):
            out[v, h] = np.where(rel <= N_SIDE, -slopes[h] * dist, np.float32(NEG_INF))
    return out


def _attn_kernel(qkv_ref, bias_ref, o_ref, l_ref, s_buf0, s_buf1, m_buf0, m_buf1,
                 *, dil, nq, seq, kw):
    t = pl.program_id(1)
    n_blocks = dil * nq
    assert n_blocks % 2 == 0
    s_bufs = (s_buf0, s_buf1)
    m_bufs = (m_buf0, m_buf1)
    heads_per_tile = LANES // HEAD_DIM
    half = lax.broadcasted_iota(jnp.int32, (Q_BLOCK, LANES), 1) // HEAD_DIM

    def coords(n):
        r = n // nq
        j = n % nq
        qs = pl.multiple_of((t * nq + j) * Q_BLOCK, Q_BLOCK)
        ks = pl.multiple_of(jnp.clip(qs - N_SIDE, 0, seq - kw), N_SIDE)
        return r, j, qs, ks

    def scores(n, slot):
        r, _, qs, ks = coords(n)
        s_buf, m_buf = s_bufs[slot], m_bufs[slot]
        var = (qs - ks) // N_SIDE
        for h in range(HEADS_PER_GROUP):
            tile = slice((h // heads_per_tile) * LANES, (h // heads_per_tile + 1) * LANES)
            q = qkv_ref[r, pl.ds(qs, Q_BLOCK), tile]
            q = jnp.where(half == h % heads_per_tile, q, jnp.zeros_like(q))
            k = qkv_ref[r, pl.ds(ks, kw), GROUP_W + tile.start:GROUP_W + tile.stop]
            s = lax.dot_general(q, k, (((1,), (1,)), ((), ())), preferred_element_type=F32)
            bias = bias_ref[var, h]
            s = jnp.where(bias > 0.5 * NEG_INF, s + bias, NEG_INF)
            s_buf[h] = s
            m_buf[h] = jnp.broadcast_to(jnp.max(s, axis=-1, keepdims=True), (Q_BLOCK, LANES))

    def output(n, slot):
        r, j, _, ks = coords(n)
        s_buf, m_buf = s_bufs[slot], m_bufs[slot]
        row0 = j * (Q_BLOCK * dil) + r
        rows = pl.ds(row0, Q_BLOCK) if dil == 1 else pl.ds(row0, Q_BLOCK, stride=dil)
        for tl in range(GROUP_W // LANES):
            o_pair, lse_pair = [], []
            for hp in range(heads_per_tile):
                h = tl * heads_per_tile + hp
                m = m_buf[h]
                p = [jnp.exp(s_buf[h, :, c * LANES:(c + 1) * LANES] - m)
                     for c in range(kw // LANES)]
                psum = p[0] if len(p) == 1 else p[0] + p[1]
                l = jnp.sum(psum, axis=-1, keepdims=True)
                pb = jnp.concatenate(p, axis=-1).astype(BF16)
                v = qkv_ref[r, pl.ds(ks, kw), 2 * GROUP_W + tl * LANES:2 * GROUP_W + (tl + 1) * LANES]
                o_pair.append(jnp.dot(pb, v, preferred_element_type=F32) * (1.0 / l))
                lse_pair.append(m + jnp.log(l))
            o_ref[tl, rows, :] = jnp.where(half == 0, o_pair[0], o_pair[1])
            l_ref[tl, rows, :] = jnp.where(half == 0, lse_pair[0], lse_pair[1])

    scores(0, 0)

    def body(i, carry):
        n = 2 * i
        scores(n + 1, 1)
        output(n, 0)
        scores(n + 2, 0)
        output(n + 1, 1)
        return carry

    lax.fori_loop(0, n_blocks // 2 - 1, body, 0)
    scores(n_blocks - 1, 1)
    output(n_blocks - 2, 0)
    output(n_blocks - 1, 1)


def _attn(qkv, dil, *, tile):
    b, d, seq, w = qkv.shape
    s = d * seq
    kw = min(2 * Q_BLOCK, seq)
    nq = tile // (Q_BLOCK * dil)
    bias = jnp.asarray(_attn_bias(dil, kw))
    n_slab = GROUP_W // LANES
    out_sds = jax.ShapeDtypeStruct((b, n_slab, s, LANES), F32)
    out_spec = pl.BlockSpec((None, n_slab, tile, LANES), lambda bi, i: (bi, 0, i, 0))
    return pl.pallas_call(
        functools.partial(_attn_kernel, dil=dil, nq=nq, seq=seq, kw=kw),
        grid=(b, s // tile),
        in_specs=[
            pl.BlockSpec((None, d, seq, w), lambda bi, i: (bi, 0, 0, 0)),
            pl.BlockSpec(bias.shape, lambda bi, i: (0, 0, 0, 0)),
        ],
        out_specs=[out_spec, out_spec],
        out_shape=[out_sds, out_sds],
        scratch_shapes=[
            pltpu.VMEM((HEADS_PER_GROUP, Q_BLOCK, kw), F32),
            pltpu.VMEM((HEADS_PER_GROUP, Q_BLOCK, kw), F32),
            pltpu.VMEM((HEADS_PER_GROUP, Q_BLOCK, LANES), F32),
            pltpu.VMEM((HEADS_PER_GROUP, Q_BLOCK, LANES), F32),
        ],
        compiler_params=_cparams(2),
        name=f"attn_d{dil}",
    )(qkv, bias)


def _mix_tile(x, gt1, o_refs, l_refs, u_ref, vn_ref, wcat_ref, bsm_ref, wout_ref, g_ref):
    t = x.shape[0]

    def slabs(ref):
        return jnp.concatenate([ref[sb] for sb in range(GROUP_W // LANES)], axis=-1)

    lses = [slabs(r) for r in l_refs]
    m = jnp.maximum(jnp.maximum(lses[0], lses[1]), lses[2])
    es = [jnp.exp(l - m) for l in lses]
    inv = 1.0 / (es[0] + es[1] + es[2])
    parts = [(es[gi] * inv * slabs(r)).astype(BF16) for gi, r in enumerate(o_refs)]

    lane_group = lax.broadcasted_iota(jnp.int32, (GMLP_CHUNK, GMLP_W), 1) // GMLP_GROUP
    n_gm = GMLP_W // GMLP_GROUP
    gms = []
    for c in range(t // GMLP_CHUNK):
        rows = slice(c * GMLP_CHUNK, (c + 1) * GMLP_CHUNK)
        vn = vn_ref[rows, :]
        stacked = jnp.concatenate(
            [jnp.where(lane_group == g, vn, jnp.zeros_like(vn)) for g in range(n_gm)], axis=0)
        sv = jnp.dot(wcat_ref[...], stacked, preferred_element_type=F32) + bsm_ref[...]
        gms.append((u_ref[rows, :].astype(F32) * sv).astype(BF16))
    parts.append(jnp.concatenate(gms, axis=0))

    mix = jnp.dot(jnp.concatenate(parts, axis=-1), wout_ref[...], preferred_element_type=F32)
    return x + gt1 * (_rms(mix) * g_ref[...])


def _ffn_tile(x, sh2, sc2, gt2, gpre_ref, wgu_ref, wdown_ref, gpost_ref):
    h = (_rms(x) * (gpre_ref[...] * (1.0 + sc2)) + sh2).astype(BF16)
    acc = None
    for c in range(D_FF // FF_CHUNK):
        gate = jnp.dot(h, wgu_ref[:, FF_CHUNK * c:FF_CHUNK * (c + 1)], preferred_element_type=F32)
        up = jnp.dot(h, wgu_ref[:, D_FF + FF_CHUNK * c:D_FF + FF_CHUNK * (c + 1)],
                     preferred_element_type=F32)
        act = (gate * jax.nn.sigmoid(gate) * up).astype(BF16)
        part = jnp.dot(act, wdown_ref[FF_CHUNK * c:FF_CHUNK * (c + 1), :],
                       preferred_element_type=F32)
        acc = part if acc is None else acc + part
    return x + gt2 * (_rms(acc) * gpost_ref[...])


def _post_kernel(x_ref, mod_ref, o0_ref, l0_ref, o1_ref, l1_ref, o2_ref, l2_ref, u_ref, vn_ref,
                 wcat_ref, bsm_ref, wout_ref, gmix_ref, gpre_ref, wgu_ref, wdown_ref, gpost_ref,
                 y_ref):
    x1 = _mix_tile(x_ref[...], mod_ref[2:3, :], (o0_ref, o1_ref, o2_ref),
                   (l0_ref, l1_ref, l2_ref), u_ref, vn_ref, wcat_ref, bsm_ref, wout_ref, gmix_ref)
    y_ref[...] = _ffn_tile(x1, mod_ref[3:4, :], mod_ref[4:5, :], mod_ref[5:6, :],
                           gpre_ref, wgu_ref, wdown_ref, gpost_ref)


def _post(x, mod3, o0, l0, o1, l1, o2, l2, u, vn, p, g_post_mix, g_pre_ffn, g_post_ffn, *, tile):
    b, s, d = x.shape
    n_slab = GROUP_W // LANES
    tok = lambda w: pl.BlockSpec((None, tile, w), lambda bi, i: (bi, i, 0))
    slab = pl.BlockSpec((None, n_slab, tile, LANES), lambda bi, i: (bi, 0, i, 0))
    const = lambda a: pl.BlockSpec(a.shape, lambda bi, i: (0,) * a.ndim)
    consts = (p["wcat"], p["bsm"], p["w_out"], g_post_mix, g_pre_ffn, p["w_gu_r"], p["w_down"],
              g_post_ffn)
    return pl.pallas_call(
        _post_kernel,
        grid=(b, s // tile),
        in_specs=[
            tok(d),
            pl.BlockSpec((None, N_MOD, d), lambda bi, i: (bi, 0, 0)),
            slab, slab, slab, slab, slab, slab,
            tok(GMLP_W), tok(GMLP_W),
        ] + [const(a) for a in consts],
        out_specs=tok(d),
        out_shape=jax.ShapeDtypeStruct((b, s, d), F32),
        compiler_params=_cparams(2),
        name="post",
    )(x, mod3, o0, l0, o1, l1, o2, l2, u, vn, *consts)


def _prep_layer(w_in, w_s, b_s, g_gmlp, w_out, w_gu, w_down):
    col_scale = np.ones((1, IN_W), np.float32)
    col_scale[:, :ATTN_W] = HEAD_DIM ** -0.5
    w_in_b = (w_in * col_scale).astype(BF16)
    n_gm = GMLP_W // GMLP_GROUP
    wcat = jnp.transpose(w_s, (1, 0, 2)).reshape(GMLP_CHUNK, n_gm * GMLP_CHUNK).astype(BF16)
    bsm = jnp.repeat(b_s.T, GMLP_GROUP, axis=1).astype(F32)
    grp = np.arange(GMLP_W) // GMLP_GROUP
    mavg = jnp.asarray((grp[:, None] == grp[None, :]).astype(np.float32) / GMLP_GROUP, BF16)
    return dict(w_in_r=w_in_b, wcat=wcat, bsm=bsm, mavg=mavg, lng=g_gmlp.reshape(1, GMLP_W),
                w_out=w_out.astype(BF16), w_gu_r=w_gu.astype(BF16), w_down=w_down.astype(BF16))


def _encoder_layer(x, mod3, p, g_pre_mix, g_post_mix, g_pre_ffn, g_post_ffn):
    s = x.shape[1]
    row = lambda g: g.reshape(1, D_MODEL)
    q0, q1, q2, u, vn = _inproj(x, mod3, row(g_pre_mix), p["w_in_r"], p["mavg"], p["lng"], tile=512)
    attn_tile = min(s, Q_BLOCK * DILATIONS[-1])
    q0 = q0.reshape(q0.shape[0], 1, s, q0.shape[-1])
    o0, l0 = _attn(q0, 1, tile=attn_tile)
    o1, l1 = _attn(q1, DILATIONS[1], tile=attn_tile)
    o2, l2 = _attn(q2, DILATIONS[2], tile=attn_tile)
    return _post(x, mod3, o0, l0, o1, l1, o2, l2, u, vn, p, row(g_post_mix), row(g_pre_ffn),
                 row(g_post_ffn), tile=512)


def kernel(x_prompt, x_sample, c_prompt, c_sample, w_ada, b_ada, g_pre_mix, w_in, w_s, b_s, g_gmlp, w_out, g_post_mix, g_pre_ffn, w_gu, w_down, g_post_ffn):
    n_p = c_prompt.shape[0]
    c_all = jnp.concatenate([c_prompt, c_sample], axis=0)
    y_prompt, y_sample = x_prompt, x_sample
    for l in range(w_ada.shape[0]):
        mod3 = _adaln(c_all, w_ada[l], b_ada[l]).reshape(c_all.shape[0], N_MOD, D_MODEL)
        p = _prep_layer(w_in[l], w_s[l], b_s[l], g_gmlp[l], w_out[l], w_gu[l], w_down[l])
        gains = (g_pre_mix[l], g_post_mix[l], g_pre_ffn[l], g_post_ffn[l])
        y_prompt = _encoder_layer(y_prompt, mod3[:n_p], p, *gains)
        y_sample = _encoder_layer(y_sample, mod3[n_p:], p, *gains)
    return (y_prompt, y_sample)
```

```python
import functools

import numpy as np
import jax
import jax.numpy as jnp
from jax import lax
from jax.experimental import pallas as pl
from jax.experimental.pallas import tpu as pltpu

F32 = jnp.float32
BF16 = jnp.bfloat16

D_MODEL = 1024
HEAD_DIM = 64
HEADS_PER_GROUP = 4
GROUP_W = HEADS_PER_GROUP * HEAD_DIM
DILATIONS = (1, 4, 16)
N_SIDE = 64
N_GROUPS = len(DILATIONS)
ATTN_W = N_GROUPS * GROUP_W
QKV_W = 3 * GROUP_W
GMLP_W = 256
GMLP_GROUP = 64
GMLP_CHUNK = 128
IN_W = 3 * ATTN_W + 2 * GMLP_W
D_FF = 2816
N_MOD = 6
RMS_EPS = 1e-6
LN_EPS = 1e-5
NEG_INF = -1e30

LANES = 128
N_SLAB = GROUP_W // LANES
Q_BLOCK = 128
ATTN_TILE = Q_BLOCK * DILATIONS[-1]
TOKEN_TILE = 512
FRONT_BLOCKS = TOKEN_TILE // Q_BLOCK
FF_CHUNK = 256
VMEM_LIMIT = 60 * 1024 * 1024


def _cparams(n_axes):
    return pltpu.CompilerParams(
        dimension_semantics=("arbitrary",) * n_axes, vmem_limit_bytes=VMEM_LIMIT)


def _adaln_kernel(c_ref, w_ref, b_ref, o_ref):
    c = c_ref[...]
    a = (c * jax.nn.sigmoid(c)).astype(BF16)
    o_ref[...] = jnp.dot(a, w_ref[...].astype(BF16), preferred_element_type=F32) + b_ref[...]


def _adaln(c, w_ada, b_ada):
    n, d = c.shape
    nout = w_ada.shape[1]
    tn = 1024
    return pl.pallas_call(
        _adaln_kernel,
        grid=(nout // tn,),
        in_specs=[
            pl.BlockSpec((n, d), lambda j: (0, 0)),
            pl.BlockSpec((d, tn), lambda j: (0, j)),
            pl.BlockSpec((1, tn), lambda j: (0, j)),
        ],
        out_specs=pl.BlockSpec((n, tn), lambda j: (0, j)),
        out_shape=jax.ShapeDtypeStruct((n, nout), F32),
        compiler_params=_cparams(1),
        name="adaln",
    )(c, w_ada, b_ada.reshape(1, nout))


def _rms(x):
    return x * lax.rsqrt(jnp.mean(x * x, axis=-1, keepdims=True) + RMS_EPS)


def _inproj_phases(x_ref, mod_ref, g_ref, w_ref, mavg_ref, lng_ref,
                   q0_ref, q1_ref, q2_ref, u_ref, vn_ref, slab1_ref, slab2_ref):
    t = x_ref.shape[0]
    state = {}

    def norm():
        sh1 = mod_ref[0:1, :]
        sc1 = mod_ref[1:2, :]
        state["h"] = (_rms(x_ref[...]) * (g_ref[...] * (1.0 + sc1)) + sh1).astype(BF16)

    def qkv(gi):
        h = state["h"]
        return jnp.concatenate(
            [jnp.dot(h, w_ref[:, part * ATTN_W + gi * GROUP_W:part * ATTN_W + (gi + 1) * GROUP_W],
                     preferred_element_type=F32) for part in range(3)], axis=-1)

    def group0():
        q0_ref[...] = qkv(0).astype(BF16)

    def project(gi, slab_ref):
        def run():
            p = qkv(gi)
            for j in range(QKV_W // LANES):
                slab_ref[j] = p[:, j * LANES:(j + 1) * LANES]
        return run

    def permute(gi, slab_ref, out_ref):
        def run():
            d = DILATIONS[gi]
            for r in range(d):
                for j in range(QKV_W // LANES):
                    out_ref[r, :, j * LANES:(j + 1) * LANES] = (
                        slab_ref[j, pl.ds(r, t // d, stride=d), :].astype(BF16))
        return run

    def gmlp_dots():
        h = state["h"]
        base = 3 * ATTN_W
        state["gu"] = jnp.dot(h, w_ref[:, base:base + GMLP_W], preferred_element_type=F32)
        state["gv"] = jnp.dot(h, w_ref[:, base + GMLP_W:base + 2 * GMLP_W],
                              preferred_element_type=F32)

    def gmlp_tail():
        u_ref[...] = jax.nn.gelu(state["gu"]).astype(BF16)
        v = jax.nn.gelu(state["gv"])
        mu = jnp.dot(v.astype(BF16), mavg_ref[...], preferred_element_type=F32)
        vc = v - mu
        var = jnp.dot((vc * vc).astype(BF16), mavg_ref[...], preferred_element_type=F32)
        vn_ref[...] = (vc * lax.rsqrt(var + LN_EPS) * lng_ref[...]).astype(BF16)

    return dict(norm=norm, group0=group0, project1=project(1, slab1_ref),
                permute1=permute(1, slab1_ref, q1_ref), project2=project(2, slab2_ref),
                permute2=permute(2, slab2_ref, q2_ref), gmlp_dots=gmlp_dots, gmlp_tail=gmlp_tail)


FRONT_PLAN = (("norm", 0), ("gmlp_dots", 2), ("project2", 3), ("gmlp_tail", 1), ("group0", 3),
              ("permute2", 0), ("project1", 3), ("permute1", 0))


def _inproj_kernel(*refs):
    phases = _inproj_phases(*refs)
    for name, _ in FRONT_PLAN:
        phases[name]()


def _inproj_specs(b, s, b_off, step_of):
    d = D_MODEL
    d1, d2 = DILATIONS[1], DILATIONS[2]
    tile = TOKEN_TILE

    def tok(w, off):
        return pl.BlockSpec((None, tile, w), lambda *g: (step_of(*g)[0] + off, step_of(*g)[1], 0))

    def cls(dil):
        return pl.BlockSpec((None, dil, tile // dil, QKV_W),
                            lambda *g: (step_of(*g)[0], 0, step_of(*g)[1], 0))

    const = lambda shape: pl.BlockSpec(shape, lambda *g: (0,) * len(shape))
    in_specs = [
        tok(d, b_off),
        pl.BlockSpec((None, N_MOD, d), lambda *g: (step_of(*g)[0] + b_off, 0, 0)),
        const((1, d)), const((d, IN_W)), const((GMLP_W, GMLP_W)), const((1, GMLP_W)),
    ]
    out_specs = [tok(QKV_W, 0), cls(d1), cls(d2), tok(GMLP_W, 0), tok(GMLP_W, 0)]
    out_shape = [
        jax.ShapeDtypeStruct((b, s, QKV_W), BF16),
        jax.ShapeDtypeStruct((b, d1, s // d1, QKV_W), BF16),
        jax.ShapeDtypeStruct((b, d2, s // d2, QKV_W), BF16),
        jax.ShapeDtypeStruct((b, s, GMLP_W), BF16),
        jax.ShapeDtypeStruct((b, s, GMLP_W), BF16),
    ]
    scratch = [pltpu.VMEM((QKV_W // LANES, tile, LANES), F32)] * 2
    return in_specs, out_specs, out_shape, scratch


def _inproj(x, mod3, g_pre, p, *, b_off, b):
    s = x.shape[1]
    in_specs, out_specs, out_shape, scratch = _inproj_specs(b, s, b_off, lambda bi, i: (bi, i))
    return pl.pallas_call(
        _inproj_kernel,
        grid=(b, s // TOKEN_TILE),
        in_specs=in_specs, out_specs=out_specs, out_shape=out_shape, scratch_shapes=scratch,
        compiler_params=_cparams(2),
        name="inproj",
    )(x, mod3, g_pre, p["w_in"], p["mavg"], p["lng"])


def _attn_bias(dil, kw):
    group = DILATIONS.index(dil)
    n_heads = N_GROUPS * HEADS_PER_GROUP
    slopes = 2.0 ** (-8.0 * np.arange(1, n_heads + 1, dtype=np.float32) / n_heads)
    slopes = slopes[group * HEADS_PER_GROUP:(group + 1) * HEADS_PER_GROUP].astype(np.float32)
    n_var = 3 if kw == 2 * Q_BLOCK else 1
    i = np.arange(Q_BLOCK)[:, None]
    c = np.arange(kw)[None, :]
    out = np.empty((n_var, HEADS_PER_GROUP, Q_BLOCK, kw), np.float32)
    for v in range(n_var):
        rel = np.abs(c - N_SIDE * v - i)
        dist = (dil * rel).astype(np.float32)
        for h in range(HEADS_PER_GROUP):
            out[v, h] = np.where(rel <= N_SIDE, -slopes[h] * dist, np.float32(NEG_INF))
    return out


def _class_major(dil):
    return dil > FRONT_BLOCKS


def _attn_stages(qkv_ref, bias_ref, o_ref, l_ref, s_bufs, m_bufs, *, dil, seq, kw, coords):
    heads_per_tile = LANES // HEAD_DIM
    half = lax.broadcasted_iota(jnp.int32, (Q_BLOCK, LANES), 1) // HEAD_DIM

    def window(qs):
        return pl.multiple_of(jnp.clip(qs - N_SIDE, 0, seq - kw), N_SIDE)

    def scores(n, slot):
        r, qs, _ = coords(n)
        qs = pl.multiple_of(qs, Q_BLOCK)
        ks = window(qs)
        s_buf, m_buf = s_bufs[slot], m_bufs[slot]
        var = (qs - ks) // N_SIDE
        for h in range(HEADS_PER_GROUP):
            tile = slice((h // heads_per_tile) * LANES, (h // heads_per_tile + 1) * LANES)
            q = qkv_ref[r, pl.ds(qs, Q_BLOCK), tile]
            q = jnp.where(half == h % heads_per_tile, q, jnp.zeros_like(q))
            k = qkv_ref[r, pl.ds(ks, kw), GROUP_W + tile.start:GROUP_W + tile.stop]
            s = lax.dot_general(q, k, (((1,), (1,)), ((), ())), preferred_element_type=F32)
            bias = bias_ref[var, h]
            s = jnp.where(bias > 0.5 * NEG_INF, s + bias, NEG_INF)
            s_buf[h, :, 0:kw] = s
            m_buf[h] = jnp.broadcast_to(jnp.max(s, axis=-1, keepdims=True), (Q_BLOCK, LANES))

    def output(n, slot):
        r, qs, row0 = coords(n)
        ks = window(pl.multiple_of(qs, Q_BLOCK))
        s_buf, m_buf = s_bufs[slot], m_bufs[slot]
        if _class_major(dil):
            dst = lambda tl: (tl, row0)
        else:
            rows = pl.ds(row0, Q_BLOCK) if dil == 1 else pl.ds(row0, Q_BLOCK, stride=dil)
            dst = lambda tl: (tl, rows, slice(None))
        for tl in range(N_SLAB):
            o_pair, lse_pair = [], []
            for hp in range(heads_per_tile):
                h = tl * heads_per_tile + hp
                m = m_buf[h]
                p = [jnp.exp(s_buf[h, :, c * LANES:(c + 1) * LANES] - m)
                     for c in range(kw // LANES)]
                psum = p[0] if len(p) == 1 else p[0] + p[1]
                l = jnp.sum(psum, axis=-1, keepdims=True)
                pb = jnp.concatenate(p, axis=-1).astype(BF16)
                v = qkv_ref[r, pl.ds(ks, kw), 2 * GROUP_W + tl * LANES:2 * GROUP_W + (tl + 1) * LANES]
                o_pair.append(jnp.dot(pb, v, preferred_element_type=F32) * (1.0 / l))
                lse_pair.append(m + jnp.log(l))
            o_ref[dst(tl)] = jnp.where(half == 0, o_pair[0], o_pair[1])
            l_ref[dst(tl)] = jnp.where(half == 0, lse_pair[0], lse_pair[1])

    return scores, output


def _attn_kernel(qkv_ref, bias_ref, o_ref, l_ref, s_buf0, s_buf1, m_buf0, m_buf1,
                 *, dil, nq, seq, kw):
    t = pl.program_id(1)
    n_blocks = dil * nq
    assert n_blocks % 2 == 0

    def coords(n):
        r = n // nq
        j = n % nq
        return r, (t * nq + j) * Q_BLOCK, (r if _class_major(dil) else j * (Q_BLOCK * dil) + r)

    scores, output = _attn_stages(qkv_ref, bias_ref, o_ref, l_ref, (s_buf0, s_buf1),
                                  (m_buf0, m_buf1), dil=dil, seq=seq, kw=kw, coords=coords)
    scores(0, 0)

    def body(i, carry):
        n = 2 * i
        scores(n + 1, 1)
        output(n, 0)
        scores(n + 2, 0)
        output(n + 1, 1)
        return carry

    lax.fori_loop(0, n_blocks // 2 - 1, body, 0)
    scores(n_blocks - 1, 1)
    output(n_blocks - 2, 0)
    output(n_blocks - 1, 1)


def _attn_scratch(kw):
    return [
        pltpu.VMEM((HEADS_PER_GROUP, Q_BLOCK, kw), F32),
        pltpu.VMEM((HEADS_PER_GROUP, Q_BLOCK, kw), F32),
        pltpu.VMEM((HEADS_PER_GROUP, Q_BLOCK, LANES), F32),
        pltpu.VMEM((HEADS_PER_GROUP, Q_BLOCK, LANES), F32),
    ]


def _attn(qkv, dil):
    b, d, seq, w = qkv.shape
    s = d * seq
    tile = min(s, ATTN_TILE)
    kw = min(2 * Q_BLOCK, seq)
    nq = tile // (Q_BLOCK * dil)
    bias = jnp.asarray(_attn_bias(dil, kw))
    if _class_major(dil):
        assert nq == 1
        out_sds = jax.ShapeDtypeStruct((b, N_SLAB, d, seq, LANES), F32)
        out_spec = pl.BlockSpec((None, N_SLAB, d, Q_BLOCK, LANES), lambda bi, i: (bi, 0, 0, i, 0))
    else:
        out_sds = jax.ShapeDtypeStruct((b, N_SLAB, s, LANES), F32)
        out_spec = pl.BlockSpec((None, N_SLAB, tile, LANES), lambda bi, i: (bi, 0, i, 0))
    return pl.pallas_call(
        functools.partial(_attn_kernel, dil=dil, nq=nq, seq=seq, kw=kw),
        grid=(b, s // tile),
        in_specs=[
            pl.BlockSpec((None, d, seq, w), lambda bi, i: (bi, 0, 0, 0)),
            pl.BlockSpec(bias.shape, lambda bi, i: (0, 0, 0, 0)),
        ],
        out_specs=[out_spec, out_spec],
        out_shape=[out_sds, out_sds],
        scratch_shapes=_attn_scratch(kw),
        compiler_params=_cparams(2),
        name=f"attn_d{dil}",
    )(qkv, bias)


N_INPROJ_IN = 6
N_INPROJ_OUT = 5


def _front_kernel(*refs, seq_tokens, steps_per_seq):
    n_in = N_INPROJ_IN + 2 * N_GROUPS
    n_out = N_INPROJ_OUT + 2 * N_GROUPS
    ins, outs, scr = refs[:n_in], refs[n_in:n_in + n_out], refs[n_in + n_out:]
    qkv_refs = ins[N_INPROJ_IN:N_INPROJ_IN + N_GROUPS]
    bias_refs = ins[N_INPROJ_IN + N_GROUPS:]
    attn_outs = outs[N_INPROJ_OUT:]
    slab_refs, s_bufs, m_bufs = scr[0:2], scr[2:4], scr[4:6]
    w = pl.program_id(0) % steps_per_seq

    units = []
    for gi, dil in enumerate(DILATIONS):
        seq = seq_tokens // dil
        kw = min(2 * Q_BLOCK, seq)

        def coords(n, dil=dil):
            m = w * FRONT_BLOCKS + n
            r, j = m % dil, m // dil
            if _class_major(dil):
                row0 = n
            else:
                row0 = (n // dil) * Q_BLOCK * dil + n % dil
            return r, j * Q_BLOCK, row0

        scores, output = _attn_stages(
            qkv_refs[gi], bias_refs[gi], attn_outs[2 * gi], attn_outs[2 * gi + 1],
            s_bufs, m_bufs, dil=dil, seq=seq, kw=kw, coords=coords)
        units += [(scores, output, n) for n in range(FRONT_BLOCKS)]

    phases = _inproj_phases(*ins[:N_INPROJ_IN], *outs[:N_INPROJ_OUT], *slab_refs)
    n_units = len(units)
    assert sum(cnt for _, cnt in FRONT_PLAN) == n_units
    units[0][0](units[0][2], 0)
    k = 0
    for name, cnt in FRONT_PLAN:
        phases[name]()
        for _ in range(cnt):
            if k + 1 < n_units:
                units[k + 1][0](units[k + 1][2], (k + 1) % 2)
            units[k][1](units[k][2], k % 2)
            k += 1


def _front(x, mod3, g_pre, p, qkvs, *, b_off, b):
    s = x.shape[1]
    tiles_in = s // TOKEN_TILE
    n_steps = b * tiles_in
    b_att = qkvs[0].shape[0]
    s_att = qkvs[0].shape[1] * qkvs[0].shape[2]
    steps_per_seq = s_att // TOKEN_TILE
    assert n_steps == b_att * steps_per_seq and s_att % ATTN_TILE == 0
    step_in = lambda g: (g // tiles_in, g % tiles_in)
    in_specs, out_specs, out_shape, scratch = _inproj_specs(b, s, b_off, step_in)

    biases = []
    for gi, dil in enumerate(DILATIONS):
        d, seq = qkvs[gi].shape[1:3]
        assert d == dil
        biases.append(jnp.asarray(_attn_bias(dil, min(2 * Q_BLOCK, seq))))
        in_specs.append(pl.BlockSpec((None, d, seq, QKV_W), lambda g: (g // steps_per_seq, 0, 0, 0)))
    in_specs += [pl.BlockSpec(bs.shape, lambda g: (0, 0, 0, 0)) for bs in biases]
    for dil in DILATIONS:
        if _class_major(dil):
            per_row = dil // FRONT_BLOCKS
            spec = pl.BlockSpec(
                (None, N_SLAB, FRONT_BLOCKS, Q_BLOCK, LANES),
                lambda g, per_row=per_row: (g // steps_per_seq, 0, (g % steps_per_seq) % per_row,
                                            (g % steps_per_seq) // per_row, 0))
            sds = jax.ShapeDtypeStruct((b_att, N_SLAB, dil, s_att // dil, LANES), F32)
        else:
            spec = pl.BlockSpec((None, N_SLAB, TOKEN_TILE, LANES),
                                lambda g: (g // steps_per_seq, 0, g % steps_per_seq, 0))
            sds = jax.ShapeDtypeStruct((b_att, N_SLAB, s_att, LANES), F32)
        out_specs += [spec, spec]
        out_shape += [sds, sds]
    scratch = scratch + _attn_scratch(2 * Q_BLOCK)
    outs = pl.pallas_call(
        functools.partial(_front_kernel, seq_tokens=s_att, steps_per_seq=steps_per_seq),
        grid=(n_steps,),
        in_specs=in_specs, out_specs=out_specs, out_shape=out_shape, scratch_shapes=scratch,
        compiler_params=_cparams(1),
        name="front",
    )(x, mod3, g_pre, p["w_in"], p["mavg"], p["lng"], *qkvs, *biases)
    return outs[:N_INPROJ_OUT], outs[N_INPROJ_OUT:]


def _mix_tile(x, gt1, o_refs, l_refs, u_ref, vn_ref, wcat_ref, bsm_ref, wout_ref, g_ref, perm_ref):
    t = x.shape[0]

    def slabs(ref, dil, base):
        if _class_major(dil):
            for r in range(dil):
                for sb in range(N_SLAB):
                    perm_ref[base + sb, pl.ds(r, t // dil, stride=dil), :] = ref[sb, r]
            ref, lo = perm_ref, base
        else:
            lo = 0
        return jnp.concatenate([ref[lo + sb] for sb in range(N_SLAB)], axis=-1)

    lses = [slabs(r, dil, 0) for r, dil in zip(l_refs, DILATIONS)]
    m = jnp.maximum(jnp.maximum(lses[0], lses[1]), lses[2])
    es = [jnp.exp(l - m) for l in lses]
    inv = 1.0 / (es[0] + es[1] + es[2])
    parts = [(es[gi] * inv * slabs(r, dil, N_SLAB)).astype(BF16)
             for gi, (r, dil) in enumerate(zip(o_refs, DILATIONS))]

    lane_group = lax.broadcasted_iota(jnp.int32, (GMLP_CHUNK, GMLP_W), 1) // GMLP_GROUP
    n_gm = GMLP_W // GMLP_GROUP
    gms = []
    for c in range(t // GMLP_CHUNK):
        rows = slice(c * GMLP_CHUNK, (c + 1) * GMLP_CHUNK)
        vn = vn_ref[rows, :]
        stacked = jnp.concatenate(
            [jnp.where(lane_group == g, vn, jnp.zeros_like(vn)) for g in range(n_gm)], axis=0)
        sv = jnp.dot(wcat_ref[...], stacked, preferred_element_type=F32) + bsm_ref[...]
        gms.append((u_ref[rows, :].astype(F32) * sv).astype(BF16))
    parts.append(jnp.concatenate(gms, axis=0))

    mix = jnp.dot(jnp.concatenate(parts, axis=-1), wout_ref[...], preferred_element_type=F32)
    return x + gt1 * (_rms(mix) * g_ref[...])


def _ffn_tile(x, sh2, sc2, gt2, gpre_ref, wgu_ref, wdown_ref, gpost_ref):
    h = (_rms(x) * (gpre_ref[...] * (1.0 + sc2)) + sh2).astype(BF16)
    acc = None
    for c in range(D_FF // FF_CHUNK):
        gate = jnp.dot(h, wgu_ref[:, FF_CHUNK * c:FF_CHUNK * (c + 1)], preferred_element_type=F32)
        up = jnp.dot(h, wgu_ref[:, D_FF + FF_CHUNK * c:D_FF + FF_CHUNK * (c + 1)],
                     preferred_element_type=F32)
        act = (gate * jax.nn.sigmoid(gate) * up).astype(BF16)
        part = jnp.dot(act, wdown_ref[FF_CHUNK * c:FF_CHUNK * (c + 1), :],
                       preferred_element_type=F32)
        acc = part if acc is None else acc + part
    return x + gt2 * (_rms(acc) * gpost_ref[...])


def _post_kernel(x_ref, mod_ref, o0_ref, l0_ref, o1_ref, l1_ref, o2_ref, l2_ref, u_ref, vn_ref,
                 wcat_ref, bsm_ref, wout_ref, gmix_ref, gpre_ref, wgu_ref, wdown_ref, gpost_ref,
                 *rest):
    y_ref, perm_ref = rest[-2:]
    x1 = _mix_tile(x_ref[...], mod_ref[2:3, :], (o0_ref, o1_ref, o2_ref),
                   (l0_ref, l1_ref, l2_ref), u_ref, vn_ref, wcat_ref, bsm_ref, wout_ref, gmix_ref,
                   perm_ref)
    y_ref[...] = _ffn_tile(x1, mod_ref[3:4, :], mod_ref[4:5, :], mod_ref[5:6, :],
                           gpre_ref, wgu_ref, wdown_ref, gpost_ref)


def _post(x, mod3, attn_outs, u, vn, p, gains, *, b_off, y_prev=None):
    b_all, s, d = x.shape
    b = u.shape[0]
    tile = TOKEN_TILE
    tok = lambda w, off: pl.BlockSpec((None, tile, w), lambda bi, i: (bi + off, i, 0))
    slabs = []
    for dil in DILATIONS:
        if _class_major(dil):
            spec = pl.BlockSpec((None, N_SLAB, dil, tile // dil, LANES),
                                lambda bi, i: (bi, 0, 0, i, 0))
        else:
            spec = pl.BlockSpec((None, N_SLAB, tile, LANES), lambda bi, i: (bi, 0, i, 0))
        slabs += [spec, spec]
    const = lambda a: pl.BlockSpec(a.shape, lambda bi, i: (0,) * a.ndim)
    g_post_mix, g_pre_ffn, g_post_ffn = gains
    consts = (p["wcat"], p["bsm"], p["w_out"], g_post_mix, g_pre_ffn, p["w_gu"], p["w_down"],
              g_post_ffn)
    operands = [x, mod3, *attn_outs, u, vn, *consts]
    in_specs = [
        tok(d, b_off),
        pl.BlockSpec((None, N_MOD, d), lambda bi, i: (bi + b_off, 0, 0)),
    ] + slabs + [tok(GMLP_W, 0), tok(GMLP_W, 0)] + [const(a) for a in consts]
    aliases = {}
    if y_prev is not None:
        operands.append(y_prev)
        in_specs.append(pl.BlockSpec(memory_space=pl.ANY))
        aliases = {len(operands) - 1: 0}
    return pl.pallas_call(
        _post_kernel,
        grid=(b, s // tile),
        in_specs=in_specs,
        out_specs=tok(d, b_off),
        out_shape=jax.ShapeDtypeStruct((b_all, s, d), F32),
        input_output_aliases=aliases,
        scratch_shapes=[pltpu.VMEM((2 * N_SLAB, tile, LANES), F32)],
        compiler_params=_cparams(2),
        name="post",
    )(*operands)


def _prep_layer(w_in, w_s, b_s, g_gmlp, w_out, w_gu, w_down):
    col_scale = np.ones((1, IN_W), np.float32)
    col_scale[:, :ATTN_W] = HEAD_DIM ** -0.5
    n_gm = GMLP_W // GMLP_GROUP
    wcat = jnp.transpose(w_s, (1, 0, 2)).reshape(GMLP_CHUNK, n_gm * GMLP_CHUNK).astype(BF16)
    bsm = jnp.repeat(b_s.T, GMLP_GROUP, axis=1).astype(F32)
    grp = np.arange(GMLP_W) // GMLP_GROUP
    mavg = jnp.asarray((grp[:, None] == grp[None, :]).astype(np.float32) / GMLP_GROUP, BF16)
    return dict(w_in=(w_in * col_scale).astype(BF16), wcat=wcat, bsm=bsm, mavg=mavg,
                lng=g_gmlp.reshape(1, GMLP_W), w_out=w_out.astype(BF16),
                w_gu=w_gu.astype(BF16), w_down=w_down.astype(BF16))


def _as_classes(q0):
    return q0.reshape(q0.shape[0], 1, q0.shape[1], q0.shape[2])


def _attn_all(q0, q1, q2):
    outs = []
    for qkv, dil in zip((_as_classes(q0), q1, q2), DILATIONS):
        outs += list(_attn(qkv, dil))
    return outs


def _layer(x_p, x_s, mod_p, mod_s, p, g_pre_mix, g_post_mix, g_pre_ffn, g_post_ffn):
    row = lambda g: g.reshape(1, D_MODEL)
    g_pre = row(g_pre_mix)
    gains = (row(g_post_mix), row(g_pre_ffn), row(g_post_ffn))
    b_p, b_s = x_p.shape[0], x_s.shape[0]
    half = b_s // 2
    steps = lambda b, x: b * (x.shape[1] // TOKEN_TILE)
    fused = (b_s % 2 == 0 and x_s.shape[1] % ATTN_TILE == 0
             and steps(half, x_s) == steps(b_p, x_p))
    if fused:
        qa0, qa1, qa2, u_a, vn_a = _inproj(x_s, mod_s, g_pre, p, b_off=0, b=half)
        (qb0, qb1, qb2, u_b, vn_b), att_a = _front(
            x_s, mod_s, g_pre, p, (_as_classes(qa0), qa1, qa2), b_off=half, b=half)
        (qp0, qp1, qp2, u_p, vn_p), att_b = _front(
            x_p, mod_p, g_pre, p, (_as_classes(qb0), qb1, qb2), b_off=0, b=b_p)
        att_p = _attn_all(qp0, qp1, qp2)
        y_s = _post(x_s, mod_s, att_a, u_a, vn_a, p, gains, b_off=0)
        y_s = _post(x_s, mod_s, att_b, u_b, vn_b, p, gains, b_off=half, y_prev=y_s)
        y_p = _post(x_p, mod_p, att_p, u_p, vn_p, p, gains, b_off=0)
        return y_p, y_s
    outs = []
    for x, mod in ((x_p, mod_p), (x_s, mod_s)):
        q0, q1, q2, u, vn = _inproj(x, mod, g_pre, p, b_off=0, b=x.shape[0])
        outs.append(_post(x, mod, _attn_all(q0, q1, q2), u, vn, p, gains, b_off=0))
    return tuple(outs)


def kernel(x_prompt, x_sample, c_prompt, c_sample, w_ada, b_ada, g_pre_mix, w_in, w_s, b_s, g_gmlp, w_out, g_post_mix, g_pre_ffn, w_gu, w_down, g_post_ffn):
    n_p = c_prompt.shape[0]
    c_all = jnp.concatenate([c_prompt, c_sample], axis=0)
    y_prompt, y_sample = x_prompt, x_sample
    for l in range(w_ada.shape[0]):
        mod3 = _adaln(c_all, w_ada[l], b_ada[l]).reshape(c_all.shape[0], N_MOD, D_MODEL)
        p = _prep_layer(w_in[l], w_s[l], b_s[l], g_gmlp[l], w_out[l], w_gu[l], w_down[l])
        y_prompt, y_sample = _layer(y_prompt, y_sample, mod3[:n_p], mod3[n_p:], p,
                                    g_pre_mix[l], g_post_mix[l], g_pre_ffn[l], g_post_ffn[l])
    return (y_prompt, y_sample)
```

```python
import functools

import numpy as np
import jax
import jax.numpy as jnp
from jax import lax
from jax.experimental import pallas as pl
from jax.experimental.pallas import tpu as pltpu

F32 = jnp.float32
BF16 = jnp.bfloat16

D_MODEL = 1024
HEAD_DIM = 64
HEADS_PER_GROUP = 4
GROUP_W = HEADS_PER_GROUP * HEAD_DIM
DILATIONS = (1, 4, 16)
N_SIDE = 64
N_GROUPS = len(DILATIONS)
ATTN_W = N_GROUPS * GROUP_W
QKV_W = 3 * GROUP_W
GMLP_W = 256
GMLP_GROUP = 64
GMLP_CHUNK = 128
IN_W = 3 * ATTN_W + 2 * GMLP_W
D_FF = 2816
N_MOD = 6
RMS_EPS = 1e-6
LN_EPS = 1e-5
NEG_INF = -1e30

LANES = 128
N_SLAB = GROUP_W // LANES
Q_BLOCK = 128
ATTN_TILE = Q_BLOCK * DILATIONS[-1]
TOKEN_TILE = 512
CHUNK_TOKENS = 8192
FRONT_BLOCKS = TOKEN_TILE // Q_BLOCK
FF_CHUNK = 256
VMEM_LIMIT = 60 * 1024 * 1024
QKV_WINDOW_BUDGET = 20 * 1024 * 1024


def _cparams(n_axes):
    return pltpu.CompilerParams(
        dimension_semantics=("arbitrary",) * n_axes, vmem_limit_bytes=VMEM_LIMIT)


def _accumulate(operands, in_specs, prev, first_out):
    aliases = {}
    for k, buf in enumerate(prev or ()):
        aliases[len(operands)] = first_out + k
        operands.append(buf)
        in_specs.append(pl.BlockSpec(memory_space=pl.ANY))
    return aliases


def _adaln_kernel(c_ref, w_ref, b_ref, o_ref):
    c = c_ref[...]
    a = (c * jax.nn.sigmoid(c)).astype(BF16)
    o_ref[...] = jnp.dot(a, w_ref[...].astype(BF16), preferred_element_type=F32) + b_ref[...]


def _adaln(c, w_ada, b_ada):
    n, d = c.shape
    nout = w_ada.shape[1]
    tn = 1024
    return pl.pallas_call(
        _adaln_kernel,
        grid=(nout // tn,),
        in_specs=[
            pl.BlockSpec((n, d), lambda j: (0, 0)),
            pl.BlockSpec((d, tn), lambda j: (0, j)),
            pl.BlockSpec((1, tn), lambda j: (0, j)),
        ],
        out_specs=pl.BlockSpec((n, tn), lambda j: (0, j)),
        out_shape=jax.ShapeDtypeStruct((n, nout), F32),
        compiler_params=_cparams(1),
        name="adaln",
    )(c, w_ada, b_ada.reshape(1, nout))


def _rms(x):
    return x * lax.rsqrt(jnp.mean(x * x, axis=-1, keepdims=True) + RMS_EPS)


def _inproj_phases(x_ref, mod_ref, g_ref, w_ref, mavg_ref, lng_ref,
                   q0_ref, q1_ref, q2_ref, u_ref, vn_ref, slab1_ref, slab2_ref):
    t = x_ref.shape[0]
    state = {}

    def norm():
        sh1 = mod_ref[0:1, :]
        sc1 = mod_ref[1:2, :]
        state["h"] = (_rms(x_ref[...]) * (g_ref[...] * (1.0 + sc1)) + sh1).astype(BF16)

    def qkv(gi):
        h = state["h"]
        return jnp.concatenate(
            [jnp.dot(h, w_ref[:, part * ATTN_W + gi * GROUP_W:part * ATTN_W + (gi + 1) * GROUP_W],
                     preferred_element_type=F32) for part in range(3)], axis=-1)

    def group0():
        q0_ref[...] = qkv(0).astype(BF16)

    def project(gi, slab_ref):
        def run():
            p = qkv(gi)
            for j in range(QKV_W // LANES):
                slab_ref[j] = p[:, j * LANES:(j + 1) * LANES]
        return run

    def permute(gi, slab_ref, out_ref):
        def run():
            d = DILATIONS[gi]
            for r in range(d):
                for j in range(QKV_W // LANES):
                    out_ref[r, :, j * LANES:(j + 1) * LANES] = (
                        slab_ref[j, pl.ds(r, t // d, stride=d), :].astype(BF16))
        return run

    def gmlp_dots():
        h = state["h"]
        base = 3 * ATTN_W
        state["gu"] = jnp.dot(h, w_ref[:, base:base + GMLP_W], preferred_element_type=F32)
        state["gv"] = jnp.dot(h, w_ref[:, base + GMLP_W:base + 2 * GMLP_W],
                              preferred_element_type=F32)

    def gmlp_tail():
        u_ref[...] = jax.nn.gelu(state["gu"]).astype(BF16)
        v = jax.nn.gelu(state["gv"])
        mu = jnp.dot(v.astype(BF16), mavg_ref[...], preferred_element_type=F32)
        vc = v - mu
        var = jnp.dot((vc * vc).astype(BF16), mavg_ref[...], preferred_element_type=F32)
        vn_ref[...] = (vc * lax.rsqrt(var + LN_EPS) * lng_ref[...]).astype(BF16)

    return dict(norm=norm, group0=group0, project1=project(1, slab1_ref),
                permute1=permute(1, slab1_ref, q1_ref), project2=project(2, slab2_ref),
                permute2=permute(2, slab2_ref, q2_ref), gmlp_dots=gmlp_dots, gmlp_tail=gmlp_tail)


FRONT_PLAN = (("norm", 0), ("gmlp_dots", 2), ("project2", 3), ("gmlp_tail", 1), ("group0", 3),
              ("permute2", 0), ("project1", 3), ("permute1", 0))


N_INPROJ_IN = 6
N_INPROJ_OUT = 5


def _inproj_kernel(*refs, n_alias):
    phases = _inproj_phases(*refs[:N_INPROJ_IN], *refs[N_INPROJ_IN + n_alias:])
    for name, _ in FRONT_PLAN:
        phases[name]()


def _inproj_specs(b, s, b_off, b_full, step_of):
    d = D_MODEL
    d1, d2 = DILATIONS[1], DILATIONS[2]
    tile = TOKEN_TILE

    def tok(w, off):
        return pl.BlockSpec((None, tile, w), lambda *g: (step_of(*g)[0] + off, step_of(*g)[1], 0))

    def cls(dil):
        return pl.BlockSpec((None, dil, tile // dil, QKV_W),
                            lambda *g: (step_of(*g)[0], 0, step_of(*g)[1], 0))

    const = lambda shape: pl.BlockSpec(shape, lambda *g: (0,) * len(shape))
    in_specs = [
        tok(d, b_off),
        pl.BlockSpec((None, N_MOD, d), lambda *g: (step_of(*g)[0] + b_off, 0, 0)),
        const((1, d)), const((d, IN_W)), const((GMLP_W, GMLP_W)), const((1, GMLP_W)),
    ]
    out_specs = [tok(QKV_W, 0), cls(d1), cls(d2), tok(GMLP_W, b_off), tok(GMLP_W, b_off)]
    out_shape = [
        jax.ShapeDtypeStruct((b, s, QKV_W), BF16),
        jax.ShapeDtypeStruct((b, d1, s // d1, QKV_W), BF16),
        jax.ShapeDtypeStruct((b, d2, s // d2, QKV_W), BF16),
        jax.ShapeDtypeStruct((b_full, s, GMLP_W), BF16),
        jax.ShapeDtypeStruct((b_full, s, GMLP_W), BF16),
    ]
    scratch = [pltpu.VMEM((QKV_W // LANES, tile, LANES), F32)] * 2
    return in_specs, out_specs, out_shape, scratch


def _inproj(x, mod3, g_pre, p, *, b_off, b, uv_prev):
    b_full, s = x.shape[:2]
    in_specs, out_specs, out_shape, scratch = _inproj_specs(b, s, b_off, b_full,
                                                            lambda bi, i: (bi, i))
    operands = [x, mod3, g_pre, p["w_in"], p["mavg"], p["lng"]]
    aliases = _accumulate(operands, in_specs, uv_prev, 3)
    outs = pl.pallas_call(
        functools.partial(_inproj_kernel, n_alias=len(aliases)),
        grid=(b, s // TOKEN_TILE),
        in_specs=in_specs, out_specs=out_specs, out_shape=out_shape, scratch_shapes=scratch,
        input_output_aliases=aliases,
        compiler_params=_cparams(2),
        name="inproj",
    )(*operands)
    return outs[:3], outs[3:]


def _attn_bias(dil, kw):
    group = DILATIONS.index(dil)
    n_heads = N_GROUPS * HEADS_PER_GROUP
    slopes = 2.0 ** (-8.0 * np.arange(1, n_heads + 1, dtype=np.float32) / n_heads)
    slopes = slopes[group * HEADS_PER_GROUP:(group + 1) * HEADS_PER_GROUP].astype(np.float32)
    n_var = 3 if kw == 2 * Q_BLOCK else 1
    i = np.arange(Q_BLOCK)[:, None]
    c = np.arange(kw)[None, :]
    out = np.empty((n_var, HEADS_PER_GROUP, Q_BLOCK, kw), np.float32)
    for v in range(n_var):
        rel = np.abs(c - N_SIDE * v - i)
        dist = (dil * rel).astype(np.float32)
        for h in range(HEADS_PER_GROUP):
            out[v, h] = np.where(rel <= N_SIDE, -slopes[h] * dist, np.float32(NEG_INF))
    return out


def _class_major(dil):
    return dil > FRONT_BLOCKS


def _attn_stages(qkv_ref, bias_ref, o_ref, l_ref, s_bufs, m_bufs, *, dil, seq, kw, coords):
    heads_per_tile = LANES // HEAD_DIM
    half = lax.broadcasted_iota(jnp.int32, (Q_BLOCK, LANES), 1) // HEAD_DIM

    def window(qs):
        return pl.multiple_of(jnp.clip(qs - N_SIDE, 0, seq - kw), N_SIDE)

    def scores(n, slot):
        r, qs, _ = coords(n)
        qs = pl.multiple_of(qs, Q_BLOCK)
        ks = window(qs)
        s_buf, m_buf = s_bufs[slot], m_bufs[slot]
        var = (qs - ks) // N_SIDE
        for h in range(HEADS_PER_GROUP):
            tile = slice((h // heads_per_tile) * LANES, (h // heads_per_tile + 1) * LANES)
            q = qkv_ref[r, pl.ds(qs, Q_BLOCK), tile]
            q = jnp.where(half == h % heads_per_tile, q, jnp.zeros_like(q))
            k = qkv_ref[r, pl.ds(ks, kw), GROUP_W + tile.start:GROUP_W + tile.stop]
            s = lax.dot_general(q, k, (((1,), (1,)), ((), ())), preferred_element_type=F32)
            bias = bias_ref[var, h]
            s = jnp.where(bias > 0.5 * NEG_INF, s + bias, NEG_INF)
            s_buf[h, :, 0:kw] = s
            m_buf[h] = jnp.broadcast_to(jnp.max(s, axis=-1, keepdims=True), (Q_BLOCK, LANES))

    def output(n, slot):
        r, qs, row0 = coords(n)
        ks = window(pl.multiple_of(qs, Q_BLOCK))
        s_buf, m_buf = s_bufs[slot], m_bufs[slot]
        if _class_major(dil):
            dst = lambda tl: (tl, row0)
        else:
            rows = pl.ds(row0, Q_BLOCK) if dil == 1 else pl.ds(row0, Q_BLOCK, stride=dil)
            dst = lambda tl: (tl, rows, slice(None))
        for tl in range(N_SLAB):
            o_pair, lse_pair = [], []
            for hp in range(heads_per_tile):
                h = tl * heads_per_tile + hp
                m = m_buf[h]
                p = [jnp.exp(s_buf[h, :, c * LANES:(c + 1) * LANES] - m)
                     for c in range(kw // LANES)]
                psum = p[0] if len(p) == 1 else p[0] + p[1]
                l = jnp.sum(psum, axis=-1, keepdims=True)
                pb = jnp.concatenate(p, axis=-1).astype(BF16)
                v = qkv_ref[r, pl.ds(ks, kw), 2 * GROUP_W + tl * LANES:2 * GROUP_W + (tl + 1) * LANES]
                o_pair.append(jnp.dot(pb, v, preferred_element_type=F32) * (1.0 / l))
                lse_pair.append(m + jnp.log(l))
            o_ref[dst(tl)] = jnp.where(half == 0, o_pair[0], o_pair[1])
            l_ref[dst(tl)] = jnp.where(half == 0, lse_pair[0], lse_pair[1])

    return scores, output


def _attn_kernel(qkv_ref, bias_ref, *rest, dil, nq, seq, kw, n_alias):
    o_ref, l_ref, s_buf0, s_buf1, m_buf0, m_buf1 = rest[n_alias:]
    t = pl.program_id(1)
    n_blocks = dil * nq
    assert n_blocks % 2 == 0

    def coords(n):
        r = n // nq
        j = n % nq
        return r, (t * nq + j) * Q_BLOCK, (r if _class_major(dil) else j * (Q_BLOCK * dil) + r)

    scores, output = _attn_stages(qkv_ref, bias_ref, o_ref, l_ref, (s_buf0, s_buf1),
                                  (m_buf0, m_buf1), dil=dil, seq=seq, kw=kw, coords=coords)
    scores(0, 0)

    def body(i, carry):
        n = 2 * i
        scores(n + 1, 1)
        output(n, 0)
        scores(n + 2, 0)
        output(n + 1, 1)
        return carry

    lax.fori_loop(0, n_blocks // 2 - 1, body, 0)
    scores(n_blocks - 1, 1)
    output(n_blocks - 2, 0)
    output(n_blocks - 1, 1)


def _attn_scratch(kw):
    return [
        pltpu.VMEM((HEADS_PER_GROUP, Q_BLOCK, kw), F32),
        pltpu.VMEM((HEADS_PER_GROUP, Q_BLOCK, kw), F32),
        pltpu.VMEM((HEADS_PER_GROUP, Q_BLOCK, LANES), F32),
        pltpu.VMEM((HEADS_PER_GROUP, Q_BLOCK, LANES), F32),
    ]


def _attn_out(dil, b_full, s, batch_of, tile_of):
    if _class_major(dil):
        rows = FRONT_BLOCKS if tile_of.fused else dil
        per_row = dil // rows
        spec = pl.BlockSpec(
            (None, N_SLAB, rows, Q_BLOCK, LANES),
            lambda *g: (batch_of(*g), 0, tile_of(*g) % per_row, tile_of(*g) // per_row, 0))
        return spec, jax.ShapeDtypeStruct((b_full, N_SLAB, dil, s // dil, LANES), F32)
    tile = TOKEN_TILE if tile_of.fused else min(s, ATTN_TILE)
    spec = pl.BlockSpec((None, N_SLAB, tile, LANES), lambda *g: (batch_of(*g), 0, tile_of(*g), 0))
    return spec, jax.ShapeDtypeStruct((b_full, N_SLAB, s, LANES), F32)


def _attn(qkv, dil, *, b_off, b_full, prev):
    b, d, seq, w = qkv.shape
    s = d * seq
    tile = min(s, ATTN_TILE)
    kw = min(2 * Q_BLOCK, seq)
    nq = tile // (Q_BLOCK * dil)
    assert not _class_major(dil) or nq == 1
    bias = jnp.asarray(_attn_bias(dil, kw))
    tile_of = lambda bi, i: i
    tile_of.fused = False
    out_spec, out_sds = _attn_out(dil, b_full, s, lambda bi, i: bi + b_off, tile_of)
    operands = [qkv, bias]
    in_specs = [
        pl.BlockSpec((None, d, seq, w), lambda bi, i: (bi, 0, 0, 0)),
        pl.BlockSpec(bias.shape, lambda bi, i: (0, 0, 0, 0)),
    ]
    aliases = _accumulate(operands, in_specs, prev, 0)
    return pl.pallas_call(
        functools.partial(_attn_kernel, dil=dil, nq=nq, seq=seq, kw=kw, n_alias=len(aliases)),
        grid=(b, s // tile),
        in_specs=in_specs,
        out_specs=[out_spec, out_spec],
        out_shape=[out_sds, out_sds],
        scratch_shapes=_attn_scratch(kw),
        input_output_aliases=aliases,
        compiler_params=_cparams(2),
        name=f"attn_d{dil}",
    )(*operands)


def _front_kernel(*refs, seq_tokens, steps_per_seq, n_alias):
    n_in = N_INPROJ_IN + 2 * N_GROUPS
    n_out = N_INPROJ_OUT + 2 * N_GROUPS
    ins, refs = refs[:n_in], refs[n_in + n_alias:]
    outs, scr = refs[:n_out], refs[n_out:]
    qkv_refs = ins[N_INPROJ_IN:N_INPROJ_IN + N_GROUPS]
    bias_refs = ins[N_INPROJ_IN + N_GROUPS:]
    attn_outs = outs[N_INPROJ_OUT:]
    slab_refs, s_bufs, m_bufs = scr[0:2], scr[2:4], scr[4:6]
    w = pl.program_id(0) % steps_per_seq

    units = []
    for gi, dil in enumerate(DILATIONS):
        seq = seq_tokens // dil
        kw = min(2 * Q_BLOCK, seq)

        def coords(n, dil=dil):
            m = w * FRONT_BLOCKS + n
            r, j = m % dil, m // dil
            if _class_major(dil):
                row0 = n
            else:
                row0 = (n // dil) * Q_BLOCK * dil + n % dil
            return r, j * Q_BLOCK, row0

        scores, output = _attn_stages(
            qkv_refs[gi], bias_refs[gi], attn_outs[2 * gi], attn_outs[2 * gi + 1],
            s_bufs, m_bufs, dil=dil, seq=seq, kw=kw, coords=coords)
        units += [(scores, output, n) for n in range(FRONT_BLOCKS)]

    phases = _inproj_phases(*ins[:N_INPROJ_IN], *outs[:N_INPROJ_OUT], *slab_refs)
    n_units = len(units)
    assert sum(cnt for _, cnt in FRONT_PLAN) == n_units
    units[0][0](units[0][2], 0)
    k = 0
    for name, cnt in FRONT_PLAN:
        phases[name]()
        for _ in range(cnt):
            if k + 1 < n_units:
                units[k + 1][0](units[k + 1][2], (k + 1) % 2)
            units[k][1](units[k][2], k % 2)
            k += 1


def _front(x, mod3, g_pre, p, *, b_off, b, uv_prev, qkvs, att_off, att_full, att_prev):
    b_full, s = x.shape[:2]
    tiles_in = s // TOKEN_TILE
    n_steps = b * tiles_in
    b_att = qkvs[0].shape[0]
    s_att = qkvs[0].shape[1] * qkvs[0].shape[2]
    steps_per_seq = s_att // TOKEN_TILE
    assert n_steps == b_att * steps_per_seq and s_att % ATTN_TILE == 0
    step_in = lambda g: (g // tiles_in, g % tiles_in)
    in_specs, out_specs, out_shape, scratch = _inproj_specs(b, s, b_off, b_full, step_in)

    qkv_bytes = N_GROUPS * s_att * QKV_W * 2
    qkv_mode = pl.Buffered(1) if 2 * qkv_bytes > QKV_WINDOW_BUDGET else None
    biases = []
    for gi, dil in enumerate(DILATIONS):
        d, seq = qkvs[gi].shape[1:3]
        assert d == dil
        biases.append(jnp.asarray(_attn_bias(dil, min(2 * Q_BLOCK, seq))))
        in_specs.append(pl.BlockSpec((None, d, seq, QKV_W), lambda g: (g // steps_per_seq, 0, 0, 0),
                                     pipeline_mode=qkv_mode))
    in_specs += [pl.BlockSpec(bs.shape, lambda g: (0, 0, 0, 0)) for bs in biases]
    tile_of = lambda g: g % steps_per_seq
    tile_of.fused = True
    for dil in DILATIONS:
        spec, sds = _attn_out(dil, att_full, s_att, lambda g: g // steps_per_seq + att_off, tile_of)
        out_specs += [spec, spec]
        out_shape += [sds, sds]
    scratch = scratch + _attn_scratch(2 * Q_BLOCK)
    operands = [x, mod3, g_pre, p["w_in"], p["mavg"], p["lng"], *qkvs, *biases]
    aliases = _accumulate(operands, in_specs, uv_prev, 3)
    aliases.update(_accumulate(operands, in_specs, att_prev, N_INPROJ_OUT))
    outs = pl.pallas_call(
        functools.partial(_front_kernel, seq_tokens=s_att, steps_per_seq=steps_per_seq,
                          n_alias=len(aliases)),
        grid=(n_steps,),
        in_specs=in_specs, out_specs=out_specs, out_shape=out_shape, scratch_shapes=scratch,
        input_output_aliases=aliases,
        compiler_params=_cparams(1),
        name="front",
    )(*operands)
    return outs[:3], outs[3:N_INPROJ_OUT], outs[N_INPROJ_OUT:]


def _mix_tile(x, gt1, o_refs, l_refs, u_ref, vn_ref, wcat_ref, bsm_ref, wout_ref, g_ref, perm_ref):
    t = x.shape[0]

    def slabs(ref, dil, base):
        if _class_major(dil):
            for r in range(dil):
                for sb in range(N_SLAB):
                    perm_ref[base + sb, pl.ds(r, t // dil, stride=dil), :] = ref[sb, r]
            ref, lo = perm_ref, base
        else:
            lo = 0
        return jnp.concatenate([ref[lo + sb] for sb in range(N_SLAB)], axis=-1)

    lses = [slabs(r, dil, 0) for r, dil in zip(l_refs, DILATIONS)]
    m = jnp.maximum(jnp.maximum(lses[0], lses[1]), lses[2])
    es = [jnp.exp(l - m) for l in lses]
    inv = 1.0 / (es[0] + es[1] + es[2])
    parts = [(es[gi] * inv * slabs(r, dil, N_SLAB)).astype(BF16)
             for gi, (r, dil) in enumerate(zip(o_refs, DILATIONS))]

    lane_group = lax.broadcasted_iota(jnp.int32, (GMLP_CHUNK, GMLP_W), 1) // GMLP_GROUP
    n_gm = GMLP_W // GMLP_GROUP
    gms = []
    for c in range(t // GMLP_CHUNK):
        rows = slice(c * GMLP_CHUNK, (c + 1) * GMLP_CHUNK)
        vn = vn_ref[rows, :]
        stacked = jnp.concatenate(
            [jnp.where(lane_group == g, vn, jnp.zeros_like(vn)) for g in range(n_gm)], axis=0)
        sv = jnp.dot(wcat_ref[...], stacked, preferred_element_type=F32) + bsm_ref[...]
        gms.append((u_ref[rows, :].astype(F32) * sv).astype(BF16))
    parts.append(jnp.concatenate(gms, axis=0))

    mix = jnp.dot(jnp.concatenate(parts, axis=-1), wout_ref[...], preferred_element_type=F32)
    return x + gt1 * (_rms(mix) * g_ref[...])


def _ffn_tile(x, sh2, sc2, gt2, gpre_ref, wgu_ref, wdown_ref, gpost_ref):
    h = (_rms(x) * (gpre_ref[...] * (1.0 + sc2)) + sh2).astype(BF16)
    acc = None
    for c in range(D_FF // FF_CHUNK):
        gate = jnp.dot(h, wgu_ref[:, FF_CHUNK * c:FF_CHUNK * (c + 1)], preferred_element_type=F32)
        up = jnp.dot(h, wgu_ref[:, D_FF + FF_CHUNK * c:D_FF + FF_CHUNK * (c + 1)],
                     preferred_element_type=F32)
        act = (gate * jax.nn.sigmoid(gate) * up).astype(BF16)
        part = jnp.dot(act, wdown_ref[FF_CHUNK * c:FF_CHUNK * (c + 1), :],
                       preferred_element_type=F32)
        acc = part if acc is None else acc + part
    return x + gt2 * (_rms(acc) * gpost_ref[...])


def _post_kernel(x_ref, mod_ref, o0_ref, l0_ref, o1_ref, l1_ref, o2_ref, l2_ref, u_ref, vn_ref,
                 wcat_ref, bsm_ref, wout_ref, gmix_ref, gpre_ref, wgu_ref, wdown_ref, gpost_ref,
                 y_ref, perm_ref):
    x1 = _mix_tile(x_ref[...], mod_ref[2:3, :], (o0_ref, o1_ref, o2_ref),
                   (l0_ref, l1_ref, l2_ref), u_ref, vn_ref, wcat_ref, bsm_ref, wout_ref, gmix_ref,
                   perm_ref)
    y_ref[...] = _ffn_tile(x1, mod_ref[3:4, :], mod_ref[4:5, :], mod_ref[5:6, :],
                           gpre_ref, wgu_ref, wdown_ref, gpost_ref)


def _post(x, mod3, attn_outs, u, vn, p, gains):
    b, s, d = x.shape
    tile = TOKEN_TILE
    tok = lambda w: pl.BlockSpec((None, tile, w), lambda bi, i: (bi, i, 0))
    slabs = []
    for dil in DILATIONS:
        if _class_major(dil):
            spec = pl.BlockSpec((None, N_SLAB, dil, tile // dil, LANES),
                                lambda bi, i: (bi, 0, 0, i, 0))
        else:
            spec = pl.BlockSpec((None, N_SLAB, tile, LANES), lambda bi, i: (bi, 0, i, 0))
        slabs += [spec, spec]
    const = lambda a: pl.BlockSpec(a.shape, lambda bi, i: (0,) * a.ndim)
    g_post_mix, g_pre_ffn, g_post_ffn = gains
    consts = (p["wcat"], p["bsm"], p["w_out"], g_post_mix, g_pre_ffn, p["w_gu"], p["w_down"],
              g_post_ffn)
    return pl.pallas_call(
        _post_kernel,
        grid=(b, s // tile),
        in_specs=[
            tok(d),
            pl.BlockSpec((None, N_MOD, d), lambda bi, i: (bi, 0, 0)),
        ] + slabs + [tok(GMLP_W), tok(GMLP_W)] + [const(a) for a in consts],
        out_specs=tok(d),
        out_shape=jax.ShapeDtypeStruct((b, s, d), F32),
        scratch_shapes=[pltpu.VMEM((2 * N_SLAB, tile, LANES), F32)],
        compiler_params=_cparams(2),
        name="post",
    )(x, mod3, *attn_outs, u, vn, *consts)


def _prep_layer(w_in, w_s, b_s, g_gmlp, w_out, w_gu, w_down):
    col_scale = np.ones((1, IN_W), np.float32)
    col_scale[:, :ATTN_W] = HEAD_DIM ** -0.5
    n_gm = GMLP_W // GMLP_GROUP
    wcat = jnp.transpose(w_s, (1, 0, 2)).reshape(GMLP_CHUNK, n_gm * GMLP_CHUNK).astype(BF16)
    bsm = jnp.repeat(b_s.T, GMLP_GROUP, axis=1).astype(F32)
    grp = np.arange(GMLP_W) // GMLP_GROUP
    mavg = jnp.asarray((grp[:, None] == grp[None, :]).astype(np.float32) / GMLP_GROUP, BF16)
    return dict(w_in=(w_in * col_scale).astype(BF16), wcat=wcat, bsm=bsm, mavg=mavg,
                lng=g_gmlp.reshape(1, GMLP_W), w_out=w_out.astype(BF16),
                w_gu=w_gu.astype(BF16), w_down=w_down.astype(BF16))


def _as_classes(q):
    q0, q1, q2 = q
    return q0.reshape(q0.shape[0], 1, q0.shape[1], q0.shape[2]), q1, q2


def _attn_all(qkvs, *, b_off, b_full, prev):
    outs = []
    for gi, (qkv, dil) in enumerate(zip(qkvs, DILATIONS)):
        outs += list(_attn(qkv, dil, b_off=b_off, b_full=b_full,
                           prev=None if prev is None else prev[2 * gi:2 * gi + 2]))
    return outs


def _layer(x_p, x_s, mod_p, mod_s, p, g_pre_mix, g_post_mix, g_pre_ffn, g_post_ffn):
    row = lambda g: g.reshape(1, D_MODEL)
    g_pre = row(g_pre_mix)
    gains = (row(g_post_mix), row(g_pre_ffn), row(g_post_ffn))
    groups = ((x_s, mod_s), (x_p, mod_p))

    def chunked(x):
        b, s = x.shape[:2]
        return (CHUNK_TOKENS % s == 0 and b % (CHUNK_TOKENS // s) == 0 and s % ATTN_TILE == 0)

    chunks = []
    for gid, (x, _) in enumerate(groups):
        per = CHUNK_TOKENS // x.shape[1] if all(chunked(g[0]) for g in groups) else x.shape[0]
        chunks += [(gid, off, per) for off in range(0, x.shape[0], per)]
    fuse = all(chunked(g[0]) for g in groups)

    uv = [None, None]
    att = [None, None]
    pending = None
    for gid, off, per in chunks:
        x, mod = groups[gid]
        if pending is not None and fuse:
            qk, pg, poff = pending
            q, uv[gid], att[pg] = _front(
                x, mod, g_pre, p, b_off=off, b=per, uv_prev=uv[gid], qkvs=qk, att_off=poff,
                att_full=groups[pg][0].shape[0], att_prev=att[pg])
        else:
            if pending is not None:
                qk, pg, poff = pending
                att[pg] = _attn_all(qk, b_off=poff, b_full=groups[pg][0].shape[0], prev=att[pg])
            q, uv[gid] = _inproj(x, mod, g_pre, p, b_off=off, b=per, uv_prev=uv[gid])
        pending = (_as_classes(q), gid, off)
    qk, pg, poff = pending
    att[pg] = _attn_all(qk, b_off=poff, b_full=groups[pg][0].shape[0], prev=att[pg])

    y_s, y_p = (_post(x, mod, att[gid], *uv[gid], p, gains)
                for gid, (x, mod) in enumerate(groups))
    return y_p, y_s


def kernel(x_prompt, x_sample, c_prompt, c_sample, w_ada, b_ada, g_pre_mix, w_in, w_s, b_s, g_gmlp, w_out, g_post_mix, g_pre_ffn, w_gu, w_down, g_post_ffn):
    n_p = c_prompt.shape[0]
    c_all = jnp.concatenate([c_prompt, c_sample], axis=0)
    y_prompt, y_sample = x_prompt, x_sample
    for l in range(w_ada.shape[0]):
        mod3 = _adaln(c_all, w_ada[l], b_ada[l]).reshape(c_all.shape[0], N_MOD, D_MODEL)
        p = _prep_layer(w_in[l], w_s[l], b_s[l], g_gmlp[l], w_out[l], w_gu[l], w_down[l])
        y_prompt, y_sample = _layer(y_prompt, y_sample, mod3[:n_p], mod3[n_p:], p,
                                    g_pre_mix[l], g_post_mix[l], g_pre_ffn[l], g_post_ffn[l])
    return (y_prompt, y_sample)
```

```python
import functools

import numpy as np
import jax
import jax.numpy as jnp
from jax import lax
from jax.experimental import pallas as pl
from jax.experimental.pallas import tpu as pltpu

F32 = jnp.float32
BF16 = jnp.bfloat16

D_MODEL = 1024
HEAD_DIM = 64
HEADS_PER_GROUP = 4
GROUP_W = HEADS_PER_GROUP * HEAD_DIM
DILATIONS = (1, 4, 16)
N_SIDE = 64
N_GROUPS = len(DILATIONS)
ATTN_W = N_GROUPS * GROUP_W
QKV_W = 3 * GROUP_W
GMLP_W = 256
GMLP_GROUP = 64
GMLP_CHUNK = 128
IN_W = 3 * ATTN_W + 2 * GMLP_W
D_FF = 2816
N_MOD = 6
RMS_EPS = 1e-6
LN_EPS = 1e-5
NEG_INF = -1e30

LANES = 128
N_SLAB = GROUP_W // LANES
Q_BLOCK = 128
ATTN_TILE = Q_BLOCK * DILATIONS[-1]
TOKEN_TILE = 512
CHUNK_TOKENS = 16384
FRONT_BLOCKS = TOKEN_TILE // Q_BLOCK
FF_CHUNK = 256
VMEM_LIMIT = 60 * 1024 * 1024
QKV_WINDOW_BUDGET = 20 * 1024 * 1024


def _cparams(n_axes):
    return pltpu.CompilerParams(
        dimension_semantics=("arbitrary",) * n_axes, vmem_limit_bytes=VMEM_LIMIT)


def _accumulate(operands, in_specs, prev, first_out):
    aliases = {}
    for k, buf in enumerate(prev or ()):
        aliases[len(operands)] = first_out + k
        operands.append(buf)
        in_specs.append(pl.BlockSpec(memory_space=pl.ANY))
    return aliases


def _adaln_kernel(c_ref, w_ref, b_ref, o_ref):
    c = c_ref[...]
    a = (c * jax.nn.sigmoid(c)).astype(BF16)
    o_ref[...] = jnp.dot(a, w_ref[...].astype(BF16), preferred_element_type=F32) + b_ref[...]


def _adaln(c, w_ada, b_ada):
    n, d = c.shape
    nout = w_ada.shape[1]
    tn = 1024
    return pl.pallas_call(
        _adaln_kernel,
        grid=(nout // tn,),
        in_specs=[
            pl.BlockSpec((n, d), lambda j: (0, 0)),
            pl.BlockSpec((d, tn), lambda j: (0, j)),
            pl.BlockSpec((1, tn), lambda j: (0, j)),
        ],
        out_specs=pl.BlockSpec((n, tn), lambda j: (0, j)),
        out_shape=jax.ShapeDtypeStruct((n, nout), F32),
        compiler_params=_cparams(1),
        name="adaln",
    )(c, w_ada, b_ada.reshape(1, nout))


def _rms(x):
    return x * lax.rsqrt(jnp.mean(x * x, axis=-1, keepdims=True) + RMS_EPS)


def _inproj_phases(x_ref, mod_ref, g_ref, w_ref, mavg_ref, lng_ref,
                   q0_ref, q1_ref, q2_ref, u_ref, vn_ref, slab1_ref, slab2_ref):
    t = x_ref.shape[0]
    state = {}

    def norm():
        sh1 = mod_ref[0:1, :]
        sc1 = mod_ref[1:2, :]
        state["h"] = (_rms(x_ref[...]) * (g_ref[...] * (1.0 + sc1)) + sh1).astype(BF16)

    def qkv(gi):
        h = state["h"]
        return jnp.concatenate(
            [jnp.dot(h, w_ref[:, part * ATTN_W + gi * GROUP_W:part * ATTN_W + (gi + 1) * GROUP_W],
                     preferred_element_type=F32) for part in range(3)], axis=-1)

    def group0():
        q0_ref[...] = qkv(0).astype(BF16)

    def project(gi, slab_ref):
        def run():
            p = qkv(gi)
            for j in range(QKV_W // LANES):
                slab_ref[j] = p[:, j * LANES:(j + 1) * LANES]
        return run

    def permute(gi, slab_ref, out_ref):
        def run():
            d = DILATIONS[gi]
            for r in range(d):
                for j in range(QKV_W // LANES):
                    out_ref[r, :, j * LANES:(j + 1) * LANES] = (
                        slab_ref[j, pl.ds(r, t // d, stride=d), :].astype(BF16))
        return run

    def gmlp_dots():
        h = state["h"]
        base = 3 * ATTN_W
        state["gu"] = jnp.dot(h, w_ref[:, base:base + GMLP_W], preferred_element_type=F32)
        state["gv"] = jnp.dot(h, w_ref[:, base + GMLP_W:base + 2 * GMLP_W],
                              preferred_element_type=F32)

    def gmlp_tail():
        u_ref[...] = jax.nn.gelu(state["gu"]).astype(BF16)
        v = jax.nn.gelu(state["gv"])
        mu = jnp.dot(v.astype(BF16), mavg_ref[...], preferred_element_type=F32)
        vc = v - mu
        var = jnp.dot((vc * vc).astype(BF16), mavg_ref[...], preferred_element_type=F32)
        vn_ref[...] = (vc * lax.rsqrt(var + LN_EPS) * lng_ref[...]).astype(BF16)

    return dict(norm=norm, group0=group0, project1=project(1, slab1_ref),
                permute1=permute(1, slab1_ref, q1_ref), project2=project(2, slab2_ref),
                permute2=permute(2, slab2_ref, q2_ref), gmlp_dots=gmlp_dots, gmlp_tail=gmlp_tail)


FRONT_PLAN = (("norm", 0), ("gmlp_dots", 2), ("project2", 3), ("gmlp_tail", 1), ("group0", 3),
              ("permute2", 0), ("project1", 3), ("permute1", 0))


N_INPROJ_IN = 6
N_INPROJ_OUT = 5


def _inproj_kernel(*refs, n_alias):
    phases = _inproj_phases(*refs[:N_INPROJ_IN], *refs[N_INPROJ_IN + n_alias:])
    for name, _ in FRONT_PLAN:
        phases[name]()


def _inproj_specs(b, s, b_off, b_full, step_of):
    d = D_MODEL
    d1, d2 = DILATIONS[1], DILATIONS[2]
    tile = TOKEN_TILE

    def tok(w, off):
        return pl.BlockSpec((None, tile, w), lambda *g: (step_of(*g)[0] + off, step_of(*g)[1], 0))

    def cls(dil):
        return pl.BlockSpec((None, dil, tile // dil, QKV_W),
                            lambda *g: (step_of(*g)[0], 0, step_of(*g)[1], 0))

    const = lambda shape: pl.BlockSpec(shape, lambda *g: (0,) * len(shape))
    in_specs = [
        tok(d, b_off),
        pl.BlockSpec((None, N_MOD, d), lambda *g: (step_of(*g)[0] + b_off, 0, 0)),
        const((1, d)), const((d, IN_W)), const((GMLP_W, GMLP_W)), const((1, GMLP_W)),
    ]
    out_specs = [tok(QKV_W, 0), cls(d1), cls(d2), tok(GMLP_W, b_off), tok(GMLP_W, b_off)]
    out_shape = [
        jax.ShapeDtypeStruct((b, s, QKV_W), BF16),
        jax.ShapeDtypeStruct((b, d1, s // d1, QKV_W), BF16),
        jax.ShapeDtypeStruct((b, d2, s // d2, QKV_W), BF16),
        jax.ShapeDtypeStruct((b_full, s, GMLP_W), BF16),
        jax.ShapeDtypeStruct((b_full, s, GMLP_W), BF16),
    ]
    scratch = [pltpu.VMEM((QKV_W // LANES, tile, LANES), F32)] * 2
    return in_specs, out_specs, out_shape, scratch


def _inproj(x, mod3, g_pre, p, *, b_off, b, uv_prev):
    b_full, s = x.shape[:2]
    in_specs, out_specs, out_shape, scratch = _inproj_specs(b, s, b_off, b_full,
                                                            lambda bi, i: (bi, i))
    operands = [x, mod3, g_pre, p["w_in"], p["mavg"], p["lng"]]
    aliases = _accumulate(operands, in_specs, uv_prev, 3)
    outs = pl.pallas_call(
        functools.partial(_inproj_kernel, n_alias=len(aliases)),
        grid=(b, s // TOKEN_TILE),
        in_specs=in_specs, out_specs=out_specs, out_shape=out_shape, scratch_shapes=scratch,
        input_output_aliases=aliases,
        compiler_params=_cparams(2),
        name="inproj",
    )(*operands)
    return outs[:3], outs[3:]


def _attn_bias(dil, kw):
    group = DILATIONS.index(dil)
    n_heads = N_GROUPS * HEADS_PER_GROUP
    slopes = 2.0 ** (-8.0 * np.arange(1, n_heads + 1, dtype=np.float32) / n_heads)
    slopes = slopes[group * HEADS_PER_GROUP:(group + 1) * HEADS_PER_GROUP].astype(np.float32)
    n_var = 3 if kw == 2 * Q_BLOCK else 1
    i = np.arange(Q_BLOCK)[:, None]
    c = np.arange(kw)[None, :]
    out = np.empty((n_var, HEADS_PER_GROUP, Q_BLOCK, kw), np.float32)
    for v in range(n_var):
        rel = np.abs(c - N_SIDE * v - i)
        dist = (dil * rel).astype(np.float32)
        for h in range(HEADS_PER_GROUP):
            out[v, h] = np.where(rel <= N_SIDE, -slopes[h] * dist, np.float32(NEG_INF))
    return out


def _class_major(dil):
    return dil > FRONT_BLOCKS


def _attn_stages(qkv_ref, bias_ref, o_ref, l_ref, s_bufs, m_bufs, *, dil, seq, kw, coords):
    heads_per_tile = LANES // HEAD_DIM
    half = lax.broadcasted_iota(jnp.int32, (Q_BLOCK, LANES), 1) // HEAD_DIM

    def window(qs):
        return pl.multiple_of(jnp.clip(qs - N_SIDE, 0, seq - kw), N_SIDE)

    def scores(n, slot):
        r, qs, _ = coords(n)
        qs = pl.multiple_of(qs, Q_BLOCK)
        ks = window(qs)
        s_buf, m_buf = s_bufs[slot], m_bufs[slot]
        var = (qs - ks) // N_SIDE
        for h in range(HEADS_PER_GROUP):
            tile = slice((h // heads_per_tile) * LANES, (h // heads_per_tile + 1) * LANES)
            q = qkv_ref[r, pl.ds(qs, Q_BLOCK), tile]
            q = jnp.where(half == h % heads_per_tile, q, jnp.zeros_like(q))
            k = qkv_ref[r, pl.ds(ks, kw), GROUP_W + tile.start:GROUP_W + tile.stop]
            s = lax.dot_general(q, k, (((1,), (1,)), ((), ())), preferred_element_type=F32)
            bias = bias_ref[var, h]
            s = jnp.where(bias > 0.5 * NEG_INF, s + bias, NEG_INF)
            s_buf[h, :, 0:kw] = s
            m_buf[h] = jnp.broadcast_to(jnp.max(s, axis=-1, keepdims=True), (Q_BLOCK, LANES))

    def output(n, slot):
        r, qs, row0 = coords(n)
        ks = window(pl.multiple_of(qs, Q_BLOCK))
        s_buf, m_buf = s_bufs[slot], m_bufs[slot]
        if _class_major(dil):
            dst = lambda tl: (tl, row0)
        else:
            rows = pl.ds(row0, Q_BLOCK) if dil == 1 else pl.ds(row0, Q_BLOCK, stride=dil)
            dst = lambda tl: (tl, rows, slice(None))
        for tl in range(N_SLAB):
            o_pair, lse_pair = [], []
            for hp in range(heads_per_tile):
                h = tl * heads_per_tile + hp
                m = m_buf[h]
                p = [jnp.exp(s_buf[h, :, c * LANES:(c + 1) * LANES] - m)
                     for c in range(kw // LANES)]
                psum = p[0] if len(p) == 1 else p[0] + p[1]
                l = jnp.sum(psum, axis=-1, keepdims=True)
                pb = jnp.concatenate(p, axis=-1).astype(BF16)
                v = qkv_ref[r, pl.ds(ks, kw), 2 * GROUP_W + tl * LANES:2 * GROUP_W + (tl + 1) * LANES]
                o_pair.append(jnp.dot(pb, v, preferred_element_type=F32) * (1.0 / l))
                lse_pair.append(m + jnp.log(l))
            o_ref[dst(tl)] = jnp.where(half == 0, o_pair[0], o_pair[1])
            l_ref[dst(tl)] = jnp.where(half == 0, lse_pair[0], lse_pair[1])

    return scores, output


def _attn_kernel(qkv_ref, bias_ref, *rest, dil, nq, seq, kw, n_alias):
    o_ref, l_ref, s_buf0, s_buf1, m_buf0, m_buf1 = rest[n_alias:]
    t = pl.program_id(1)
    n_blocks = dil * nq
    assert n_blocks % 2 == 0

    def coords(n):
        r = n // nq
        j = n % nq
        return r, (t * nq + j) * Q_BLOCK, (r if _class_major(dil) else j * (Q_BLOCK * dil) + r)

    scores, output = _attn_stages(qkv_ref, bias_ref, o_ref, l_ref, (s_buf0, s_buf1),
                                  (m_buf0, m_buf1), dil=dil, seq=seq, kw=kw, coords=coords)
    scores(0, 0)

    def body(i, carry):
        n = 2 * i
        scores(n + 1, 1)
        output(n, 0)
        scores(n + 2, 0)
        output(n + 1, 1)
        return carry

    lax.fori_loop(0, n_blocks // 2 - 1, body, 0)
    scores(n_blocks - 1, 1)
    output(n_blocks - 2, 0)
    output(n_blocks - 1, 1)


def _attn_scratch(kw):
    return [
        pltpu.VMEM((HEADS_PER_GROUP, Q_BLOCK, kw), F32),
        pltpu.VMEM((HEADS_PER_GROUP, Q_BLOCK, kw), F32),
        pltpu.VMEM((HEADS_PER_GROUP, Q_BLOCK, LANES), F32),
        pltpu.VMEM((HEADS_PER_GROUP, Q_BLOCK, LANES), F32),
    ]


def _attn_out(dil, b_full, s, batch_of, tile_of):
    if _class_major(dil):
        rows = FRONT_BLOCKS if tile_of.fused else dil
        per_row = dil // rows
        spec = pl.BlockSpec(
            (None, N_SLAB, rows, Q_BLOCK, LANES),
            lambda *g: (batch_of(*g), 0, tile_of(*g) % per_row, tile_of(*g) // per_row, 0))
        return spec, jax.ShapeDtypeStruct((b_full, N_SLAB, dil, s // dil, LANES), F32)
    tile = TOKEN_TILE if tile_of.fused else min(s, ATTN_TILE)
    spec = pl.BlockSpec((None, N_SLAB, tile, LANES), lambda *g: (batch_of(*g), 0, tile_of(*g), 0))
    return spec, jax.ShapeDtypeStruct((b_full, N_SLAB, s, LANES), F32)


def _attn(qkv, dil, *, b_off, b_full, prev):
    b, d, seq, w = qkv.shape
    s = d * seq
    tile = min(s, ATTN_TILE)
    kw = min(2 * Q_BLOCK, seq)
    nq = tile // (Q_BLOCK * dil)
    assert not _class_major(dil) or nq == 1
    bias = jnp.asarray(_attn_bias(dil, kw))
    tile_of = lambda bi, i: i
    tile_of.fused = False
    out_spec, out_sds = _attn_out(dil, b_full, s, lambda bi, i: bi + b_off, tile_of)
    operands = [qkv, bias]
    in_specs = [
        pl.BlockSpec((None, d, seq, w), lambda bi, i: (bi, 0, 0, 0)),
        pl.BlockSpec(bias.shape, lambda bi, i: (0, 0, 0, 0)),
    ]
    aliases = _accumulate(operands, in_specs, prev, 0)
    return pl.pallas_call(
        functools.partial(_attn_kernel, dil=dil, nq=nq, seq=seq, kw=kw, n_alias=len(aliases)),
        grid=(b, s // tile),
        in_specs=in_specs,
        out_specs=[out_spec, out_spec],
        out_shape=[out_sds, out_sds],
        scratch_shapes=_attn_scratch(kw),
        input_output_aliases=aliases,
        compiler_params=_cparams(2),
        name=f"attn_d{dil}",
    )(*operands)


def _front_kernel(*refs, seq_tokens, steps_per_seq, n_alias):
    n_in = N_INPROJ_IN + 2 * N_GROUPS
    n_out = N_INPROJ_OUT + 2 * N_GROUPS
    ins, refs = refs[:n_in], refs[n_in + n_alias:]
    outs, scr = refs[:n_out], refs[n_out:]
    qkv_refs = ins[N_INPROJ_IN:N_INPROJ_IN + N_GROUPS]
    bias_refs = ins[N_INPROJ_IN + N_GROUPS:]
    attn_outs = outs[N_INPROJ_OUT:]
    slab_refs, s_bufs, m_bufs = scr[0:2], scr[2:4], scr[4:6]
    w = pl.program_id(0) % steps_per_seq

    units = []
    for gi, dil in enumerate(DILATIONS):
        seq = seq_tokens // dil
        kw = min(2 * Q_BLOCK, seq)

        def coords(n, dil=dil):
            m = w * FRONT_BLOCKS + n
            r, j = m % dil, m // dil
            if _class_major(dil):
                row0 = n
            else:
                row0 = (n // dil) * Q_BLOCK * dil + n % dil
            return r, j * Q_BLOCK, row0

        scores, output = _attn_stages(
            qkv_refs[gi], bias_refs[gi], attn_outs[2 * gi], attn_outs[2 * gi + 1],
            s_bufs, m_bufs, dil=dil, seq=seq, kw=kw, coords=coords)
        units += [(scores, output, n) for n in range(FRONT_BLOCKS)]

    phases = _inproj_phases(*ins[:N_INPROJ_IN], *outs[:N_INPROJ_OUT], *slab_refs)
    n_units = len(units)
    assert sum(cnt for _, cnt in FRONT_PLAN) == n_units
    units[0][0](units[0][2], 0)
    k = 0
    for name, cnt in FRONT_PLAN:
        phases[name]()
        for _ in range(cnt):
            if k + 1 < n_units:
                units[k + 1][0](units[k + 1][2], (k + 1) % 2)
            units[k][1](units[k][2], k % 2)
            k += 1


def _front(x, mod3, g_pre, p, *, b_off, b, uv_prev, qkvs, att_off, att_full, att_prev):
    b_full, s = x.shape[:2]
    tiles_in = s // TOKEN_TILE
    n_steps = b * tiles_in
    b_att = qkvs[0].shape[0]
    s_att = qkvs[0].shape[1] * qkvs[0].shape[2]
    steps_per_seq = s_att // TOKEN_TILE
    assert n_steps == b_att * steps_per_seq and s_att % ATTN_TILE == 0
    step_in = lambda g: (g // tiles_in, g % tiles_in)
    in_specs, out_specs, out_shape, scratch = _inproj_specs(b, s, b_off, b_full, step_in)

    qkv_bytes = N_GROUPS * s_att * QKV_W * 2
    qkv_mode = pl.Buffered(1) if 2 * qkv_bytes > QKV_WINDOW_BUDGET else None
    biases = []
    for gi, dil in enumerate(DILATIONS):
        d, seq = qkvs[gi].shape[1:3]
        assert d == dil
        biases.append(jnp.asarray(_attn_bias(dil, min(2 * Q_BLOCK, seq))))
        in_specs.append(pl.BlockSpec((None, d, seq, QKV_W), lambda g: (g // steps_per_seq, 0, 0, 0),
                                     pipeline_mode=qkv_mode))
    in_specs += [pl.BlockSpec(bs.shape, lambda g: (0, 0, 0, 0)) for bs in biases]
    tile_of = lambda g: g % steps_per_seq
    tile_of.fused = True
    for dil in DILATIONS:
        spec, sds = _attn_out(dil, att_full, s_att, lambda g: g // steps_per_seq + att_off, tile_of)
        out_specs += [spec, spec]
        out_shape += [sds, sds]
    scratch = scratch + _attn_scratch(2 * Q_BLOCK)
    operands = [x, mod3, g_pre, p["w_in"], p["mavg"], p["lng"], *qkvs, *biases]
    aliases = _accumulate(operands, in_specs, uv_prev, 3)
    aliases.update(_accumulate(operands, in_specs, att_prev, N_INPROJ_OUT))
    outs = pl.pallas_call(
        functools.partial(_front_kernel, seq_tokens=s_att, steps_per_seq=steps_per_seq,
                          n_alias=len(aliases)),
        grid=(n_steps,),
        in_specs=in_specs, out_specs=out_specs, out_shape=out_shape, scratch_shapes=scratch,
        input_output_aliases=aliases,
        compiler_params=_cparams(1),
        name="front",
    )(*operands)
    return outs[:3], outs[3:N_INPROJ_OUT], outs[N_INPROJ_OUT:]


def _mix_tile(x, gt1, o_refs, l_refs, u_ref, vn_ref, wcat_ref, bsm_ref, wout_ref, g_ref, perm_ref):
    t = x.shape[0]

    def slabs(ref, dil, base):
        if _class_major(dil):
            for r in range(dil):
                for sb in range(N_SLAB):
                    perm_ref[base + sb, pl.ds(r, t // dil, stride=dil), :] = ref[sb, r]
            ref, lo = perm_ref, base
        else:
            lo = 0
        return jnp.concatenate([ref[lo + sb] for sb in range(N_SLAB)], axis=-1)

    lses = [slabs(r, dil, 0) for r, dil in zip(l_refs, DILATIONS)]
    m = jnp.maximum(jnp.maximum(lses[0], lses[1]), lses[2])
    es = [jnp.exp(l - m) for l in lses]
    inv = 1.0 / (es[0] + es[1] + es[2])
    parts = [(es[gi] * inv * slabs(r, dil, N_SLAB)).astype(BF16)
             for gi, (r, dil) in enumerate(zip(o_refs, DILATIONS))]

    lane_group = lax.broadcasted_iota(jnp.int32, (GMLP_CHUNK, GMLP_W), 1) // GMLP_GROUP
    n_gm = GMLP_W // GMLP_GROUP
    gms = []
    for c in range(t // GMLP_CHUNK):
        rows = slice(c * GMLP_CHUNK, (c + 1) * GMLP_CHUNK)
        vn = vn_ref[rows, :]
        stacked = jnp.concatenate(
            [jnp.where(lane_group == g, vn, jnp.zeros_like(vn)) for g in range(n_gm)], axis=0)
        sv = jnp.dot(wcat_ref[...], stacked, preferred_element_type=F32) + bsm_ref[...]
        gms.append((u_ref[rows, :].astype(F32) * sv).astype(BF16))
    parts.append(jnp.concatenate(gms, axis=0))

    mix = jnp.dot(jnp.concatenate(parts, axis=-1), wout_ref[...], preferred_element_type=F32)
    return x + _rms(mix) * (gt1 * g_ref[...])


def _ffn_tile(x, sh2, sc2, gt2, gpre_ref, wgu_ref, wdown_ref, gpost_ref):
    h = (_rms(x) * (gpre_ref[...] * (1.0 + sc2)) + sh2).astype(BF16)
    acc = None
    for c in range(D_FF // FF_CHUNK):
        gate = jnp.dot(h, wgu_ref[:, FF_CHUNK * c:FF_CHUNK * (c + 1)], preferred_element_type=F32)
        up = jnp.dot(h, wgu_ref[:, D_FF + FF_CHUNK * c:D_FF + FF_CHUNK * (c + 1)],
                     preferred_element_type=F32)
        act = (gate * jax.nn.sigmoid(gate) * up).astype(BF16)
        part = jnp.dot(act, wdown_ref[FF_CHUNK * c:FF_CHUNK * (c + 1), :],
                       preferred_element_type=F32)
        acc = part if acc is None else acc + part
    return x + _rms(acc) * (gt2 * gpost_ref[...])


def _post_kernel(x_ref, mod_ref, o0_ref, l0_ref, o1_ref, l1_ref, o2_ref, l2_ref, u_ref, vn_ref,
                 wcat_ref, bsm_ref, wout_ref, gmix_ref, gpre_ref, wgu_ref, wdown_ref, gpost_ref,
                 y_ref, perm_ref):
    x1 = _mix_tile(x_ref[...], mod_ref[2:3, :], (o0_ref, o1_ref, o2_ref),
                   (l0_ref, l1_ref, l2_ref), u_ref, vn_ref, wcat_ref, bsm_ref, wout_ref, gmix_ref,
                   perm_ref)
    y_ref[...] = _ffn_tile(x1, mod_ref[3:4, :], mod_ref[4:5, :], mod_ref[5:6, :],
                           gpre_ref, wgu_ref, wdown_ref, gpost_ref)


def _post(x, mod3, attn_outs, u, vn, p, gains):
    b, s, d = x.shape
    tile = TOKEN_TILE
    tok = lambda w: pl.BlockSpec((None, tile, w), lambda bi, i: (bi, i, 0))
    slabs = []
    for dil in DILATIONS:
        if _class_major(dil):
            spec = pl.BlockSpec((None, N_SLAB, dil, tile // dil, LANES),
                                lambda bi, i: (bi, 0, 0, i, 0))
        else:
            spec = pl.BlockSpec((None, N_SLAB, tile, LANES), lambda bi, i: (bi, 0, i, 0))
        slabs += [spec, spec]
    const = lambda a: pl.BlockSpec(a.shape, lambda bi, i: (0,) * a.ndim)
    g_post_mix, g_pre_ffn, g_post_ffn = gains
    consts = (p["wcat"], p["bsm"], p["w_out"], g_post_mix, g_pre_ffn, p["w_gu"], p["w_down"],
              g_post_ffn)
    return pl.pallas_call(
        _post_kernel,
        grid=(b, s // tile),
        in_specs=[
            tok(d),
            pl.BlockSpec((None, N_MOD, d), lambda bi, i: (bi, 0, 0)),
        ] + slabs + [tok(GMLP_W), tok(GMLP_W)] + [const(a) for a in consts],
        out_specs=tok(d),
        out_shape=jax.ShapeDtypeStruct((b, s, d), F32),
        scratch_shapes=[pltpu.VMEM((2 * N_SLAB, tile, LANES), F32)],
        compiler_params=_cparams(2),
        name="post",
    )(x, mod3, *attn_outs, u, vn, *consts)


def _prep_layer(w_in, w_s, b_s, g_gmlp, w_out, w_gu, w_down):
    col_scale = np.ones((1, IN_W), np.float32)
    col_scale[:, :ATTN_W] = HEAD_DIM ** -0.5
    n_gm = GMLP_W // GMLP_GROUP
    wcat = jnp.transpose(w_s, (1, 0, 2)).reshape(GMLP_CHUNK, n_gm * GMLP_CHUNK).astype(BF16)
    bsm = jnp.repeat(b_s.T, GMLP_GROUP, axis=1).astype(F32)
    grp = np.arange(GMLP_W) // GMLP_GROUP
    mavg = jnp.asarray((grp[:, None] == grp[None, :]).astype(np.float32) / GMLP_GROUP, BF16)
    return dict(w_in=(w_in * col_scale).astype(BF16), wcat=wcat, bsm=bsm, mavg=mavg,
                lng=g_gmlp.reshape(1, GMLP_W), w_out=w_out.astype(BF16),
                w_gu=w_gu.astype(BF16), w_down=w_down.astype(BF16))


def _as_classes(q):
    q0, q1, q2 = q
    return q0.reshape(q0.shape[0], 1, q0.shape[1], q0.shape[2]), q1, q2


def _attn_all(qkvs, *, b_off, b_full, prev):
    outs = []
    for gi, (qkv, dil) in enumerate(zip(qkvs, DILATIONS)):
        outs += list(_attn(qkv, dil, b_off=b_off, b_full=b_full,
                           prev=None if prev is None else prev[2 * gi:2 * gi + 2]))
    return outs


def _layer(x_p, x_s, mod_p, mod_s, p, g_pre_mix, g_post_mix, g_pre_ffn, g_post_ffn):
    row = lambda g: g.reshape(1, D_MODEL)
    g_pre = row(g_pre_mix)
    gains = (row(g_post_mix), row(g_pre_ffn), row(g_post_ffn))
    groups = ((x_s, mod_s), (x_p, mod_p))

    def chunked(x):
        b, s = x.shape[:2]
        return (CHUNK_TOKENS % s == 0 and b % (CHUNK_TOKENS // s) == 0 and s % ATTN_TILE == 0)

    chunks = []
    for gid, (x, _) in enumerate(groups):
        per = CHUNK_TOKENS // x.shape[1] if all(chunked(g[0]) for g in groups) else x.shape[0]
        chunks += [(gid, off, per) for off in range(0, x.shape[0], per)]
    fuse = all(chunked(g[0]) for g in groups)

    uv = [None, None]
    att = [None, None]
    pending = None
    for gid, off, per in chunks:
        x, mod = groups[gid]
        if pending is not None and fuse:
            qk, pg, poff = pending
            q, uv[gid], att[pg] = _front(
                x, mod, g_pre, p, b_off=off, b=per, uv_prev=uv[gid], qkvs=qk, att_off=poff,
                att_full=groups[pg][0].shape[0], att_prev=att[pg])
        else:
            if pending is not None:
                qk, pg, poff = pending
                att[pg] = _attn_all(qk, b_off=poff, b_full=groups[pg][0].shape[0], prev=att[pg])
            q, uv[gid] = _inproj(x, mod, g_pre, p, b_off=off, b=per, uv_prev=uv[gid])
        pending = (_as_classes(q), gid, off)
    qk, pg, poff = pending
    att[pg] = _attn_all(qk, b_off=poff, b_full=groups[pg][0].shape[0], prev=att[pg])

    y_s, y_p = (_post(x, mod, att[gid], *uv[gid], p, gains)
                for gid, (x, mod) in enumerate(groups))
    return y_p, y_s


def kernel(x_prompt, x_sample, c_prompt, c_sample, w_ada, b_ada, g_pre_mix, w_in, w_s, b_s, g_gmlp, w_out, g_post_mix, g_pre_ffn, w_gu, w_down, g_post_ffn):
    n_p = c_prompt.shape[0]
    c_all = jnp.concatenate([c_prompt, c_sample], axis=0)
    y_prompt, y_sample = x_prompt, x_sample
    for l in range(w_ada.shape[0]):
        mod3 = _adaln(c_all, w_ada[l], b_ada[l]).reshape(c_all.shape[0], N_MOD, D_MODEL)
        p = _prep_layer(w_in[l], w_s[l], b_s[l], g_gmlp[l], w_out[l], w_gu[l], w_down[l])
        y_prompt, y_sample = _layer(y_prompt, y_sample, mod3[:n_p], mod3[n_p:], p,
                                    g_pre_mix[l], g_post_mix[l], g_pre_ffn[l], g_post_ffn[l])
    return (y_prompt, y_sample)
```

```python
import functools

import numpy as np
import jax
import jax.numpy as jnp
from jax import lax
from jax.experimental import pallas as pl
from jax.experimental.pallas import tpu as pltpu

F32 = jnp.float32
BF16 = jnp.bfloat16

D_MODEL = 1024
HEAD_DIM = 64
HEADS_PER_GROUP = 4
GROUP_W = HEADS_PER_GROUP * HEAD_DIM
DILATIONS = (1, 4, 16)
N_SIDE = 64
N_GROUPS = len(DILATIONS)
ATTN_W = N_GROUPS * GROUP_W
QKV_W = 3 * GROUP_W
GMLP_W = 256
GMLP_GROUP = 64
GMLP_CHUNK = 128
IN_W = 3 * ATTN_W + 2 * GMLP_W
D_FF = 2816
N_MOD = 6
RMS_EPS = 1e-6
LN_EPS = 1e-5
NEG_INF = -1e30

LANES = 128
N_SLAB = GROUP_W // LANES
Q_BLOCK = 128
ATTN_TILE = Q_BLOCK * DILATIONS[-1]
TOKEN_TILE = 512
CHUNK_TOKENS = 16384
FRONT_BLOCKS = TOKEN_TILE // Q_BLOCK
FF_CHUNK = 256
VMEM_LIMIT = 60 * 1024 * 1024
QKV_WINDOW_BUDGET = 20 * 1024 * 1024


def _cparams(n_axes):
    return pltpu.CompilerParams(
        dimension_semantics=("arbitrary",) * n_axes, vmem_limit_bytes=VMEM_LIMIT)


def _adaln_kernel(c_ref, w_ref, b_ref, o_ref):
    c = c_ref[...]
    a = (c * jax.nn.sigmoid(c)).astype(BF16)
    o_ref[...] = jnp.dot(a, w_ref[...].astype(BF16), preferred_element_type=F32) + b_ref[...]


def _adaln(c, w_ada, b_ada):
    n, d = c.shape
    nout = w_ada.shape[1]
    tn = 1024
    return pl.pallas_call(
        _adaln_kernel,
        grid=(nout // tn,),
        in_specs=[
            pl.BlockSpec((n, d), lambda j: (0, 0)),
            pl.BlockSpec((d, tn), lambda j: (0, j)),
            pl.BlockSpec((1, tn), lambda j: (0, j)),
        ],
        out_specs=pl.BlockSpec((n, tn), lambda j: (0, j)),
        out_shape=jax.ShapeDtypeStruct((n, nout), F32),
        compiler_params=_cparams(1),
        name="adaln",
    )(c, w_ada, b_ada.reshape(1, nout))


def _rms(x):
    return x * lax.rsqrt(jnp.mean(x * x, axis=-1, keepdims=True) + RMS_EPS)


def _inproj_phases(x_ref, mod_ref, g_ref, w_ref, mavg_ref, lng_ref,
                   q0_ref, q1_ref, q2_ref, u_ref, vn_ref, slab1_ref, slab2_ref):
    t = x_ref.shape[0]
    state = {}

    def norm():
        sh1 = mod_ref[0:1, :]
        sc1 = mod_ref[1:2, :]
        state["h"] = (_rms(x_ref[...]) * (g_ref[...] * (1.0 + sc1)) + sh1).astype(BF16)

    def qkv(gi):
        h = state["h"]
        return jnp.concatenate(
            [jnp.dot(h, w_ref[:, part * ATTN_W + gi * GROUP_W:part * ATTN_W + (gi + 1) * GROUP_W],
                     preferred_element_type=F32) for part in range(3)], axis=-1)

    def group0():
        q0_ref[...] = qkv(0).astype(BF16)

    def project(gi, slab_ref):
        def run():
            p = qkv(gi)
            for j in range(QKV_W // LANES):
                slab_ref[j] = p[:, j * LANES:(j + 1) * LANES]
        return run

    def permute(gi, slab_ref, out_ref):
        def run():
            d = DILATIONS[gi]
            for r in range(d):
                for j in range(QKV_W // LANES):
                    out_ref[r, :, j * LANES:(j + 1) * LANES] = (
                        slab_ref[j, pl.ds(r, t // d, stride=d), :].astype(BF16))
        return run

    def gmlp_dots():
        h = state["h"]
        base = 3 * ATTN_W
        state["gu"] = jnp.dot(h, w_ref[:, base:base + GMLP_W], preferred_element_type=F32)
        state["gv"] = jnp.dot(h, w_ref[:, base + GMLP_W:base + 2 * GMLP_W],
                              preferred_element_type=F32)

    def gmlp_tail():
        u_ref[...] = jax.nn.gelu(state["gu"]).astype(BF16)
        v = jax.nn.gelu(state["gv"])
        mu = jnp.dot(v.astype(BF16), mavg_ref[...], preferred_element_type=F32)
        vc = v - mu
        var = jnp.dot((vc * vc).astype(BF16), mavg_ref[...], preferred_element_type=F32)
        vn_ref[...] = (vc * lax.rsqrt(var + LN_EPS) * lng_ref[...]).astype(BF16)

    return dict(norm=norm, group0=group0, project1=project(1, slab1_ref),
                permute1=permute(1, slab1_ref, q1_ref), project2=project(2, slab2_ref),
                permute2=permute(2, slab2_ref, q2_ref), gmlp_dots=gmlp_dots, gmlp_tail=gmlp_tail)


FRONT_PLAN = (("norm", 0), ("gmlp_dots", 2), ("project2", 3), ("gmlp_tail", 1), ("group0", 3),
              ("permute2", 0), ("project1", 3), ("permute1", 0))


N_INPROJ_IN = 6
N_INPROJ_OUT = 5


def _inproj_kernel(*refs):
    phases = _inproj_phases(*refs)
    for name, _ in FRONT_PLAN:
        phases[name]()


def _inproj_specs(b, s, b_off, step_of):
    d = D_MODEL
    d1, d2 = DILATIONS[1], DILATIONS[2]
    tile = TOKEN_TILE

    def tok(w, off):
        return pl.BlockSpec((None, tile, w), lambda *g: (step_of(*g)[0] + off, step_of(*g)[1], 0))

    def cls(dil):
        return pl.BlockSpec((None, dil, tile // dil, QKV_W),
                            lambda *g: (step_of(*g)[0], 0, step_of(*g)[1], 0))

    const = lambda shape: pl.BlockSpec(shape, lambda *g: (0,) * len(shape))
    in_specs = [
        tok(d, b_off),
        pl.BlockSpec((None, N_MOD, d), lambda *g: (step_of(*g)[0] + b_off, 0, 0)),
        const((1, d)), const((d, IN_W)), const((GMLP_W, GMLP_W)), const((1, GMLP_W)),
    ]
    out_specs = [tok(QKV_W, 0), cls(d1), cls(d2), tok(GMLP_W, 0), tok(GMLP_W, 0)]
    out_shape = [
        jax.ShapeDtypeStruct((b, s, QKV_W), BF16),
        jax.ShapeDtypeStruct((b, d1, s // d1, QKV_W), BF16),
        jax.ShapeDtypeStruct((b, d2, s // d2, QKV_W), BF16),
        jax.ShapeDtypeStruct((b, s, GMLP_W), BF16),
        jax.ShapeDtypeStruct((b, s, GMLP_W), BF16),
    ]
    scratch = [pltpu.VMEM((QKV_W // LANES, tile, LANES), F32)] * 2
    return in_specs, out_specs, out_shape, scratch


def _inproj(x, mod3, g_pre, p, *, b_off, b):
    s = x.shape[1]
    in_specs, out_specs, out_shape, scratch = _inproj_specs(b, s, b_off, lambda bi, i: (bi, i))
    outs = pl.pallas_call(
        _inproj_kernel,
        grid=(b, s // TOKEN_TILE),
        in_specs=in_specs, out_specs=out_specs, out_shape=out_shape, scratch_shapes=scratch,
        compiler_params=_cparams(2),
        name="inproj",
    )(x, mod3, g_pre, p["w_in"], p["mavg"], p["lng"])
    return outs[:3], outs[3:]


def _attn_bias(dil, kw):
    group = DILATIONS.index(dil)
    n_heads = N_GROUPS * HEADS_PER_GROUP
    slopes = 2.0 ** (-8.0 * np.arange(1, n_heads + 1, dtype=np.float32) / n_heads)
    slopes = slopes[group * HEADS_PER_GROUP:(group + 1) * HEADS_PER_GROUP].astype(np.float32)
    n_var = 3 if kw == 2 * Q_BLOCK else 1
    i = np.arange(Q_BLOCK)[:, None]
    c = np.arange(kw)[None, :]
    out = np.empty((n_var, HEADS_PER_GROUP, Q_BLOCK, kw), np.float32)
    for v in range(n_var):
        rel = np.abs(c - N_SIDE * v - i)
        dist = (dil * rel).astype(np.float32)
        for h in range(HEADS_PER_GROUP):
            out[v, h] = np.where(rel <= N_SIDE, -slopes[h] * dist, np.float32(NEG_INF))
    return out


def _class_major(dil):
    return dil > FRONT_BLOCKS


def _attn_stages(qkv_ref, bias_ref, o_ref, l_ref, s_bufs, m_bufs, *, dil, seq, kw, coords):
    heads_per_tile = LANES // HEAD_DIM
    half = lax.broadcasted_iota(jnp.int32, (Q_BLOCK, LANES), 1) // HEAD_DIM

    def window(qs):
        return pl.multiple_of(jnp.clip(qs - N_SIDE, 0, seq - kw), N_SIDE)

    def scores(n, slot):
        r, qs, _ = coords(n)
        qs = pl.multiple_of(qs, Q_BLOCK)
        ks = window(qs)
        s_buf, m_buf = s_bufs[slot], m_bufs[slot]
        var = (qs - ks) // N_SIDE
        for h in range(HEADS_PER_GROUP):
            tile = slice((h // heads_per_tile) * LANES, (h // heads_per_tile + 1) * LANES)
            q = qkv_ref[r, pl.ds(qs, Q_BLOCK), tile]
            q = jnp.where(half == h % heads_per_tile, q, jnp.zeros_like(q))
            k = qkv_ref[r, pl.ds(ks, kw), GROUP_W + tile.start:GROUP_W + tile.stop]
            s = lax.dot_general(q, k, (((1,), (1,)), ((), ())), preferred_element_type=F32)
            bias = bias_ref[var, h]
            s = jnp.where(bias > 0.5 * NEG_INF, s + bias, NEG_INF)
            s_buf[h, :, 0:kw] = s
            m_buf[h] = jnp.broadcast_to(jnp.max(s, axis=-1, keepdims=True), (Q_BLOCK, LANES))

    def output(n, slot):
        r, qs, row0 = coords(n)
        ks = window(pl.multiple_of(qs, Q_BLOCK))
        s_buf, m_buf = s_bufs[slot], m_bufs[slot]
        if _class_major(dil):
            dst = lambda tl: (tl, row0)
        else:
            rows = pl.ds(row0, Q_BLOCK) if dil == 1 else pl.ds(row0, Q_BLOCK, stride=dil)
            dst = lambda tl: (tl, rows, slice(None))
        for tl in range(N_SLAB):
            o_pair, lse_pair = [], []
            for hp in range(heads_per_tile):
                h = tl * heads_per_tile + hp
                m = m_buf[h]
                p = [jnp.exp(s_buf[h, :, c * LANES:(c + 1) * LANES] - m)
                     for c in range(kw // LANES)]
                psum = p[0] if len(p) == 1 else p[0] + p[1]
                l = jnp.sum(psum, axis=-1, keepdims=True)
                pb = jnp.concatenate(p, axis=-1).astype(BF16)
                v = qkv_ref[r, pl.ds(ks, kw), 2 * GROUP_W + tl * LANES:2 * GROUP_W + (tl + 1) * LANES]
                o_pair.append(jnp.dot(pb, v, preferred_element_type=F32) * (1.0 / l))
                lse_pair.append(m + jnp.log(l))
            o_ref[dst(tl)] = jnp.where(half == 0, o_pair[0], o_pair[1])
            l_ref[dst(tl)] = jnp.where(half == 0, lse_pair[0], lse_pair[1])

    return scores, output


def _attn_kernel(qkv_ref, bias_ref, o_ref, l_ref, s_buf0, s_buf1, m_buf0, m_buf1,
                 *, dil, nq, seq, kw):
    t = pl.program_id(1)
    n_blocks = dil * nq
    assert n_blocks % 2 == 0

    def coords(n):
        r = n // nq
        j = n % nq
        return r, (t * nq + j) * Q_BLOCK, (r if _class_major(dil) else j * (Q_BLOCK * dil) + r)

    scores, output = _attn_stages(qkv_ref, bias_ref, o_ref, l_ref, (s_buf0, s_buf1),
                                  (m_buf0, m_buf1), dil=dil, seq=seq, kw=kw, coords=coords)
    scores(0, 0)

    def body(i, carry):
        n = 2 * i
        scores(n + 1, 1)
        output(n, 0)
        scores(n + 2, 0)
        output(n + 1, 1)
        return carry

    lax.fori_loop(0, n_blocks // 2 - 1, body, 0)
    scores(n_blocks - 1, 1)
    output(n_blocks - 2, 0)
    output(n_blocks - 1, 1)


def _attn_scratch(kw):
    return [
        pltpu.VMEM((HEADS_PER_GROUP, Q_BLOCK, kw), F32),
        pltpu.VMEM((HEADS_PER_GROUP, Q_BLOCK, kw), F32),
        pltpu.VMEM((HEADS_PER_GROUP, Q_BLOCK, LANES), F32),
        pltpu.VMEM((HEADS_PER_GROUP, Q_BLOCK, LANES), F32),
    ]


def _attn_out(dil, b, s, batch_of, tile_of):
    if _class_major(dil):
        rows = FRONT_BLOCKS if tile_of.fused else dil
        per_row = dil // rows
        spec = pl.BlockSpec(
            (None, N_SLAB, rows, Q_BLOCK, LANES),
            lambda *g: (batch_of(*g), 0, tile_of(*g) % per_row, tile_of(*g) // per_row, 0))
        return spec, jax.ShapeDtypeStruct((b, N_SLAB, dil, s // dil, LANES), F32)
    tile = TOKEN_TILE if tile_of.fused else min(s, ATTN_TILE)
    spec = pl.BlockSpec((None, N_SLAB, tile, LANES), lambda *g: (batch_of(*g), 0, tile_of(*g), 0))
    return spec, jax.ShapeDtypeStruct((b, N_SLAB, s, LANES), F32)


def _attn(qkv, dil):
    b, d, seq, w = qkv.shape
    s = d * seq
    tile = min(s, ATTN_TILE)
    kw = min(2 * Q_BLOCK, seq)
    nq = tile // (Q_BLOCK * dil)
    assert not _class_major(dil) or nq == 1
    bias = jnp.asarray(_attn_bias(dil, kw))
    tile_of = lambda bi, i: i
    tile_of.fused = False
    out_spec, out_sds = _attn_out(dil, b, s, lambda bi, i: bi, tile_of)
    return pl.pallas_call(
        functools.partial(_attn_kernel, dil=dil, nq=nq, seq=seq, kw=kw),
        grid=(b, s // tile),
        in_specs=[
            pl.BlockSpec((None, d, seq, w), lambda bi, i: (bi, 0, 0, 0)),
            pl.BlockSpec(bias.shape, lambda bi, i: (0, 0, 0, 0)),
        ],
        out_specs=[out_spec, out_spec],
        out_shape=[out_sds, out_sds],
        scratch_shapes=_attn_scratch(kw),
        compiler_params=_cparams(2),
        name=f"attn_d{dil}",
    )(qkv, bias)


def _front_kernel(*refs, seq_tokens, steps_per_seq):
    n_in = N_INPROJ_IN + 2 * N_GROUPS
    n_out = N_INPROJ_OUT + 2 * N_GROUPS
    ins, outs, scr = refs[:n_in], refs[n_in:n_in + n_out], refs[n_in + n_out:]
    qkv_refs = ins[N_INPROJ_IN:N_INPROJ_IN + N_GROUPS]
    bias_refs = ins[N_INPROJ_IN + N_GROUPS:]
    attn_outs = outs[N_INPROJ_OUT:]
    slab_refs, s_bufs, m_bufs = scr[0:2], scr[2:4], scr[4:6]
    w = pl.program_id(0) % steps_per_seq

    units = []
    for gi, dil in enumerate(DILATIONS):
        seq = seq_tokens // dil
        kw = min(2 * Q_BLOCK, seq)

        def coords(n, dil=dil):
            m = w * FRONT_BLOCKS + n
            r, j = m % dil, m // dil
            if _class_major(dil):
                row0 = n
            else:
                row0 = (n // dil) * Q_BLOCK * dil + n % dil
            return r, j * Q_BLOCK, row0

        scores, output = _attn_stages(
            qkv_refs[gi], bias_refs[gi], attn_outs[2 * gi], attn_outs[2 * gi + 1],
            s_bufs, m_bufs, dil=dil, seq=seq, kw=kw, coords=coords)
        units += [(scores, output, n) for n in range(FRONT_BLOCKS)]

    phases = _inproj_phases(*ins[:N_INPROJ_IN], *outs[:N_INPROJ_OUT], *slab_refs)
    n_units = len(units)
    assert sum(cnt for _, cnt in FRONT_PLAN) == n_units
    units[0][0](units[0][2], 0)
    k = 0
    for name, cnt in FRONT_PLAN:
        phases[name]()
        for _ in range(cnt):
            if k + 1 < n_units:
                units[k + 1][0](units[k + 1][2], (k + 1) % 2)
            units[k][1](units[k][2], k % 2)
            k += 1


def _front(x, mod3, g_pre, p, *, b_off, b, qkvs):
    s = x.shape[1]
    tiles_in = s // TOKEN_TILE
    n_steps = b * tiles_in
    b_att = qkvs[0].shape[0]
    s_att = qkvs[0].shape[1] * qkvs[0].shape[2]
    steps_per_seq = s_att // TOKEN_TILE
    assert n_steps == b_att * steps_per_seq and s_att % ATTN_TILE == 0
    step_in = lambda g: (g // tiles_in, g % tiles_in)
    in_specs, out_specs, out_shape, scratch = _inproj_specs(b, s, b_off, step_in)

    qkv_bytes = N_GROUPS * s_att * QKV_W * 2
    qkv_mode = pl.Buffered(1) if 2 * qkv_bytes > QKV_WINDOW_BUDGET else None
    biases = []
    for gi, dil in enumerate(DILATIONS):
        d, seq = qkvs[gi].shape[1:3]
        assert d == dil
        biases.append(jnp.asarray(_attn_bias(dil, min(2 * Q_BLOCK, seq))))
        in_specs.append(pl.BlockSpec((None, d, seq, QKV_W), lambda g: (g // steps_per_seq, 0, 0, 0),
                                     pipeline_mode=qkv_mode))
    in_specs += [pl.BlockSpec(bs.shape, lambda g: (0, 0, 0, 0)) for bs in biases]
    tile_of = lambda g: g % steps_per_seq
    tile_of.fused = True
    for dil in DILATIONS:
        spec, sds = _attn_out(dil, b_att, s_att, lambda g: g // steps_per_seq, tile_of)
        out_specs += [spec, spec]
        out_shape += [sds, sds]
    scratch = scratch + _attn_scratch(2 * Q_BLOCK)
    outs = pl.pallas_call(
        functools.partial(_front_kernel, seq_tokens=s_att, steps_per_seq=steps_per_seq),
        grid=(n_steps,),
        in_specs=in_specs, out_specs=out_specs, out_shape=out_shape, scratch_shapes=scratch,
        compiler_params=_cparams(1),
        name="front",
    )(x, mod3, g_pre, p["w_in"], p["mavg"], p["lng"], *qkvs, *biases)
    return outs[:3], outs[3:N_INPROJ_OUT], outs[N_INPROJ_OUT:]


def _mix_tile(x, gt1, o_refs, l_refs, u_ref, vn_ref, wcat_ref, bsm_ref, wout_ref, g_ref, perm_ref):
    t = x.shape[0]

    def slabs(ref, dil, base):
        if _class_major(dil):
            for r in range(dil):
                for sb in range(N_SLAB):
                    perm_ref[base + sb, pl.ds(r, t // dil, stride=dil), :] = ref[sb, r]
            ref, lo = perm_ref, base
        else:
            lo = 0
        return jnp.concatenate([ref[lo + sb] for sb in range(N_SLAB)], axis=-1)

    lses = [slabs(r, dil, 0) for r, dil in zip(l_refs, DILATIONS)]
    m = jnp.maximum(jnp.maximum(lses[0], lses[1]), lses[2])
    es = [jnp.exp(l - m) for l in lses]
    inv = 1.0 / (es[0] + es[1] + es[2])
    parts = [(es[gi] * inv * slabs(r, dil, N_SLAB)).astype(BF16)
             for gi, (r, dil) in enumerate(zip(o_refs, DILATIONS))]

    lane_group = lax.broadcasted_iota(jnp.int32, (GMLP_CHUNK, GMLP_W), 1) // GMLP_GROUP
    n_gm = GMLP_W // GMLP_GROUP
    gms = []
    for c in range(t // GMLP_CHUNK):
        rows = slice(c * GMLP_CHUNK, (c + 1) * GMLP_CHUNK)
        vn = vn_ref[rows, :]
        stacked = jnp.concatenate(
            [jnp.where(lane_group == g, vn, jnp.zeros_like(vn)) for g in range(n_gm)], axis=0)
        sv = jnp.dot(wcat_ref[...], stacked, preferred_element_type=F32) + bsm_ref[...]
        gms.append((u_ref[rows, :].astype(F32) * sv).astype(BF16))
    parts.append(jnp.concatenate(gms, axis=0))

    mix = jnp.dot(jnp.concatenate(parts, axis=-1), wout_ref[...], preferred_element_type=F32)
    return x + _rms(mix) * (gt1 * g_ref[...])


def _ffn_tile(x, sh2, sc2, gt2, gpre_ref, wgu_ref, wdown_ref, gpost_ref):
    h = (_rms(x) * (gpre_ref[...] * (1.0 + sc2)) + sh2).astype(BF16)
    acc = None
    for c in range(D_FF // FF_CHUNK):
        gate = jnp.dot(h, wgu_ref[:, FF_CHUNK * c:FF_CHUNK * (c + 1)], preferred_element_type=F32)
        up = jnp.dot(h, wgu_ref[:, D_FF + FF_CHUNK * c:D_FF + FF_CHUNK * (c + 1)],
                     preferred_element_type=F32)
        act = (gate * jax.nn.sigmoid(gate) * up).astype(BF16)
        part = jnp.dot(act, wdown_ref[FF_CHUNK * c:FF_CHUNK * (c + 1), :],
                       preferred_element_type=F32)
        acc = part if acc is None else acc + part
    return x + _rms(acc) * (gt2 * gpost_ref[...])


N_CHUNK_IN = 2 * N_GROUPS + 2


def _post_kernel(x_ref, mod_ref, *refs, bounds):
    n = len(bounds) * N_CHUNK_IN
    wcat_ref, bsm_ref, wout_ref, gmix_ref, gpre_ref, wgu_ref, wdown_ref, gpost_ref = refs[n:n + 8]
    y_ref, perm_ref = refs[n + 8:]
    bi = pl.program_id(0)

    def run(chunk):
        o0, l0, o1, l1, o2, l2, u_ref, vn_ref = chunk
        x1 = _mix_tile(x_ref[...], mod_ref[2:3, :], (o0, o1, o2), (l0, l1, l2), u_ref, vn_ref,
                       wcat_ref, bsm_ref, wout_ref, gmix_ref, perm_ref)
        y_ref[...] = _ffn_tile(x1, mod_ref[3:4, :], mod_ref[4:5, :], mod_ref[5:6, :],
                               gpre_ref, wgu_ref, wdown_ref, gpost_ref)

    for k, (off, cnt) in enumerate(bounds):
        chunk = refs[k * N_CHUNK_IN:(k + 1) * N_CHUNK_IN]
        if len(bounds) == 1:
            run(chunk)
        else:
            pl.when((bi >= off) & (bi < off + cnt))(functools.partial(run, chunk))


def _post(x, mod3, chunks, p, gains):
    b, s, d = x.shape
    tile = TOKEN_TILE
    tok = lambda w: pl.BlockSpec((None, tile, w), lambda bi, i: (bi, i, 0))
    const = lambda a: pl.BlockSpec(a.shape, lambda bi, i: (0,) * a.ndim)
    in_specs = [tok(d), pl.BlockSpec((None, N_MOD, d), lambda bi, i: (bi, 0, 0))]
    operands = [x, mod3]
    bounds = []
    for off, att, u, vn in chunks:
        cnt = u.shape[0]
        bounds.append((off, cnt))
        local = lambda bi, off=off, cnt=cnt: jnp.clip(bi - off, 0, cnt - 1)
        for dil in DILATIONS:
            if _class_major(dil):
                spec = pl.BlockSpec((None, N_SLAB, dil, tile // dil, LANES),
                                    lambda bi, i, local=local: (local(bi), 0, 0, i, 0))
            else:
                spec = pl.BlockSpec((None, N_SLAB, tile, LANES),
                                    lambda bi, i, local=local: (local(bi), 0, i, 0))
            in_specs += [spec, spec]
        in_specs += [pl.BlockSpec((None, tile, GMLP_W),
                                  lambda bi, i, local=local: (local(bi), i, 0))] * 2
        operands += [*att, u, vn]
    g_post_mix, g_pre_ffn, g_post_ffn = gains
    consts = (p["wcat"], p["bsm"], p["w_out"], g_post_mix, g_pre_ffn, p["w_gu"], p["w_down"],
              g_post_ffn)
    return pl.pallas_call(
        functools.partial(_post_kernel, bounds=tuple(bounds)),
        grid=(b, s // tile),
        in_specs=in_specs + [const(a) for a in consts],
        out_specs=tok(d),
        out_shape=jax.ShapeDtypeStruct((b, s, d), F32),
        scratch_shapes=[pltpu.VMEM((2 * N_SLAB, tile, LANES), F32)],
        compiler_params=_cparams(2),
        name="post",
    )(*operands, *consts)


def _prep_layer(w_in, w_s, b_s, g_gmlp, w_out, w_gu, w_down):
    col_scale = np.ones((1, IN_W), np.float32)
    col_scale[:, :ATTN_W] = HEAD_DIM ** -0.5
    n_gm = GMLP_W // GMLP_GROUP
    wcat = jnp.transpose(w_s, (1, 0, 2)).reshape(GMLP_CHUNK, n_gm * GMLP_CHUNK).astype(BF16)
    bsm = jnp.repeat(b_s.T, GMLP_GROUP, axis=1).astype(F32)
    grp = np.arange(GMLP_W) // GMLP_GROUP
    mavg = jnp.asarray((grp[:, None] == grp[None, :]).astype(np.float32) / GMLP_GROUP, BF16)
    return dict(w_in=(w_in * col_scale).astype(BF16), wcat=wcat, bsm=bsm, mavg=mavg,
                lng=g_gmlp.reshape(1, GMLP_W), w_out=w_out.astype(BF16),
                w_gu=w_gu.astype(BF16), w_down=w_down.astype(BF16))


def _as_classes(q):
    q0, q1, q2 = q
    return q0.reshape(q0.shape[0], 1, q0.shape[1], q0.shape[2]), q1, q2


def _attn_all(qkvs):
    outs = []
    for qkv, dil in zip(qkvs, DILATIONS):
        outs += list(_attn(qkv, dil))
    return outs


def _layer(x_p, x_s, mod_p, mod_s, p, g_pre_mix, g_post_mix, g_pre_ffn, g_post_ffn):
    row = lambda g: g.reshape(1, D_MODEL)
    g_pre = row(g_pre_mix)
    gains = (row(g_post_mix), row(g_pre_ffn), row(g_post_ffn))
    groups = ((x_s, mod_s), (x_p, mod_p))

    def chunked(x):
        b, s = x.shape[:2]
        return (CHUNK_TOKENS % s == 0 and b % (CHUNK_TOKENS // s) == 0 and s % ATTN_TILE == 0)

    fuse = all(chunked(g[0]) for g in groups)
    chunks = []
    for gid, (x, _) in enumerate(groups):
        per = CHUNK_TOKENS // x.shape[1] if fuse else x.shape[0]
        chunks += [(gid, off, per) for off in range(0, x.shape[0], per)]

    done = [[], []]
    pending = None
    for gid, off, per in chunks:
        x, mod = groups[gid]
        if pending is not None and fuse:
            qk, pg, poff, pu, pvn = pending
            q, (u, vn), att = _front(x, mod, g_pre, p, b_off=off, b=per, qkvs=qk)
            done[pg].append((poff, att, pu, pvn))
        else:
            if pending is not None:
                qk, pg, poff, pu, pvn = pending
                done[pg].append((poff, _attn_all(qk), pu, pvn))
            q, (u, vn) = _inproj(x, mod, g_pre, p, b_off=off, b=per)
        pending = (_as_classes(q), gid, off, u, vn)
    qk, pg, poff, pu, pvn = pending
    done[pg].append((poff, _attn_all(qk), pu, pvn))

    y_s, y_p = (_post(x, mod, done[gid], p, gains) for gid, (x, mod) in enumerate(groups))
    return y_p, y_s


def kernel(x_prompt, x_sample, c_prompt, c_sample, w_ada, b_ada, g_pre_mix, w_in, w_s, b_s, g_gmlp, w_out, g_post_mix, g_pre_ffn, w_gu, w_down, g_post_ffn):
    n_p = c_prompt.shape[0]
    c_all = jnp.concatenate([c_prompt, c_sample], axis=0)
    y_prompt, y_sample = x_prompt, x_sample
    for l in range(w_ada.shape[0]):
        mod3 = _adaln(c_all, w_ada[l], b_ada[l]).reshape(c_all.shape[0], N_MOD, D_MODEL)
        p = _prep_layer(w_in[l], w_s[l], b_s[l], g_gmlp[l], w_out[l], w_gu[l], w_down[l])
        y_prompt, y_sample = _layer(y_prompt, y_sample, mod3[:n_p], mod3[n_p:], p,
                                    g_pre_mix[l], g_post_mix[l], g_pre_ffn[l], g_post_ffn[l])
    return (y_prompt, y_sample)
```

```python
import functools

import numpy as np
import jax
import jax.numpy as jnp
from jax import lax
from jax.experimental import pallas as pl
from jax.experimental.pallas import tpu as pltpu

F32 = jnp.float32
BF16 = jnp.bfloat16

D_MODEL = 1024
HEAD_DIM = 64
HEADS_PER_GROUP = 4
GROUP_W = HEADS_PER_GROUP * HEAD_DIM
DILATIONS = (1, 4, 16)
N_SIDE = 64
N_GROUPS = len(DILATIONS)
ATTN_W = N_GROUPS * GROUP_W
QKV_W = 3 * GROUP_W
GMLP_W = 256
GMLP_GROUP = 64
GMLP_CHUNK = 128
IN_W = 3 * ATTN_W + 2 * GMLP_W
D_FF = 2816
N_MOD = 6
RMS_EPS = 1e-6
LN_EPS = 1e-5
NEG_INF = -1e30

LANES = 128
N_SLAB = GROUP_W // LANES
Q_BLOCK = 128
ATTN_TILE = Q_BLOCK * DILATIONS[-1]
TOKEN_TILE = 512
CHUNK_TOKENS = 16384
FRONT_BLOCKS = TOKEN_TILE // Q_BLOCK
FF_CHUNK = 256
VMEM_LIMIT = 60 * 1024 * 1024
QKV_WINDOW_BUDGET = 20 * 1024 * 1024


def _cparams(n_axes):
    return pltpu.CompilerParams(
        dimension_semantics=("arbitrary",) * n_axes, vmem_limit_bytes=VMEM_LIMIT)


def _adaln_kernel(c_ref, w_ref, b_ref, o_ref):
    c = c_ref[...]
    a = (c * jax.nn.sigmoid(c)).astype(BF16)
    o_ref[...] = jnp.dot(a, w_ref[...].astype(BF16), preferred_element_type=F32) + b_ref[...]


def _adaln(c, w_ada, b_ada):
    n, d = c.shape
    nout = w_ada.shape[1]
    tn = 1024
    return pl.pallas_call(
        _adaln_kernel,
        grid=(nout // tn,),
        in_specs=[
            pl.BlockSpec((n, d), lambda j: (0, 0)),
            pl.BlockSpec((d, tn), lambda j: (0, j)),
            pl.BlockSpec((1, tn), lambda j: (0, j)),
        ],
        out_specs=pl.BlockSpec((n, tn), lambda j: (0, j)),
        out_shape=jax.ShapeDtypeStruct((n, nout), F32),
        compiler_params=_cparams(1),
        name="adaln",
    )(c, w_ada, b_ada.reshape(1, nout))


def _rms(x):
    return x * lax.rsqrt(jnp.mean(x * x, axis=-1, keepdims=True) + RMS_EPS)


def _inproj_phases(x_ref, mod_ref, g_ref, w_ref, mavg_ref, lng_ref,
                   q0_ref, q1_ref, q2_ref, u_ref, vn_ref, slab1_ref, slab2_ref):
    t = x_ref.shape[0]
    state = {}

    def norm():
        sh1 = mod_ref[0:1, :]
        sc1 = mod_ref[1:2, :]
        state["h"] = (_rms(x_ref[...]) * (g_ref[...] * (1.0 + sc1)) + sh1).astype(BF16)

    def qkv(gi):
        h = state["h"]
        return jnp.concatenate(
            [jnp.dot(h, w_ref[:, part * ATTN_W + gi * GROUP_W:part * ATTN_W + (gi + 1) * GROUP_W],
                     preferred_element_type=F32) for part in range(3)], axis=-1)

    def group0():
        q0_ref[...] = qkv(0).astype(BF16)

    def project(gi, slab_ref):
        def run():
            p = qkv(gi)
            for j in range(QKV_W // LANES):
                slab_ref[j] = p[:, j * LANES:(j + 1) * LANES]
        return run

    def permute(gi, slab_ref, out_ref):
        def run():
            d = DILATIONS[gi]
            for r in range(d):
                for j in range(QKV_W // LANES):
                    out_ref[r, :, j * LANES:(j + 1) * LANES] = (
                        slab_ref[j, pl.ds(r, t // d, stride=d), :].astype(BF16))
        return run

    def gmlp_dots():
        h = state["h"]
        base = 3 * ATTN_W
        state["gu"] = jnp.dot(h, w_ref[:, base:base + GMLP_W], preferred_element_type=F32)
        state["gv"] = jnp.dot(h, w_ref[:, base + GMLP_W:base + 2 * GMLP_W],
                              preferred_element_type=F32)

    def gmlp_tail():
        u_ref[...] = jax.nn.gelu(state["gu"]).astype(BF16)
        v = jax.nn.gelu(state["gv"])
        mu = jnp.dot(v.astype(BF16), mavg_ref[...], preferred_element_type=F32)
        vc = v - mu
        var = jnp.dot((vc * vc).astype(BF16), mavg_ref[...], preferred_element_type=F32)
        vn_ref[...] = (vc * lax.rsqrt(var + LN_EPS) * lng_ref[...]).astype(BF16)

    return dict(norm=norm, group0=group0, project1=project(1, slab1_ref),
                permute1=permute(1, slab1_ref, q1_ref), project2=project(2, slab2_ref),
                permute2=permute(2, slab2_ref, q2_ref), gmlp_dots=gmlp_dots, gmlp_tail=gmlp_tail)


FRONT_PLAN = (("norm", 0), ("gmlp_dots", 2), ("project2", 3), ("gmlp_tail", 1), ("group0", 3),
              ("permute2", 0), ("project1", 3), ("permute1", 0))


N_INPROJ_IN = 6
N_INPROJ_OUT = 5


def _inproj_kernel(*refs):
    phases = _inproj_phases(*refs)
    for name, _ in FRONT_PLAN:
        phases[name]()


def _inproj_specs(b, s, b_off, step_of):
    d = D_MODEL
    d1, d2 = DILATIONS[1], DILATIONS[2]
    tile = TOKEN_TILE

    def tok(w, off):
        return pl.BlockSpec((None, tile, w), lambda *g: (step_of(*g)[0] + off, step_of(*g)[1], 0))

    def cls(dil):
        return pl.BlockSpec((None, dil, tile // dil, QKV_W),
                            lambda *g: (step_of(*g)[0], 0, step_of(*g)[1], 0))

    const = lambda shape: pl.BlockSpec(shape, lambda *g: (0,) * len(shape))
    in_specs = [
        tok(d, b_off),
        pl.BlockSpec((None, N_MOD, d), lambda *g: (step_of(*g)[0] + b_off, 0, 0)),
        const((1, d)), const((d, IN_W)), const((GMLP_W, GMLP_W)), const((1, GMLP_W)),
    ]
    out_specs = [tok(QKV_W, 0), cls(d1), cls(d2), tok(GMLP_W, 0), tok(GMLP_W, 0)]
    out_shape = [
        jax.ShapeDtypeStruct((b, s, QKV_W), BF16),
        jax.ShapeDtypeStruct((b, d1, s // d1, QKV_W), BF16),
        jax.ShapeDtypeStruct((b, d2, s // d2, QKV_W), BF16),
        jax.ShapeDtypeStruct((b, s, GMLP_W), BF16),
        jax.ShapeDtypeStruct((b, s, GMLP_W), BF16),
    ]
    scratch = [pltpu.VMEM((QKV_W // LANES, tile, LANES), F32)] * 2
    return in_specs, out_specs, out_shape, scratch


def _inproj(x, mod3, g_pre, p, *, b_off, b):
    s = x.shape[1]
    in_specs, out_specs, out_shape, scratch = _inproj_specs(b, s, b_off, lambda bi, i: (bi, i))
    outs = pl.pallas_call(
        _inproj_kernel,
        grid=(b, s // TOKEN_TILE),
        in_specs=in_specs, out_specs=out_specs, out_shape=out_shape, scratch_shapes=scratch,
        compiler_params=_cparams(2),
        name="inproj",
    )(x, mod3, g_pre, p["w_in"], p["mavg"], p["lng"])
    return outs[:3], outs[3:]


def _attn_bias(dil, kw):
    group = DILATIONS.index(dil)
    n_heads = N_GROUPS * HEADS_PER_GROUP
    slopes = 2.0 ** (-8.0 * np.arange(1, n_heads + 1, dtype=np.float32) / n_heads)
    slopes = slopes[group * HEADS_PER_GROUP:(group + 1) * HEADS_PER_GROUP].astype(np.float32)
    n_var = 3 if kw == 2 * Q_BLOCK else 1
    i = np.arange(Q_BLOCK)[:, None]
    c = np.arange(kw)[None, :]
    out = np.empty((n_var, HEADS_PER_GROUP, Q_BLOCK, kw), np.float32)
    for v in range(n_var):
        rel = np.abs(c - N_SIDE * v - i)
        dist = (dil * rel).astype(np.float32)
        for h in range(HEADS_PER_GROUP):
            out[v, h] = np.where(rel <= N_SIDE, -slopes[h] * dist, np.float32(NEG_INF))
    return out


def _class_major(dil):
    return dil > FRONT_BLOCKS


def _attn_stages(qkv_ref, bias_ref, o_ref, l_ref, s_bufs, m_bufs, *, dil, seq, kw, coords):
    heads_per_tile = LANES // HEAD_DIM
    half = lax.broadcasted_iota(jnp.int32, (Q_BLOCK, LANES), 1) // HEAD_DIM

    def window(qs):
        return pl.multiple_of(jnp.clip(qs - N_SIDE, 0, seq - kw), N_SIDE)

    def scores(n, slot):
        r, qs, _ = coords(n)
        qs = pl.multiple_of(qs, Q_BLOCK)
        ks = window(qs)
        s_buf, m_buf = s_bufs[slot], m_bufs[slot]
        var = (qs - ks) // N_SIDE
        for h in range(HEADS_PER_GROUP):
            tile = slice((h // heads_per_tile) * LANES, (h // heads_per_tile + 1) * LANES)
            q = qkv_ref[r, pl.ds(qs, Q_BLOCK), tile]
            q = jnp.where(half == h % heads_per_tile, q, jnp.zeros_like(q))
            k = qkv_ref[r, pl.ds(ks, kw), GROUP_W + tile.start:GROUP_W + tile.stop]
            s = lax.dot_general(q, k, (((1,), (1,)), ((), ())), preferred_element_type=F32)
            bias = bias_ref[var, h]
            s = jnp.where(bias > 0.5 * NEG_INF, s + bias, NEG_INF)
            s_buf[h, :, 0:kw] = s
            m_buf[h] = jnp.broadcast_to(jnp.max(s, axis=-1, keepdims=True), (Q_BLOCK, LANES))

    def output(n, slot):
        r, qs, row0 = coords(n)
        ks = window(pl.multiple_of(qs, Q_BLOCK))
        s_buf, m_buf = s_bufs[slot], m_bufs[slot]
        if _class_major(dil):
            dst = lambda tl: (tl, row0)
        else:
            rows = pl.ds(row0, Q_BLOCK) if dil == 1 else pl.ds(row0, Q_BLOCK, stride=dil)
            dst = lambda tl: (tl, rows, slice(None))
        for tl in range(N_SLAB):
            o_pair, lse_pair = [], []
            for hp in range(heads_per_tile):
                h = tl * heads_per_tile + hp
                m = m_buf[h]
                p = [jnp.exp(s_buf[h, :, c * LANES:(c + 1) * LANES] - m)
                     for c in range(kw // LANES)]
                psum = p[0] if len(p) == 1 else p[0] + p[1]
                l = jnp.sum(psum, axis=-1, keepdims=True)
                pb = jnp.concatenate(p, axis=-1).astype(BF16)
                v = qkv_ref[r, pl.ds(ks, kw), 2 * GROUP_W + tl * LANES:2 * GROUP_W + (tl + 1) * LANES]
                o_pair.append(jnp.dot(pb, v, preferred_element_type=F32) * (1.0 / l))
                lse_pair.append(m + jnp.log(l))
            o_ref[dst(tl)] = jnp.where(half == 0, o_pair[0], o_pair[1])
            l_ref[dst(tl)] = jnp.where(half == 0, lse_pair[0], lse_pair[1])

    return scores, output


def _attn_kernel(qkv_ref, bias_ref, o_ref, l_ref, s_buf0, s_buf1, m_buf0, m_buf1,
                 *, dil, nq, seq, kw):
    t = pl.program_id(1)
    n_blocks = dil * nq
    assert n_blocks % 2 == 0

    def coords(n):
        r = n // nq
        j = n % nq
        return r, (t * nq + j) * Q_BLOCK, (r if _class_major(dil) else j * (Q_BLOCK * dil) + r)

    scores, output = _attn_stages(qkv_ref, bias_ref, o_ref, l_ref, (s_buf0, s_buf1),
                                  (m_buf0, m_buf1), dil=dil, seq=seq, kw=kw, coords=coords)
    scores(0, 0)

    def body(i, carry):
        n = 2 * i
        scores(n + 1, 1)
        output(n, 0)
        scores(n + 2, 0)
        output(n + 1, 1)
        return carry

    lax.fori_loop(0, n_blocks // 2 - 1, body, 0)
    scores(n_blocks - 1, 1)
    output(n_blocks - 2, 0)
    output(n_blocks - 1, 1)


def _attn_scratch(kw):
    return [
        pltpu.VMEM((HEADS_PER_GROUP, Q_BLOCK, kw), F32),
        pltpu.VMEM((HEADS_PER_GROUP, Q_BLOCK, kw), F32),
        pltpu.VMEM((HEADS_PER_GROUP, Q_BLOCK, LANES), F32),
        pltpu.VMEM((HEADS_PER_GROUP, Q_BLOCK, LANES), F32),
    ]


def _attn_out(dil, b, s, batch_of, tile_of):
    if _class_major(dil):
        rows = FRONT_BLOCKS if tile_of.fused else dil
        per_row = dil // rows
        spec = pl.BlockSpec(
            (None, N_SLAB, rows, Q_BLOCK, LANES),
            lambda *g: (batch_of(*g), 0, tile_of(*g) % per_row, tile_of(*g) // per_row, 0))
        return spec, jax.ShapeDtypeStruct((b, N_SLAB, dil, s // dil, LANES), F32)
    tile = TOKEN_TILE if tile_of.fused else min(s, ATTN_TILE)
    spec = pl.BlockSpec((None, N_SLAB, tile, LANES), lambda *g: (batch_of(*g), 0, tile_of(*g), 0))
    return spec, jax.ShapeDtypeStruct((b, N_SLAB, s, LANES), F32)


def _attn(qkv, dil):
    b, d, seq, w = qkv.shape
    s = d * seq
    tile = min(s, ATTN_TILE)
    kw = min(2 * Q_BLOCK, seq)
    nq = tile // (Q_BLOCK * dil)
    assert not _class_major(dil) or nq == 1
    bias = jnp.asarray(_attn_bias(dil, kw))
    tile_of = lambda bi, i: i
    tile_of.fused = False
    out_spec, out_sds = _attn_out(dil, b, s, lambda bi, i: bi, tile_of)
    return pl.pallas_call(
        functools.partial(_attn_kernel, dil=dil, nq=nq, seq=seq, kw=kw),
        grid=(b, s // tile),
        in_specs=[
            pl.BlockSpec((None, d, seq, w), lambda bi, i: (bi, 0, 0, 0)),
            pl.BlockSpec(bias.shape, lambda bi, i: (0, 0, 0, 0)),
        ],
        out_specs=[out_spec, out_spec],
        out_shape=[out_sds, out_sds],
        scratch_shapes=_attn_scratch(kw),
        compiler_params=_cparams(2),
        name=f"attn_d{dil}",
    )(qkv, bias)


def _front_kernel(*refs, seq_tokens, steps_per_seq, project):
    n_proj_in, n_proj_out = (N_INPROJ_IN, N_INPROJ_OUT) if project else (0, 0)
    n_in = n_proj_in + 2 * N_GROUPS
    n_out = n_proj_out + 2 * N_GROUPS
    ins, outs, scr = refs[:n_in], refs[n_in:n_in + n_out], refs[n_in + n_out:]
    qkv_refs = ins[n_proj_in:n_proj_in + N_GROUPS]
    bias_refs = ins[n_proj_in + N_GROUPS:]
    attn_outs = outs[n_proj_out:]
    slab_refs, s_bufs, m_bufs = scr[:-4], scr[-4:-2], scr[-2:]
    w = pl.program_id(0) % steps_per_seq

    units = []
    for gi, dil in enumerate(DILATIONS):
        seq = seq_tokens // dil
        kw = min(2 * Q_BLOCK, seq)

        def coords(n, dil=dil):
            m = w * FRONT_BLOCKS + n
            r, j = m % dil, m // dil
            if _class_major(dil):
                row0 = n
            else:
                row0 = (n // dil) * Q_BLOCK * dil + n % dil
            return r, j * Q_BLOCK, row0

        scores, output = _attn_stages(
            qkv_refs[gi], bias_refs[gi], attn_outs[2 * gi], attn_outs[2 * gi + 1],
            s_bufs, m_bufs, dil=dil, seq=seq, kw=kw, coords=coords)
        units += [(scores, output, n) for n in range(FRONT_BLOCKS)]

    n_units = len(units)
    if project:
        phases = _inproj_phases(*ins[:N_INPROJ_IN], *outs[:N_INPROJ_OUT], *slab_refs)
        plan = FRONT_PLAN
    else:
        phases = {"attention": lambda: None}
        plan = (("attention", n_units),)
    assert sum(cnt for _, cnt in plan) == n_units
    units[0][0](units[0][2], 0)
    k = 0
    for name, cnt in plan:
        phases[name]()
        for _ in range(cnt):
            if k + 1 < n_units:
                units[k + 1][0](units[k + 1][2], (k + 1) % 2)
            units[k][1](units[k][2], k % 2)
            k += 1


def _front(x, mod3, g_pre, p, *, b_off, b, qkvs):
    b_att = qkvs[0].shape[0]
    s_att = qkvs[0].shape[1] * qkvs[0].shape[2]
    steps_per_seq = s_att // TOKEN_TILE
    n_steps = b_att * steps_per_seq
    assert s_att % ATTN_TILE == 0
    project = x is not None
    in_specs, out_specs, out_shape, scratch, operands = [], [], [], [], []
    if project:
        s = x.shape[1]
        tiles_in = s // TOKEN_TILE
        assert n_steps == b * tiles_in
        step_in = lambda g: (g // tiles_in, g % tiles_in)
        in_specs, out_specs, out_shape, scratch = _inproj_specs(b, s, b_off, step_in)
        operands = [x, mod3, g_pre, p["w_in"], p["mavg"], p["lng"]]

    qkv_bytes = N_GROUPS * s_att * QKV_W * 2
    qkv_mode = pl.Buffered(1) if project and 2 * qkv_bytes > QKV_WINDOW_BUDGET else None
    biases = []
    for gi, dil in enumerate(DILATIONS):
        d, seq = qkvs[gi].shape[1:3]
        assert d == dil
        biases.append(jnp.asarray(_attn_bias(dil, min(2 * Q_BLOCK, seq))))
        in_specs.append(pl.BlockSpec((None, d, seq, QKV_W), lambda g: (g // steps_per_seq, 0, 0, 0),
                                     pipeline_mode=qkv_mode))
    in_specs += [pl.BlockSpec(bs.shape, lambda g: (0, 0, 0, 0)) for bs in biases]
    tile_of = lambda g: g % steps_per_seq
    tile_of.fused = True
    for dil in DILATIONS:
        spec, sds = _attn_out(dil, b_att, s_att, lambda g: g // steps_per_seq, tile_of)
        out_specs += [spec, spec]
        out_shape += [sds, sds]
    scratch = scratch + _attn_scratch(2 * Q_BLOCK)
    outs = pl.pallas_call(
        functools.partial(_front_kernel, seq_tokens=s_att, steps_per_seq=steps_per_seq,
                          project=project),
        grid=(n_steps,),
        in_specs=in_specs, out_specs=out_specs, out_shape=out_shape, scratch_shapes=scratch,
        compiler_params=_cparams(1),
        name="front" if project else "attn",
    )(*operands, *qkvs, *biases)
    if not project:
        return list(outs)
    return outs[:3], outs[3:N_INPROJ_OUT], outs[N_INPROJ_OUT:]


def _mix_tile(x, gt1, o_refs, l_refs, u_ref, vn_ref, wcat_ref, bsm_ref, wout_ref, g_ref, perm_ref):
    t = x.shape[0]

    def slabs(ref, dil, base):
        if _class_major(dil):
            for r in range(dil):
                for sb in range(N_SLAB):
                    perm_ref[base + sb, pl.ds(r, t // dil, stride=dil), :] = ref[sb, r]
            ref, lo = perm_ref, base
        else:
            lo = 0
        return jnp.concatenate([ref[lo + sb] for sb in range(N_SLAB)], axis=-1)

    lses = [slabs(r, dil, 0) for r, dil in zip(l_refs, DILATIONS)]
    m = jnp.maximum(jnp.maximum(lses[0], lses[1]), lses[2])
    es = [jnp.exp(l - m) for l in lses]
    inv = 1.0 / (es[0] + es[1] + es[2])
    parts = [(es[gi] * inv * slabs(r, dil, N_SLAB)).astype(BF16)
             for gi, (r, dil) in enumerate(zip(o_refs, DILATIONS))]

    lane_group = lax.broadcasted_iota(jnp.int32, (GMLP_CHUNK, GMLP_W), 1) // GMLP_GROUP
    n_gm = GMLP_W // GMLP_GROUP
    gms = []
    for c in range(t // GMLP_CHUNK):
        rows = slice(c * GMLP_CHUNK, (c + 1) * GMLP_CHUNK)
        vn = vn_ref[rows, :]
        stacked = jnp.concatenate(
            [jnp.where(lane_group == g, vn, jnp.zeros_like(vn)) for g in range(n_gm)], axis=0)
        sv = jnp.dot(wcat_ref[...], stacked, preferred_element_type=F32) + bsm_ref[...]
        gms.append((u_ref[rows, :].astype(F32) * sv).astype(BF16))
    parts.append(jnp.concatenate(gms, axis=0))

    mix = jnp.dot(jnp.concatenate(parts, axis=-1), wout_ref[...], preferred_element_type=F32)
    return x + _rms(mix) * (gt1 * g_ref[...])


def _ffn_tile(x, sh2, sc2, gt2, gpre_ref, wgu_ref, wdown_ref, gpost_ref):
    h = (_rms(x) * (gpre_ref[...] * (1.0 + sc2)) + sh2).astype(BF16)
    acc = None
    for c in range(D_FF // FF_CHUNK):
        gate = jnp.dot(h, wgu_ref[:, FF_CHUNK * c:FF_CHUNK * (c + 1)], preferred_element_type=F32)
        up = jnp.dot(h, wgu_ref[:, D_FF + FF_CHUNK * c:D_FF + FF_CHUNK * (c + 1)],
                     preferred_element_type=F32)
        act = (gate * jax.nn.sigmoid(gate) * up).astype(BF16)
        part = jnp.dot(act, wdown_ref[FF_CHUNK * c:FF_CHUNK * (c + 1), :],
                       preferred_element_type=F32)
        acc = part if acc is None else acc + part
    return x + _rms(acc) * (gt2 * gpost_ref[...])


N_CHUNK_IN = 2 * N_GROUPS + 2


def _post_kernel(x_ref, mod_ref, *refs, bounds):
    n = len(bounds) * N_CHUNK_IN
    wcat_ref, bsm_ref, wout_ref, gmix_ref, gpre_ref, wgu_ref, wdown_ref, gpost_ref = refs[n:n + 8]
    y_ref, perm_ref = refs[n + 8:]
    bi = pl.program_id(0)

    def run(chunk):
        o0, l0, o1, l1, o2, l2, u_ref, vn_ref = chunk
        x1 = _mix_tile(x_ref[...], mod_ref[2:3, :], (o0, o1, o2), (l0, l1, l2), u_ref, vn_ref,
                       wcat_ref, bsm_ref, wout_ref, gmix_ref, perm_ref)
        y_ref[...] = _ffn_tile(x1, mod_ref[3:4, :], mod_ref[4:5, :], mod_ref[5:6, :],
                               gpre_ref, wgu_ref, wdown_ref, gpost_ref)

    for k, (off, cnt) in enumerate(bounds):
        chunk = refs[k * N_CHUNK_IN:(k + 1) * N_CHUNK_IN]
        if len(bounds) == 1:
            run(chunk)
        else:
            pl.when((bi >= off) & (bi < off + cnt))(functools.partial(run, chunk))


def _post(x, mod3, chunks, p, gains):
    b, s, d = x.shape
    tile = TOKEN_TILE
    tok = lambda w: pl.BlockSpec((None, tile, w), lambda bi, i: (bi, i, 0))
    const = lambda a: pl.BlockSpec(a.shape, lambda bi, i: (0,) * a.ndim)
    in_specs = [tok(d), pl.BlockSpec((None, N_MOD, d), lambda bi, i: (bi, 0, 0))]
    operands = [x, mod3]
    bounds = []
    for off, att, u, vn in chunks:
        cnt = u.shape[0]
        bounds.append((off, cnt))
        local = lambda bi, off=off, cnt=cnt: jnp.clip(bi - off, 0, cnt - 1)
        for dil in DILATIONS:
            if _class_major(dil):
                spec = pl.BlockSpec((None, N_SLAB, dil, tile // dil, LANES),
                                    lambda bi, i, local=local: (local(bi), 0, 0, i, 0))
            else:
                spec = pl.BlockSpec((None, N_SLAB, tile, LANES),
                                    lambda bi, i, local=local: (local(bi), 0, i, 0))
            in_specs += [spec, spec]
        in_specs += [pl.BlockSpec((None, tile, GMLP_W),
                                  lambda bi, i, local=local: (local(bi), i, 0))] * 2
        operands += [*att, u, vn]
    g_post_mix, g_pre_ffn, g_post_ffn = gains
    consts = (p["wcat"], p["bsm"], p["w_out"], g_post_mix, g_pre_ffn, p["w_gu"], p["w_down"],
              g_post_ffn)
    return pl.pallas_call(
        functools.partial(_post_kernel, bounds=tuple(bounds)),
        grid=(b, s // tile),
        in_specs=in_specs + [const(a) for a in consts],
        out_specs=tok(d),
        out_shape=jax.ShapeDtypeStruct((b, s, d), F32),
        scratch_shapes=[pltpu.VMEM((2 * N_SLAB, tile, LANES), F32)],
        compiler_params=_cparams(2),
        name="post",
    )(*operands, *consts)


def _prep_layer(w_in, w_s, b_s, g_gmlp, w_out, w_gu, w_down):
    col_scale = np.ones((1, IN_W), np.float32)
    col_scale[:, :ATTN_W] = HEAD_DIM ** -0.5
    n_gm = GMLP_W // GMLP_GROUP
    wcat = jnp.transpose(w_s, (1, 0, 2)).reshape(GMLP_CHUNK, n_gm * GMLP_CHUNK).astype(BF16)
    bsm = jnp.repeat(b_s.T, GMLP_GROUP, axis=1).astype(F32)
    grp = np.arange(GMLP_W) // GMLP_GROUP
    mavg = jnp.asarray((grp[:, None] == grp[None, :]).astype(np.float32) / GMLP_GROUP, BF16)
    return dict(w_in=(w_in * col_scale).astype(BF16), wcat=wcat, bsm=bsm, mavg=mavg,
                lng=g_gmlp.reshape(1, GMLP_W), w_out=w_out.astype(BF16),
                w_gu=w_gu.astype(BF16), w_down=w_down.astype(BF16))


def _as_classes(q):
    q0, q1, q2 = q
    return q0.reshape(q0.shape[0], 1, q0.shape[1], q0.shape[2]), q1, q2


def _attn_all(qkvs):
    s_att = qkvs[0].shape[1] * qkvs[0].shape[2]
    if s_att % ATTN_TILE == 0:
        return _front(None, None, None, None, b_off=0, b=0, qkvs=qkvs)
    outs = []
    for qkv, dil in zip(qkvs, DILATIONS):
        outs += list(_attn(qkv, dil))
    return outs


def _layer(x_p, x_s, mod_p, mod_s, p, g_pre_mix, g_post_mix, g_pre_ffn, g_post_ffn):
    row = lambda g: g.reshape(1, D_MODEL)
    g_pre = row(g_pre_mix)
    gains = (row(g_post_mix), row(g_pre_ffn), row(g_post_ffn))
    groups = ((x_s, mod_s), (x_p, mod_p))

    def chunked(x):
        b, s = x.shape[:2]
        return (CHUNK_TOKENS % s == 0 and b % (CHUNK_TOKENS // s) == 0 and s % ATTN_TILE == 0)

    fuse = all(chunked(g[0]) for g in groups)
    chunks = []
    for gid, (x, _) in enumerate(groups):
        per = CHUNK_TOKENS // x.shape[1] if fuse else x.shape[0]
        chunks += [(gid, off, per) for off in range(0, x.shape[0], per)]

    done = [[], []]
    pending = None
    for gid, off, per in chunks:
        x, mod = groups[gid]
        if pending is not None and fuse:
            qk, pg, poff, pu, pvn = pending
            q, (u, vn), att = _front(x, mod, g_pre, p, b_off=off, b=per, qkvs=qk)
            done[pg].append((poff, att, pu, pvn))
        else:
            if pending is not None:
                qk, pg, poff, pu, pvn = pending
                done[pg].append((poff, _attn_all(qk), pu, pvn))
            q, (u, vn) = _inproj(x, mod, g_pre, p, b_off=off, b=per)
        pending = (_as_classes(q), gid, off, u, vn)
    qk, pg, poff, pu, pvn = pending
    done[pg].append((poff, _attn_all(qk), pu, pvn))

    y_s, y_p = (_post(x, mod, done[gid], p, gains) for gid, (x, mod) in enumerate(groups))
    return y_p, y_s


def kernel(x_prompt, x_sample, c_prompt, c_sample, w_ada, b_ada, g_pre_mix, w_in, w_s, b_s, g_gmlp, w_out, g_post_mix, g_pre_ffn, w_gu, w_down, g_post_ffn):
    n_p = c_prompt.shape[0]
    c_all = jnp.concatenate([c_prompt, c_sample], axis=0)
    y_prompt, y_sample = x_prompt, x_sample
    for l in range(w_ada.shape[0]):
        mod3 = _adaln(c_all, w_ada[l], b_ada[l]).reshape(c_all.shape[0], N_MOD, D_MODEL)
        p = _prep_layer(w_in[l], w_s[l], b_s[l], g_gmlp[l], w_out[l], w_gu[l], w_down[l])
        y_prompt, y_sample = _layer(y_prompt, y_sample, mod3[:n_p], mod3[n_p:], p,
                                    g_pre_mix[l], g_post_mix[l], g_pre_ffn[l], g_post_ffn[l])
    return (y_prompt, y_sample)
```

```python
import functools

import numpy as np
import jax
import jax.numpy as jnp
from jax import lax
from jax.experimental import pallas as pl
from jax.experimental.pallas import tpu as pltpu

F32 = jnp.float32
BF16 = jnp.bfloat16

D_MODEL = 1024
HEAD_DIM = 64
HEADS_PER_GROUP = 4
GROUP_W = HEADS_PER_GROUP * HEAD_DIM
DILATIONS = (1, 4, 16)
N_SIDE = 64
N_GROUPS = len(DILATIONS)
ATTN_W = N_GROUPS * GROUP_W
QKV_W = 3 * GROUP_W
GMLP_W = 256
GMLP_GROUP = 64
GMLP_CHUNK = 128
IN_W = 3 * ATTN_W + 2 * GMLP_W
D_FF = 2816
N_MOD = 6
RMS_EPS = 1e-6
LN_EPS = 1e-5
NEG_INF = -1e30

LANES = 128
N_SLAB = GROUP_W // LANES
Q_BLOCK = 128
ATTN_TILE = Q_BLOCK * DILATIONS[-1]
TOKEN_TILE = 512
CHUNK_TOKENS = 16384
FRONT_BLOCKS = TOKEN_TILE // Q_BLOCK
FF_CHUNK = 256
VMEM_LIMIT = 60 * 1024 * 1024
QKV_WINDOW_BUDGET = 20 * 1024 * 1024


def _cparams(n_axes):
    return pltpu.CompilerParams(
        dimension_semantics=("arbitrary",) * n_axes, vmem_limit_bytes=VMEM_LIMIT)


def _adaln_kernel(c_ref, w_ref, b_ref, o_ref):
    c = c_ref[...]
    a = (c * jax.nn.sigmoid(c)).astype(BF16)
    o_ref[...] = jnp.dot(a, w_ref[...].astype(BF16), preferred_element_type=F32) + b_ref[...]


def _adaln(c, w_ada, b_ada):
    n, d = c.shape
    nout = w_ada.shape[1]
    tn = 1024
    return pl.pallas_call(
        _adaln_kernel,
        grid=(nout // tn,),
        in_specs=[
            pl.BlockSpec((n, d), lambda j: (0, 0)),
            pl.BlockSpec((d, tn), lambda j: (0, j)),
            pl.BlockSpec((1, tn), lambda j: (0, j)),
        ],
        out_specs=pl.BlockSpec((n, tn), lambda j: (0, j)),
        out_shape=jax.ShapeDtypeStruct((n, nout), F32),
        compiler_params=_cparams(1),
        name="adaln",
    )(c, w_ada, b_ada.reshape(1, nout))


def _rms(x):
    return x * lax.rsqrt(jnp.mean(x * x, axis=-1, keepdims=True) + RMS_EPS)


def _inproj_phases(x_ref, mod_ref, g_ref, w_ref, mavg_ref, lng_ref,
                   q0_ref, q1_ref, q2_ref, u_ref, vn_ref, slab1_ref, slab2_ref):
    t = x_ref.shape[0]
    state = {}

    def norm():
        sh1 = mod_ref[0:1, :]
        sc1 = mod_ref[1:2, :]
        state["h"] = (_rms(x_ref[...]) * (g_ref[...] * (1.0 + sc1)) + sh1).astype(BF16)

    def qkv(gi):
        h = state["h"]
        return jnp.concatenate(
            [jnp.dot(h, w_ref[:, part * ATTN_W + gi * GROUP_W:part * ATTN_W + (gi + 1) * GROUP_W],
                     preferred_element_type=F32) for part in range(3)], axis=-1)

    def group0():
        q0_ref[...] = qkv(0).astype(BF16)

    def project(gi, slab_ref):
        def run():
            p = qkv(gi)
            for j in range(QKV_W // LANES):
                slab_ref[j] = p[:, j * LANES:(j + 1) * LANES]
        return run

    def permute(gi, slab_ref, out_ref):
        def run():
            d = DILATIONS[gi]
            for r in range(d):
                for j in range(QKV_W // LANES):
                    out_ref[r, :, j * LANES:(j + 1) * LANES] = (
                        slab_ref[j, pl.ds(r, t // d, stride=d), :].astype(BF16))
        return run

    def gmlp_dots():
        h = state["h"]
        base = 3 * ATTN_W
        state["gu"] = jnp.dot(h, w_ref[:, base:base + GMLP_W], preferred_element_type=F32)
        state["gv"] = jnp.dot(h, w_ref[:, base + GMLP_W:base + 2 * GMLP_W],
                              preferred_element_type=F32)

    def gmlp_tail():
        u_ref[...] = jax.nn.gelu(state["gu"]).astype(BF16)
        v = jax.nn.gelu(state["gv"])
        mu = jnp.dot(v.astype(BF16), mavg_ref[...], preferred_element_type=F32)
        vc = v - mu
        var = jnp.dot((vc * vc).astype(BF16), mavg_ref[...], preferred_element_type=F32)
        vn_ref[...] = (vc * lax.rsqrt(var + LN_EPS) * lng_ref[...]).astype(BF16)

    return dict(norm=norm, group0=group0, project1=project(1, slab1_ref),
                permute1=permute(1, slab1_ref, q1_ref), project2=project(2, slab2_ref),
                permute2=permute(2, slab2_ref, q2_ref), gmlp_dots=gmlp_dots, gmlp_tail=gmlp_tail)


FRONT_PLAN = (("norm", 0), ("gmlp_dots", 2), ("project2", 3), ("gmlp_tail", 1), ("group0", 3),
              ("permute2", 0), ("project1", 3), ("permute1", 0))


N_INPROJ_IN = 6
N_INPROJ_OUT = 5


def _inproj_kernel(*refs):
    phases = _inproj_phases(*refs)
    for name, _ in FRONT_PLAN:
        phases[name]()


def _inproj_specs(b, s, b_off, step_of):
    d = D_MODEL
    d1, d2 = DILATIONS[1], DILATIONS[2]
    tile = TOKEN_TILE

    def tok(w, off):
        return pl.BlockSpec((None, tile, w), lambda *g: (step_of(*g)[0] + off, step_of(*g)[1], 0))

    def cls(dil):
        return pl.BlockSpec((None, dil, tile // dil, QKV_W),
                            lambda *g: (step_of(*g)[0], 0, step_of(*g)[1], 0))

    const = lambda shape: pl.BlockSpec(shape, lambda *g: (0,) * len(shape))
    in_specs = [
        tok(d, b_off),
        pl.BlockSpec((None, N_MOD, d), lambda *g: (step_of(*g)[0] + b_off, 0, 0)),
        const((1, d)), const((d, IN_W)), const((GMLP_W, GMLP_W)), const((1, GMLP_W)),
    ]
    out_specs = [tok(QKV_W, 0), cls(d1), cls(d2), tok(GMLP_W, 0), tok(GMLP_W, 0)]
    out_shape = [
        jax.ShapeDtypeStruct((b, s, QKV_W), BF16),
        jax.ShapeDtypeStruct((b, d1, s // d1, QKV_W), BF16),
        jax.ShapeDtypeStruct((b, d2, s // d2, QKV_W), BF16),
        jax.ShapeDtypeStruct((b, s, GMLP_W), BF16),
        jax.ShapeDtypeStruct((b, s, GMLP_W), BF16),
    ]
    scratch = [pltpu.VMEM((QKV_W // LANES, tile, LANES), F32)] * 2
    return in_specs, out_specs, out_shape, scratch


def _inproj(x, mod3, g_pre, p, *, b_off, b):
    s = x.shape[1]
    in_specs, out_specs, out_shape, scratch = _inproj_specs(b, s, b_off, lambda bi, i: (bi, i))
    outs = pl.pallas_call(
        _inproj_kernel,
        grid=(b, s // TOKEN_TILE),
        in_specs=in_specs, out_specs=out_specs, out_shape=out_shape, scratch_shapes=scratch,
        compiler_params=_cparams(2),
        name="inproj",
    )(x, mod3, g_pre, p["w_in"], p["mavg"], p["lng"])
    return outs[:3], outs[3:]


def _attn_bias(dil, kw):
    group = DILATIONS.index(dil)
    n_heads = N_GROUPS * HEADS_PER_GROUP
    slopes = 2.0 ** (-8.0 * np.arange(1, n_heads + 1, dtype=np.float32) / n_heads)
    slopes = slopes[group * HEADS_PER_GROUP:(group + 1) * HEADS_PER_GROUP].astype(np.float32)
    n_var = 3 if kw == 2 * Q_BLOCK else 1
    i = np.arange(Q_BLOCK)[:, None]
    c = np.arange(kw)[None, :]
    out = np.empty((n_var, HEADS_PER_GROUP, Q_BLOCK, kw), np.float32)
    for v in range(n_var):
        rel = np.abs(c - N_SIDE * v - i)
        dist = (dil * rel).astype(np.float32)
        for h in range(HEADS_PER_GROUP):
            out[v, h] = np.where(rel <= N_SIDE, -slopes[h] * dist, np.float32(NEG_INF))
    return out


def _class_major(dil):
    return dil > FRONT_BLOCKS


def _attn_stages(qkv_ref, bias_ref, o_ref, l_ref, s_bufs, m_bufs, *, dil, seq, kw, coords):
    heads_per_tile = LANES // HEAD_DIM
    half = lax.broadcasted_iota(jnp.int32, (Q_BLOCK, LANES), 1) // HEAD_DIM

    def window(qs):
        return pl.multiple_of(jnp.clip(qs - N_SIDE, 0, seq - kw), N_SIDE)

    def scores(n, slot):
        r, qs, _ = coords(n)
        qs = pl.multiple_of(qs, Q_BLOCK)
        ks = window(qs)
        s_buf, m_buf = s_bufs[slot], m_bufs[slot]
        var = (qs - ks) // N_SIDE
        for h in range(HEADS_PER_GROUP):
            tile = slice((h // heads_per_tile) * LANES, (h // heads_per_tile + 1) * LANES)
            q = qkv_ref[r, pl.ds(qs, Q_BLOCK), tile]
            q = jnp.where(half == h % heads_per_tile, q, jnp.zeros_like(q))
            k = qkv_ref[r, pl.ds(ks, kw), GROUP_W + tile.start:GROUP_W + tile.stop]
            s = lax.dot_general(q, k, (((1,), (1,)), ((), ())), preferred_element_type=F32)
            bias = bias_ref[var, h]
            s = jnp.where(bias > 0.5 * NEG_INF, s + bias, NEG_INF)
            s_buf[h, :, 0:kw] = s
            m_buf[h] = jnp.broadcast_to(jnp.max(s, axis=-1, keepdims=True), (Q_BLOCK, LANES))

    def output(n, slot):
        r, qs, row0 = coords(n)
        ks = window(pl.multiple_of(qs, Q_BLOCK))
        s_buf, m_buf = s_bufs[slot], m_bufs[slot]
        if _class_major(dil):
            dst = lambda tl: (tl, row0)
        else:
            rows = pl.ds(row0, Q_BLOCK) if dil == 1 else pl.ds(row0, Q_BLOCK, stride=dil)
            dst = lambda tl: (tl, rows, slice(None))
        for tl in range(N_SLAB):
            o_pair, lse_pair = [], []
            for hp in range(heads_per_tile):
                h = tl * heads_per_tile + hp
                m = m_buf[h]
                p = [jnp.exp(s_buf[h, :, c * LANES:(c + 1) * LANES] - m)
                     for c in range(kw // LANES)]
                psum = p[0] if len(p) == 1 else p[0] + p[1]
                l = jnp.sum(psum, axis=-1, keepdims=True)
                pb = jnp.concatenate(p, axis=-1).astype(BF16)
                v = qkv_ref[r, pl.ds(ks, kw), 2 * GROUP_W + tl * LANES:2 * GROUP_W + (tl + 1) * LANES]
                o_pair.append(jnp.dot(pb, v, preferred_element_type=F32) * (1.0 / l))
                lse_pair.append(m + jnp.log(l))
            o_ref[dst(tl)] = jnp.where(half == 0, o_pair[0], o_pair[1])
            l_ref[dst(tl)] = jnp.where(half == 0, lse_pair[0], lse_pair[1])

    return scores, output


def _attn_scratch(kw):
    return [
        pltpu.VMEM((HEADS_PER_GROUP, Q_BLOCK, kw), F32),
        pltpu.VMEM((HEADS_PER_GROUP, Q_BLOCK, kw), F32),
        pltpu.VMEM((HEADS_PER_GROUP, Q_BLOCK, LANES), F32),
        pltpu.VMEM((HEADS_PER_GROUP, Q_BLOCK, LANES), F32),
    ]


def _attn_out(dil, b, s, batch_of, tile_of):
    if _class_major(dil):
        per_row = dil // FRONT_BLOCKS
        spec = pl.BlockSpec(
            (None, N_SLAB, FRONT_BLOCKS, Q_BLOCK, LANES),
            lambda g: (batch_of(g), 0, tile_of(g) % per_row, tile_of(g) // per_row, 0))
        return spec, jax.ShapeDtypeStruct((b, N_SLAB, dil, s // dil, LANES), F32)
    spec = pl.BlockSpec((None, N_SLAB, TOKEN_TILE, LANES),
                        lambda g: (batch_of(g), 0, tile_of(g), 0))
    return spec, jax.ShapeDtypeStruct((b, N_SLAB, s, LANES), F32)


def _front_kernel(*refs, seq_tokens, steps_per_seq, project):
    n_proj_in, n_proj_out = (N_INPROJ_IN, N_INPROJ_OUT) if project else (0, 0)
    n_in = n_proj_in + 2 * N_GROUPS
    n_out = n_proj_out + 2 * N_GROUPS
    ins, outs, scr = refs[:n_in], refs[n_in:n_in + n_out], refs[n_in + n_out:]
    qkv_refs = ins[n_proj_in:n_proj_in + N_GROUPS]
    bias_refs = ins[n_proj_in + N_GROUPS:]
    attn_outs = outs[n_proj_out:]
    slab_refs, s_bufs, m_bufs = scr[:-4], scr[-4:-2], scr[-2:]
    w = pl.program_id(0) % steps_per_seq

    units = []
    for gi, dil in enumerate(DILATIONS):
        seq = seq_tokens // dil
        kw = min(2 * Q_BLOCK, seq)

        def coords(n, dil=dil):
            m = w * FRONT_BLOCKS + n
            r, j = m % dil, m // dil
            if _class_major(dil):
                row0 = n
            else:
                row0 = (n // dil) * Q_BLOCK * dil + n % dil
            return r, j * Q_BLOCK, row0

        scores, output = _attn_stages(
            qkv_refs[gi], bias_refs[gi], attn_outs[2 * gi], attn_outs[2 * gi + 1],
            s_bufs, m_bufs, dil=dil, seq=seq, kw=kw, coords=coords)
        units += [(scores, output, n) for n in range(FRONT_BLOCKS)]

    n_units = len(units)
    if project:
        phases = _inproj_phases(*ins[:N_INPROJ_IN], *outs[:N_INPROJ_OUT], *slab_refs)
        plan = FRONT_PLAN
    else:
        phases = {"attention": lambda: None}
        plan = (("attention", n_units),)
    assert sum(cnt for _, cnt in plan) == n_units
    units[0][0](units[0][2], 0)
    k = 0
    for name, cnt in plan:
        phases[name]()
        for _ in range(cnt):
            if k + 1 < n_units:
                units[k + 1][0](units[k + 1][2], (k + 1) % 2)
            units[k][1](units[k][2], k % 2)
            k += 1


def _front(x, mod3, g_pre, p, *, b_off, b, qkvs):
    b_att = qkvs[0].shape[0]
    s_att = qkvs[0].shape[1] * qkvs[0].shape[2]
    steps_per_seq = s_att // TOKEN_TILE
    n_steps = b_att * steps_per_seq
    assert s_att % ATTN_TILE == 0
    project = x is not None
    in_specs, out_specs, out_shape, scratch, operands = [], [], [], [], []
    if project:
        s = x.shape[1]
        tiles_in = s // TOKEN_TILE
        assert n_steps == b * tiles_in
        step_in = lambda g: (g // tiles_in, g % tiles_in)
        in_specs, out_specs, out_shape, scratch = _inproj_specs(b, s, b_off, step_in)
        operands = [x, mod3, g_pre, p["w_in"], p["mavg"], p["lng"]]

    qkv_bytes = N_GROUPS * s_att * QKV_W * 2
    qkv_mode = pl.Buffered(1) if project and 2 * qkv_bytes > QKV_WINDOW_BUDGET else None
    biases = []
    for gi, dil in enumerate(DILATIONS):
        d, seq = qkvs[gi].shape[1:3]
        assert d == dil
        biases.append(jnp.asarray(_attn_bias(dil, min(2 * Q_BLOCK, seq))))
        in_specs.append(pl.BlockSpec((None, d, seq, QKV_W), lambda g: (g // steps_per_seq, 0, 0, 0),
                                     pipeline_mode=qkv_mode))
    in_specs += [pl.BlockSpec(bs.shape, lambda g: (0, 0, 0, 0)) for bs in biases]
    for dil in DILATIONS:
        spec, sds = _attn_out(dil, b_att, s_att, lambda g: g // steps_per_seq,
                              lambda g: g % steps_per_seq)
        out_specs += [spec, spec]
        out_shape += [sds, sds]
    scratch = scratch + _attn_scratch(2 * Q_BLOCK)
    outs = pl.pallas_call(
        functools.partial(_front_kernel, seq_tokens=s_att, steps_per_seq=steps_per_seq,
                          project=project),
        grid=(n_steps,),
        in_specs=in_specs, out_specs=out_specs, out_shape=out_shape, scratch_shapes=scratch,
        compiler_params=_cparams(1),
        name="front" if project else "attn",
    )(*operands, *qkvs, *biases)
    if not project:
        return list(outs)
    return outs[:3], outs[3:N_INPROJ_OUT], outs[N_INPROJ_OUT:]


def _mix_tile(x, gt1, o_refs, l_refs, u_ref, vn_ref, wcat_ref, bsm_ref, wout_ref, g_ref, perm_ref):
    t = x.shape[0]

    def slabs(ref, dil, base):
        if _class_major(dil):
            for r in range(dil):
                for sb in range(N_SLAB):
                    perm_ref[base + sb, pl.ds(r, t // dil, stride=dil), :] = ref[sb, r]
            ref, lo = perm_ref, base
        else:
            lo = 0
        return jnp.concatenate([ref[lo + sb] for sb in range(N_SLAB)], axis=-1)

    lses = [slabs(r, dil, 0) for r, dil in zip(l_refs, DILATIONS)]
    m = jnp.maximum(jnp.maximum(lses[0], lses[1]), lses[2])
    es = [jnp.exp(l - m) for l in lses]
    inv = 1.0 / (es[0] + es[1] + es[2])
    parts = [(es[gi] * inv * slabs(r, dil, N_SLAB)).astype(BF16)
             for gi, (r, dil) in enumerate(zip(o_refs, DILATIONS))]

    lane_group = lax.broadcasted_iota(jnp.int32, (GMLP_CHUNK, GMLP_W), 1) // GMLP_GROUP
    n_gm = GMLP_W // GMLP_GROUP
    gms = []
    for c in range(t // GMLP_CHUNK):
        rows = slice(c * GMLP_CHUNK, (c + 1) * GMLP_CHUNK)
        vn = vn_ref[rows, :]
        stacked = jnp.concatenate(
            [jnp.where(lane_group == g, vn, jnp.zeros_like(vn)) for g in range(n_gm)], axis=0)
        sv = jnp.dot(wcat_ref[...], stacked, preferred_element_type=F32) + bsm_ref[...]
        gms.append((u_ref[rows, :].astype(F32) * sv).astype(BF16))
    parts.append(jnp.concatenate(gms, axis=0))

    mix = jnp.dot(jnp.concatenate(parts, axis=-1), wout_ref[...], preferred_element_type=F32)
    return x + _rms(mix) * (gt1 * g_ref[...])


def _ffn_tile(x, sh2, sc2, gt2, gpre_ref, wgu_ref, wdown_ref, gpost_ref):
    h = (_rms(x) * (gpre_ref[...] * (1.0 + sc2)) + sh2).astype(BF16)
    acc = None
    for c in range(D_FF // FF_CHUNK):
        gate = jnp.dot(h, wgu_ref[:, FF_CHUNK * c:FF_CHUNK * (c + 1)], preferred_element_type=F32)
        up = jnp.dot(h, wgu_ref[:, D_FF + FF_CHUNK * c:D_FF + FF_CHUNK * (c + 1)],
                     preferred_element_type=F32)
        act = (gate * jax.nn.sigmoid(gate) * up).astype(BF16)
        part = jnp.dot(act, wdown_ref[FF_CHUNK * c:FF_CHUNK * (c + 1), :],
                       preferred_element_type=F32)
        acc = part if acc is None else acc + part
    return x + _rms(acc) * (gt2 * gpost_ref[...])


N_CHUNK_IN = 2 * N_GROUPS + 2


def _post_kernel(x_ref, mod_ref, *refs, bounds):
    n = len(bounds) * N_CHUNK_IN
    wcat_ref, bsm_ref, wout_ref, gmix_ref, gpre_ref, wgu_ref, wdown_ref, gpost_ref = refs[n:n + 8]
    y_ref, perm_ref = refs[n + 8:]
    bi = pl.program_id(0)

    def run(chunk):
        o0, l0, o1, l1, o2, l2, u_ref, vn_ref = chunk
        x1 = _mix_tile(x_ref[...], mod_ref[2:3, :], (o0, o1, o2), (l0, l1, l2), u_ref, vn_ref,
                       wcat_ref, bsm_ref, wout_ref, gmix_ref, perm_ref)
        y_ref[...] = _ffn_tile(x1, mod_ref[3:4, :], mod_ref[4:5, :], mod_ref[5:6, :],
                               gpre_ref, wgu_ref, wdown_ref, gpost_ref)

    for k, (off, cnt) in enumerate(bounds):
        chunk = refs[k * N_CHUNK_IN:(k + 1) * N_CHUNK_IN]
        if len(bounds) == 1:
            run(chunk)
        else:
            pl.when((bi >= off) & (bi < off + cnt))(functools.partial(run, chunk))


def _post(x, mod3, chunks, p, gains):
    b, s, d = x.shape
    tile = TOKEN_TILE
    tok = lambda w: pl.BlockSpec((None, tile, w), lambda bi, i: (bi, i, 0))
    const = lambda a: pl.BlockSpec(a.shape, lambda bi, i: (0,) * a.ndim)
    in_specs = [tok(d), pl.BlockSpec((None, N_MOD, d), lambda bi, i: (bi, 0, 0))]
    operands = [x, mod3]
    bounds = []
    for off, att, u, vn in chunks:
        cnt = u.shape[0]
        bounds.append((off, cnt))
        local = lambda bi, off=off, cnt=cnt: jnp.clip(bi - off, 0, cnt - 1)
        for dil in DILATIONS:
            if _class_major(dil):
                spec = pl.BlockSpec((None, N_SLAB, dil, tile // dil, LANES),
                                    lambda bi, i, local=local: (local(bi), 0, 0, i, 0))
            else:
                spec = pl.BlockSpec((None, N_SLAB, tile, LANES),
                                    lambda bi, i, local=local: (local(bi), 0, i, 0))
            in_specs += [spec, spec]
        in_specs += [pl.BlockSpec((None, tile, GMLP_W),
                                  lambda bi, i, local=local: (local(bi), i, 0))] * 2
        operands += [*att, u, vn]
    g_post_mix, g_pre_ffn, g_post_ffn = gains
    consts = (p["wcat"], p["bsm"], p["w_out"], g_post_mix, g_pre_ffn, p["w_gu"], p["w_down"],
              g_post_ffn)
    return pl.pallas_call(
        functools.partial(_post_kernel, bounds=tuple(bounds)),
        grid=(b, s // tile),
        in_specs=in_specs + [const(a) for a in consts],
        out_specs=tok(d),
        out_shape=jax.ShapeDtypeStruct((b, s, d), F32),
        scratch_shapes=[pltpu.VMEM((2 * N_SLAB, tile, LANES), F32)],
        compiler_params=_cparams(2),
        name="post",
    )(*operands, *consts)


def _prep_layer(w_in, w_s, b_s, g_gmlp, w_out, w_gu, w_down):
    col_scale = np.ones((1, IN_W), np.float32)
    col_scale[:, :ATTN_W] = HEAD_DIM ** -0.5
    n_gm = GMLP_W // GMLP_GROUP
    wcat = jnp.transpose(w_s, (1, 0, 2)).reshape(GMLP_CHUNK, n_gm * GMLP_CHUNK).astype(BF16)
    bsm = jnp.repeat(b_s.T, GMLP_GROUP, axis=1).astype(F32)
    grp = np.arange(GMLP_W) // GMLP_GROUP
    mavg = jnp.asarray((grp[:, None] == grp[None, :]).astype(np.float32) / GMLP_GROUP, BF16)
    return dict(w_in=(w_in * col_scale).astype(BF16), wcat=wcat, bsm=bsm, mavg=mavg,
                lng=g_gmlp.reshape(1, GMLP_W), w_out=w_out.astype(BF16),
                w_gu=w_gu.astype(BF16), w_down=w_down.astype(BF16))


def _as_classes(q):
    q0, q1, q2 = q
    return q0.reshape(q0.shape[0], 1, q0.shape[1], q0.shape[2]), q1, q2


def _attn_all(qkvs):
    return _front(None, None, None, None, b_off=0, b=0, qkvs=qkvs)


def _layer(x_p, x_s, mod_p, mod_s, p, g_pre_mix, g_post_mix, g_pre_ffn, g_post_ffn):
    row = lambda g: g.reshape(1, D_MODEL)
    g_pre = row(g_pre_mix)
    gains = (row(g_post_mix), row(g_pre_ffn), row(g_post_ffn))
    groups = ((x_s, mod_s), (x_p, mod_p))

    def chunked(x):
        b, s = x.shape[:2]
        return (CHUNK_TOKENS % s == 0 and b % (CHUNK_TOKENS // s) == 0 and s % ATTN_TILE == 0)

    fuse = all(chunked(g[0]) for g in groups)
    chunks = []
    for gid, (x, _) in enumerate(groups):
        per = CHUNK_TOKENS // x.shape[1] if fuse else x.shape[0]
        chunks += [(gid, off, per) for off in range(0, x.shape[0], per)]

    done = [[], []]
    pending = None
    for gid, off, per in chunks:
        x, mod = groups[gid]
        if pending is not None and fuse:
            qk, pg, poff, pu, pvn = pending
            q, (u, vn), att = _front(x, mod, g_pre, p, b_off=off, b=per, qkvs=qk)
            done[pg].append((poff, att, pu, pvn))
        else:
            if pending is not None:
                qk, pg, poff, pu, pvn = pending
                done[pg].append((poff, _attn_all(qk), pu, pvn))
            q, (u, vn) = _inproj(x, mod, g_pre, p, b_off=off, b=per)
        pending = (_as_classes(q), gid, off, u, vn)
    qk, pg, poff, pu, pvn = pending
    done[pg].append((poff, _attn_all(qk), pu, pvn))

    y_s, y_p = (_post(x, mod, done[gid], p, gains) for gid, (x, mod) in enumerate(groups))
    return y_p, y_s


def kernel(x_prompt, x_sample, c_prompt, c_sample, w_ada, b_ada, g_pre_mix, w_in, w_s, b_s, g_gmlp, w_out, g_post_mix, g_pre_ffn, w_gu, w_down, g_post_ffn):
    n_p = c_prompt.shape[0]
    c_all = jnp.concatenate([c_prompt, c_sample], axis=0)
    y_prompt, y_sample = x_prompt, x_sample
    for l in range(w_ada.shape[0]):
        mod3 = _adaln(c_all, w_ada[l], b_ada[l]).reshape(c_all.shape[0], N_MOD, D_MODEL)
        p = _prep_layer(w_in[l], w_s[l], b_s[l], g_gmlp[l], w_out[l], w_gu[l], w_down[l])
        y_prompt, y_sample = _layer(y_prompt, y_sample, mod3[:n_p], mod3[n_p:], p,
                                    g_pre_mix[l], g_post_mix[l], g_pre_ffn[l], g_post_ffn[l])
    return (y_prompt, y_sample)
```

```python
import functools

import numpy as np
import jax
import jax.numpy as jnp
from jax import lax
from jax.experimental import pallas as pl
from jax.experimental.pallas import tpu as pltpu

F32 = jnp.float32
BF16 = jnp.bfloat16

D_MODEL = 1024
HEAD_DIM = 64
HEADS_PER_GROUP = 4
GROUP_W = HEADS_PER_GROUP * HEAD_DIM
DILATIONS = (1, 4, 16)
N_SIDE = 64
N_GROUPS = len(DILATIONS)
ATTN_W = N_GROUPS * GROUP_W
QKV_W = 3 * GROUP_W
GMLP_W = 256
GMLP_GROUP = 64
GMLP_CHUNK = 128
IN_W = 3 * ATTN_W + 2 * GMLP_W
D_FF = 2816
N_MOD = 6
RMS_EPS = 1e-6
LN_EPS = 1e-5
NEG_INF = -1e30

LANES = 128
N_SLAB = GROUP_W // LANES
Q_BLOCK = 128
ATTN_TILE = Q_BLOCK * DILATIONS[-1]
TOKEN_TILE = 512
CHUNK_TOKENS = 16384
FRONT_BLOCKS = TOKEN_TILE // Q_BLOCK
FF_CHUNK = 256
VMEM_LIMIT = 60 * 1024 * 1024
QKV_WINDOW_BUDGET = 20 * 1024 * 1024


def _cparams(n_axes):
    return pltpu.CompilerParams(
        dimension_semantics=("arbitrary",) * n_axes, vmem_limit_bytes=VMEM_LIMIT)


def _adaln_kernel(c_ref, w_ref, b_ref, o_ref):
    c = c_ref[...]
    a = (c * jax.nn.sigmoid(c)).astype(BF16)
    o_ref[...] = jnp.dot(a, w_ref[...].astype(BF16), preferred_element_type=F32) + b_ref[...]


def _adaln(c, w_ada, b_ada):
    n, d = c.shape
    nout = w_ada.shape[1]
    tn = 1024
    return pl.pallas_call(
        _adaln_kernel,
        grid=(nout // tn,),
        in_specs=[
            pl.BlockSpec((n, d), lambda j: (0, 0)),
            pl.BlockSpec((d, tn), lambda j: (0, j)),
            pl.BlockSpec((1, tn), lambda j: (0, j)),
        ],
        out_specs=pl.BlockSpec((n, tn), lambda j: (0, j)),
        out_shape=jax.ShapeDtypeStruct((n, nout), F32),
        compiler_params=_cparams(1),
        name="adaln",
    )(c, w_ada, b_ada.reshape(1, nout))


def _rms(x):
    return x * lax.rsqrt(jnp.mean(x * x, axis=-1, keepdims=True) + RMS_EPS)


def _inproj_phases(x_ref, mod_ref, g_ref, w_ref, mavg_ref, lng_ref,
                   q0_ref, q1_ref, q2_ref, u_ref, vn_ref, slab1_ref, slab2_ref):
    t = x_ref.shape[0]
    state = {}

    def norm():
        sh1 = mod_ref[0:1, :]
        sc1 = mod_ref[1:2, :]
        state["h"] = (_rms(x_ref[...]) * (g_ref[...] * (1.0 + sc1)) + sh1).astype(BF16)

    def qkv(gi):
        h = state["h"]
        return jnp.concatenate(
            [jnp.dot(h, w_ref[:, part * ATTN_W + gi * GROUP_W:part * ATTN_W + (gi + 1) * GROUP_W],
                     preferred_element_type=F32) for part in range(3)], axis=-1)

    def group0():
        q0_ref[...] = qkv(0).astype(BF16)

    def project(gi, slab_ref):
        def run():
            p = qkv(gi)
            for j in range(QKV_W // LANES):
                slab_ref[j] = p[:, j * LANES:(j + 1) * LANES]
        return run

    def permute(gi, slab_ref, out_ref):
        def run():
            d = DILATIONS[gi]
            for r in range(d):
                for j in range(QKV_W // LANES):
                    out_ref[r, :, j * LANES:(j + 1) * LANES] = (
                        slab_ref[j, pl.ds(r, t // d, stride=d), :].astype(BF16))
        return run

    def gmlp_dots():
        h = state["h"]
        base = 3 * ATTN_W
        state["gu"] = jnp.dot(h, w_ref[:, base:base + GMLP_W], preferred_element_type=F32)
        state["gv"] = jnp.dot(h, w_ref[:, base + GMLP_W:base + 2 * GMLP_W],
                              preferred_element_type=F32)

    def gmlp_tail():
        u_ref[...] = jax.nn.gelu(state["gu"]).astype(BF16)
        v = jax.nn.gelu(state["gv"])
        mu = jnp.dot(v.astype(BF16), mavg_ref[...], preferred_element_type=F32)
        vc = v - mu
        var = jnp.dot((vc * vc).astype(BF16), mavg_ref[...], preferred_element_type=F32)
        vn_ref[...] = (vc * lax.rsqrt(var + LN_EPS) * lng_ref[...]).astype(BF16)

    return dict(norm=norm, group0=group0, project1=project(1, slab1_ref),
                permute1=permute(1, slab1_ref, q1_ref), project2=project(2, slab2_ref),
                permute2=permute(2, slab2_ref, q2_ref), gmlp_dots=gmlp_dots, gmlp_tail=gmlp_tail)


FRONT_PLAN = (("norm", 0), ("gmlp_dots", 2), ("project2", 3), ("gmlp_tail", 1), ("group0", 3),
              ("permute2", 0), ("project1", 3), ("permute1", 0))


N_INPROJ_IN = 6
N_INPROJ_OUT = 5


def _inproj_kernel(*refs):
    phases = _inproj_phases(*refs)
    for name, _ in FRONT_PLAN:
        phases[name]()


def _inproj_specs(b, s, b_off, step_of):
    d = D_MODEL
    d1, d2 = DILATIONS[1], DILATIONS[2]
    tile = TOKEN_TILE

    def tok(w, off):
        return pl.BlockSpec((None, tile, w), lambda *g: (step_of(*g)[0] + off, step_of(*g)[1], 0))

    def cls(dil):
        return pl.BlockSpec((None, dil, tile // dil, QKV_W),
                            lambda *g: (step_of(*g)[0], 0, step_of(*g)[1], 0))

    const = lambda shape: pl.BlockSpec(shape, lambda *g: (0,) * len(shape))
    in_specs = [
        tok(d, b_off),
        pl.BlockSpec((None, N_MOD, d), lambda *g: (step_of(*g)[0] + b_off, 0, 0)),
        const((1, d)), const((d, IN_W)), const((GMLP_W, GMLP_W)), const((1, GMLP_W)),
    ]
    out_specs = [tok(QKV_W, 0), cls(d1), cls(d2), tok(GMLP_W, 0), tok(GMLP_W, 0)]
    out_shape = [
        jax.ShapeDtypeStruct((b, s, QKV_W), BF16),
        jax.ShapeDtypeStruct((b, d1, s // d1, QKV_W), BF16),
        jax.ShapeDtypeStruct((b, d2, s // d2, QKV_W), BF16),
        jax.ShapeDtypeStruct((b, s, GMLP_W), BF16),
        jax.ShapeDtypeStruct((b, s, GMLP_W), BF16),
    ]
    scratch = [pltpu.VMEM((QKV_W // LANES, tile, LANES), F32)] * 2
    return in_specs, out_specs, out_shape, scratch


def _inproj(x, mod3, g_pre, p, *, b_off, b):
    s = x.shape[1]
    in_specs, out_specs, out_shape, scratch = _inproj_specs(b, s, b_off, lambda bi, i: (bi, i))
    outs = pl.pallas_call(
        _inproj_kernel,
        grid=(b, s // TOKEN_TILE),
        in_specs=in_specs, out_specs=out_specs, out_shape=out_shape, scratch_shapes=scratch,
        compiler_params=_cparams(2),
        name="inproj",
    )(x, mod3, g_pre, p["w_in"], p["mavg"], p["lng"])
    return outs[:3], outs[3:]


def _attn_bias(dil, kw):
    group = DILATIONS.index(dil)
    n_heads = N_GROUPS * HEADS_PER_GROUP
    slopes = 2.0 ** (-8.0 * np.arange(1, n_heads + 1, dtype=np.float32) / n_heads)
    slopes = slopes[group * HEADS_PER_GROUP:(group + 1) * HEADS_PER_GROUP].astype(np.float32)
    n_var = 3 if kw == 2 * Q_BLOCK else 1
    i = np.arange(Q_BLOCK)[:, None]
    c = np.arange(kw)[None, :]
    out = np.empty((n_var, HEADS_PER_GROUP, Q_BLOCK, kw), np.float32)
    for v in range(n_var):
        rel = np.abs(c - N_SIDE * v - i)
        dist = (dil * rel).astype(np.float32)
        for h in range(HEADS_PER_GROUP):
            out[v, h] = np.where(rel <= N_SIDE, -slopes[h] * dist, np.float32(NEG_INF))
    return out


def _class_major(dil):
    return dil > FRONT_BLOCKS


def _attn_stages(qkv_ref, bias_ref, o_ref, l_ref, s_bufs, m_bufs, *, dil, seq, kw, coords):
    heads_per_tile = LANES // HEAD_DIM
    half = lax.broadcasted_iota(jnp.int32, (Q_BLOCK, LANES), 1) // HEAD_DIM

    def window(qs):
        return pl.multiple_of(jnp.clip(qs - N_SIDE, 0, seq - kw), N_SIDE)

    def scores(n, slot):
        r, qs, _ = coords(n)
        qs = pl.multiple_of(qs, Q_BLOCK)
        ks = window(qs)
        s_buf, m_buf = s_bufs[slot], m_bufs[slot]
        var = (qs - ks) // N_SIDE
        for h in range(HEADS_PER_GROUP):
            tile = slice((h // heads_per_tile) * LANES, (h // heads_per_tile + 1) * LANES)
            q = qkv_ref[r, pl.ds(qs, Q_BLOCK), tile]
            q = jnp.where(half == h % heads_per_tile, q, jnp.zeros_like(q))
            k = qkv_ref[r, pl.ds(ks, kw), GROUP_W + tile.start:GROUP_W + tile.stop]
            s = lax.dot_general(q, k, (((1,), (1,)), ((), ())), preferred_element_type=F32)
            bias = bias_ref[var, h]
            s = jnp.where(bias > 0.5 * NEG_INF, s + bias, NEG_INF)
            s_buf[h, :, 0:kw] = s
            m_buf[h] = jnp.broadcast_to(jnp.max(s, axis=-1, keepdims=True), (Q_BLOCK, LANES))

    def output(n, slot):
        r, qs, row0 = coords(n)
        ks = window(pl.multiple_of(qs, Q_BLOCK))
        s_buf, m_buf = s_bufs[slot], m_bufs[slot]
        if _class_major(dil):
            dst = lambda tl: (tl, row0)
        else:
            rows = pl.ds(row0, Q_BLOCK) if dil == 1 else pl.ds(row0, Q_BLOCK, stride=dil)
            dst = lambda tl: (tl, rows, slice(None))
        for tl in range(N_SLAB):
            o_pair, lse_pair = [], []
            for hp in range(heads_per_tile):
                h = tl * heads_per_tile + hp
                m = m_buf[h]
                pb = jnp.concatenate(
                    [jnp.exp(s_buf[h, :, c * LANES:(c + 1) * LANES] - m).astype(BF16)
                     for c in range(kw // LANES)], axis=-1)
                l = jnp.dot(pb, jnp.ones((kw, LANES), BF16), preferred_element_type=F32)
                v = qkv_ref[r, pl.ds(ks, kw), 2 * GROUP_W + tl * LANES:2 * GROUP_W + (tl + 1) * LANES]
                o_pair.append(jnp.dot(pb, v, preferred_element_type=F32) * (1.0 / l))
                lse_pair.append(m + jnp.log(l))
            o_ref[dst(tl)] = jnp.where(half == 0, o_pair[0], o_pair[1])
            l_ref[dst(tl)] = jnp.where(half == 0, lse_pair[0], lse_pair[1])

    return scores, output


def _attn_scratch(kw):
    return [
        pltpu.VMEM((HEADS_PER_GROUP, Q_BLOCK, kw), F32),
        pltpu.VMEM((HEADS_PER_GROUP, Q_BLOCK, kw), F32),
        pltpu.VMEM((HEADS_PER_GROUP, Q_BLOCK, LANES), F32),
        pltpu.VMEM((HEADS_PER_GROUP, Q_BLOCK, LANES), F32),
    ]


def _attn_out(dil, b, s, batch_of, tile_of):
    if _class_major(dil):
        per_row = dil // FRONT_BLOCKS
        spec = pl.BlockSpec(
            (None, N_SLAB, FRONT_BLOCKS, Q_BLOCK, LANES),
            lambda g: (batch_of(g), 0, tile_of(g) % per_row, tile_of(g) // per_row, 0))
        return spec, jax.ShapeDtypeStruct((b, N_SLAB, dil, s // dil, LANES), F32)
    spec = pl.BlockSpec((None, N_SLAB, TOKEN_TILE, LANES),
                        lambda g: (batch_of(g), 0, tile_of(g), 0))
    return spec, jax.ShapeDtypeStruct((b, N_SLAB, s, LANES), F32)


def _front_kernel(*refs, seq_tokens, steps_per_seq, project):
    n_proj_in, n_proj_out = (N_INPROJ_IN, N_INPROJ_OUT) if project else (0, 0)
    n_in = n_proj_in + 2 * N_GROUPS
    n_out = n_proj_out + 2 * N_GROUPS
    ins, outs, scr = refs[:n_in], refs[n_in:n_in + n_out], refs[n_in + n_out:]
    qkv_refs = ins[n_proj_in:n_proj_in + N_GROUPS]
    bias_refs = ins[n_proj_in + N_GROUPS:]
    attn_outs = outs[n_proj_out:]
    slab_refs, s_bufs, m_bufs = scr[:-4], scr[-4:-2], scr[-2:]
    w = pl.program_id(0) % steps_per_seq

    units = []
    for gi, dil in enumerate(DILATIONS):
        seq = seq_tokens // dil
        kw = min(2 * Q_BLOCK, seq)

        def coords(n, dil=dil):
            m = w * FRONT_BLOCKS + n
            r, j = m % dil, m // dil
            if _class_major(dil):
                row0 = n
            else:
                row0 = (n // dil) * Q_BLOCK * dil + n % dil
            return r, j * Q_BLOCK, row0

        scores, output = _attn_stages(
            qkv_refs[gi], bias_refs[gi], attn_outs[2 * gi], attn_outs[2 * gi + 1],
            s_bufs, m_bufs, dil=dil, seq=seq, kw=kw, coords=coords)
        units += [(scores, output, n) for n in range(FRONT_BLOCKS)]

    n_units = len(units)
    if project:
        phases = _inproj_phases(*ins[:N_INPROJ_IN], *outs[:N_INPROJ_OUT], *slab_refs)
        plan = FRONT_PLAN
    else:
        phases = {"attention": lambda: None}
        plan = (("attention", n_units),)
    assert sum(cnt for _, cnt in plan) == n_units
    units[0][0](units[0][2], 0)
    k = 0
    for name, cnt in plan:
        phases[name]()
        for _ in range(cnt):
            if k + 1 < n_units:
                units[k + 1][0](units[k + 1][2], (k + 1) % 2)
            units[k][1](units[k][2], k % 2)
            k += 1


def _front(x, mod3, g_pre, p, *, b_off, b, qkvs):
    b_att = qkvs[0].shape[0]
    s_att = qkvs[0].shape[1] * qkvs[0].shape[2]
    steps_per_seq = s_att // TOKEN_TILE
    n_steps = b_att * steps_per_seq
    assert s_att % ATTN_TILE == 0
    project = x is not None
    in_specs, out_specs, out_shape, scratch, operands = [], [], [], [], []
    if project:
        s = x.shape[1]
        tiles_in = s // TOKEN_TILE
        assert n_steps == b * tiles_in
        step_in = lambda g: (g // tiles_in, g % tiles_in)
        in_specs, out_specs, out_shape, scratch = _inproj_specs(b, s, b_off, step_in)
        operands = [x, mod3, g_pre, p["w_in"], p["mavg"], p["lng"]]

    qkv_bytes = N_GROUPS * s_att * QKV_W * 2
    qkv_mode = pl.Buffered(1) if project and 2 * qkv_bytes > QKV_WINDOW_BUDGET else None
    biases = []
    for gi, dil in enumerate(DILATIONS):
        d, seq = qkvs[gi].shape[1:3]
        assert d == dil
        biases.append(jnp.asarray(_attn_bias(dil, min(2 * Q_BLOCK, seq))))
        in_specs.append(pl.BlockSpec((None, d, seq, QKV_W), lambda g: (g // steps_per_seq, 0, 0, 0),
                                     pipeline_mode=qkv_mode))
    in_specs += [pl.BlockSpec(bs.shape, lambda g: (0, 0, 0, 0)) for bs in biases]
    for dil in DILATIONS:
        spec, sds = _attn_out(dil, b_att, s_att, lambda g: g // steps_per_seq,
                              lambda g: g % steps_per_seq)
        out_specs += [spec, spec]
        out_shape += [sds, sds]
    scratch = scratch + _attn_scratch(2 * Q_BLOCK)
    outs = pl.pallas_call(
        functools.partial(_front_kernel, seq_tokens=s_att, steps_per_seq=steps_per_seq,
                          project=project),
        grid=(n_steps,),
        in_specs=in_specs, out_specs=out_specs, out_shape=out_shape, scratch_shapes=scratch,
        compiler_params=_cparams(1),
        name="front" if project else "attn",
    )(*operands, *qkvs, *biases)
    if not project:
        return list(outs)
    return outs[:3], outs[3:N_INPROJ_OUT], outs[N_INPROJ_OUT:]


def _mix_tile(x, gt1, o_refs, l_refs, u_ref, vn_ref, wcat_ref, bsm_ref, wout_ref, g_ref, perm_ref):
    t = x.shape[0]

    def slabs(ref, dil, base):
        if _class_major(dil):
            for r in range(dil):
                for sb in range(N_SLAB):
                    perm_ref[base + sb, pl.ds(r, t // dil, stride=dil), :] = ref[sb, r]
            ref, lo = perm_ref, base
        else:
            lo = 0
        return jnp.concatenate([ref[lo + sb] for sb in range(N_SLAB)], axis=-1)

    lses = [slabs(r, dil, 0) for r, dil in zip(l_refs, DILATIONS)]
    m = jnp.maximum(jnp.maximum(lses[0], lses[1]), lses[2])
    es = [jnp.exp(l - m) for l in lses]
    inv = 1.0 / (es[0] + es[1] + es[2])
    parts = [(es[gi] * inv * slabs(r, dil, N_SLAB)).astype(BF16)
             for gi, (r, dil) in enumerate(zip(o_refs, DILATIONS))]

    lane_group = lax.broadcasted_iota(jnp.int32, (GMLP_CHUNK, GMLP_W), 1) // GMLP_GROUP
    n_gm = GMLP_W // GMLP_GROUP
    gms = []
    for c in range(t // GMLP_CHUNK):
        rows = slice(c * GMLP_CHUNK, (c + 1) * GMLP_CHUNK)
        vn = vn_ref[rows, :]
        stacked = jnp.concatenate(
            [jnp.where(lane_group == g, vn, jnp.zeros_like(vn)) for g in range(n_gm)], axis=0)
        sv = jnp.dot(wcat_ref[...], stacked, preferred_element_type=F32) + bsm_ref[...]
        gms.append((u_ref[rows, :].astype(F32) * sv).astype(BF16))
    parts.append(jnp.concatenate(gms, axis=0))

    mix = jnp.dot(jnp.concatenate(parts, axis=-1), wout_ref[...], preferred_element_type=F32)
    return x + _rms(mix) * (gt1 * g_ref[...])


def _ffn_tile(x, sh2, sc2, gt2, gpre_ref, wgu_ref, wdown_ref, gpost_ref):
    h = (_rms(x) * (gpre_ref[...] * (1.0 + sc2)) + sh2).astype(BF16)
    acc = None
    for c in range(D_FF // FF_CHUNK):
        gate = jnp.dot(h, wgu_ref[:, FF_CHUNK * c:FF_CHUNK * (c + 1)], preferred_element_type=F32)
        up = jnp.dot(h, wgu_ref[:, D_FF + FF_CHUNK * c:D_FF + FF_CHUNK * (c + 1)],
                     preferred_element_type=F32)
        act = (gate * jax.nn.sigmoid(gate) * up).astype(BF16)
        part = jnp.dot(act, wdown_ref[FF_CHUNK * c:FF_CHUNK * (c + 1), :],
                       preferred_element_type=F32)
        acc = part if acc is None else acc + part
    return x + _rms(acc) * (gt2 * gpost_ref[...])


N_CHUNK_IN = 2 * N_GROUPS + 2


def _post_kernel(x_ref, mod_ref, *refs, bounds):
    n = len(bounds) * N_CHUNK_IN
    wcat_ref, bsm_ref, wout_ref, gmix_ref, gpre_ref, wgu_ref, wdown_ref, gpost_ref = refs[n:n + 8]
    y_ref, perm_ref = refs[n + 8:]
    bi = pl.program_id(0)

    def run(chunk):
        o0, l0, o1, l1, o2, l2, u_ref, vn_ref = chunk
        x1 = _mix_tile(x_ref[...], mod_ref[2:3, :], (o0, o1, o2), (l0, l1, l2), u_ref, vn_ref,
                       wcat_ref, bsm_ref, wout_ref, gmix_ref, perm_ref)
        y_ref[...] = _ffn_tile(x1, mod_ref[3:4, :], mod_ref[4:5, :], mod_ref[5:6, :],
                               gpre_ref, wgu_ref, wdown_ref, gpost_ref)

    for k, (off, cnt) in enumerate(bounds):
        chunk = refs[k * N_CHUNK_IN:(k + 1) * N_CHUNK_IN]
        if len(bounds) == 1:
            run(chunk)
        else:
            pl.when((bi >= off) & (bi < off + cnt))(functools.partial(run, chunk))


def _post(x, mod3, chunks, p, gains):
    b, s, d = x.shape
    tile = TOKEN_TILE
    tok = lambda w: pl.BlockSpec((None, tile, w), lambda bi, i: (bi, i, 0))
    const = lambda a: pl.BlockSpec(a.shape, lambda bi, i: (0,) * a.ndim)
    in_specs = [tok(d), pl.BlockSpec((None, N_MOD, d), lambda bi, i: (bi, 0, 0))]
    operands = [x, mod3]
    bounds = []
    for off, att, u, vn in chunks:
        cnt = u.shape[0]
        bounds.append((off, cnt))
        local = lambda bi, off=off, cnt=cnt: jnp.clip(bi - off, 0, cnt - 1)
        for dil in DILATIONS:
            if _class_major(dil):
                spec = pl.BlockSpec((None, N_SLAB, dil, tile // dil, LANES),
                                    lambda bi, i, local=local: (local(bi), 0, 0, i, 0))
            else:
                spec = pl.BlockSpec((None, N_SLAB, tile, LANES),
                                    lambda bi, i, local=local: (local(bi), 0, i, 0))
            in_specs += [spec, spec]
        in_specs += [pl.BlockSpec((None, tile, GMLP_W),
                                  lambda bi, i, local=local: (local(bi), i, 0))] * 2
        operands += [*att, u, vn]
    g_post_mix, g_pre_ffn, g_post_ffn = gains
    consts = (p["wcat"], p["bsm"], p["w_out"], g_post_mix, g_pre_ffn, p["w_gu"], p["w_down"],
              g_post_ffn)
    return pl.pallas_call(
        functools.partial(_post_kernel, bounds=tuple(bounds)),
        grid=(b, s // tile),
        in_specs=in_specs + [const(a) for a in consts],
        out_specs=tok(d),
        out_shape=jax.ShapeDtypeStruct((b, s, d), F32),
        scratch_shapes=[pltpu.VMEM((2 * N_SLAB, tile, LANES), F32)],
        compiler_params=_cparams(2),
        name="post",
    )(*operands, *consts)


def _prep_layer(w_in, w_s, b_s, g_gmlp, w_out, w_gu, w_down):
    col_scale = np.ones((1, IN_W), np.float32)
    col_scale[:, :ATTN_W] = HEAD_DIM ** -0.5
    n_gm = GMLP_W // GMLP_GROUP
    wcat = jnp.transpose(w_s, (1, 0, 2)).reshape(GMLP_CHUNK, n_gm * GMLP_CHUNK).astype(BF16)
    bsm = jnp.repeat(b_s.T, GMLP_GROUP, axis=1).astype(F32)
    grp = np.arange(GMLP_W) // GMLP_GROUP
    mavg = jnp.asarray((grp[:, None] == grp[None, :]).astype(np.float32) / GMLP_GROUP, BF16)
    return dict(w_in=(w_in * col_scale).astype(BF16), wcat=wcat, bsm=bsm, mavg=mavg,
                lng=g_gmlp.reshape(1, GMLP_W), w_out=w_out.astype(BF16),
                w_gu=w_gu.astype(BF16), w_down=w_down.astype(BF16))


def _as_classes(q):
    q0, q1, q2 = q
    return q0.reshape(q0.shape[0], 1, q0.shape[1], q0.shape[2]), q1, q2


def _attn_all(qkvs):
    return _front(None, None, None, None, b_off=0, b=0, qkvs=qkvs)


def _layer(x_p, x_s, mod_p, mod_s, p, g_pre_mix, g_post_mix, g_pre_ffn, g_post_ffn):
    row = lambda g: g.reshape(1, D_MODEL)
    g_pre = row(g_pre_mix)
    gains = (row(g_post_mix), row(g_pre_ffn), row(g_post_ffn))
    groups = ((x_s, mod_s), (x_p, mod_p))

    def chunked(x):
        b, s = x.shape[:2]
        return (CHUNK_TOKENS % s == 0 and b % (CHUNK_TOKENS // s) == 0 and s % ATTN_TILE == 0)

    fuse = all(chunked(g[0]) for g in groups)
    chunks = []
    for gid, (x, _) in enumerate(groups):
        per = CHUNK_TOKENS // x.shape[1] if fuse else x.shape[0]
        chunks += [(gid, off, per) for off in range(0, x.shape[0], per)]

    done = [[], []]
    pending = None
    for gid, off, per in chunks:
        x, mod = groups[gid]
        if pending is not None and fuse:
            qk, pg, poff, pu, pvn = pending
            q, (u, vn), att = _front(x, mod, g_pre, p, b_off=off, b=per, qkvs=qk)
            done[pg].append((poff, att, pu, pvn))
        else:
            if pending is not None:
                qk, pg, poff, pu, pvn = pending
                done[pg].append((poff, _attn_all(qk), pu, pvn))
            q, (u, vn) = _inproj(x, mod, g_pre, p, b_off=off, b=per)
        pending = (_as_classes(q), gid, off, u, vn)
    qk, pg, poff, pu, pvn = pending
    done[pg].append((poff, _attn_all(qk), pu, pvn))

    y_s, y_p = (_post(x, mod, done[gid], p, gains) for gid, (x, mod) in enumerate(groups))
    return y_p, y_s


def kernel(x_prompt, x_sample, c_prompt, c_sample, w_ada, b_ada, g_pre_mix, w_in, w_s, b_s, g_gmlp, w_out, g_post_mix, g_pre_ffn, w_gu, w_down, g_post_ffn):
    n_p = c_prompt.shape[0]
    c_all = jnp.concatenate([c_prompt, c_sample], axis=0)
    y_prompt, y_sample = x_prompt, x_sample
    for l in range(w_ada.shape[0]):
        mod3 = _adaln(c_all, w_ada[l], b_ada[l]).reshape(c_all.shape[0], N_MOD, D_MODEL)
        p = _prep_layer(w_in[l], w_s[l], b_s[l], g_gmlp[l], w_out[l], w_gu[l], w_down[l])
        y_prompt, y_sample = _layer(y_prompt, y_sample, mod3[:n_p], mod3[n_p:], p,
                                    g_pre_mix[l], g_post_mix[l], g_pre_ffn[l], g_post_ffn[l])
    return (y_prompt, y_sample)
```

```python
import functools

import numpy as np
import jax
import jax.numpy as jnp
from jax import lax
from jax.experimental import pallas as pl
from jax.experimental.pallas import tpu as pltpu

F32 = jnp.float32
BF16 = jnp.bfloat16

D_MODEL = 1024
HEAD_DIM = 64
HEADS_PER_GROUP = 4
GROUP_W = HEADS_PER_GROUP * HEAD_DIM
DILATIONS = (1, 4, 16)
N_SIDE = 64
N_GROUPS = len(DILATIONS)
ATTN_W = N_GROUPS * GROUP_W
QKV_W = 3 * GROUP_W
GMLP_W = 256
GMLP_GROUP = 64
GMLP_CHUNK = 128
IN_W = 3 * ATTN_W + 2 * GMLP_W
D_FF = 2816
N_MOD = 6
RMS_EPS = 1e-6
LN_EPS = 1e-5
NEG_INF = -1e30

LANES = 128
N_SLAB = GROUP_W // LANES
Q_BLOCK = 128
ATTN_TILE = Q_BLOCK * DILATIONS[-1]
TOKEN_TILE = 512
CHUNK_TOKENS = 16384
FRONT_BLOCKS = TOKEN_TILE // Q_BLOCK
FF_CHUNK = 256
VMEM_LIMIT = 60 * 1024 * 1024
QKV_WINDOW_BUDGET = 20 * 1024 * 1024


def _cparams(n_axes):
    return pltpu.CompilerParams(
        dimension_semantics=("arbitrary",) * n_axes, vmem_limit_bytes=VMEM_LIMIT)


def _adaln_kernel(c_ref, w_ref, b_ref, o_ref):
    c = c_ref[...]
    a = (c * jax.nn.sigmoid(c)).astype(BF16)
    o_ref[...] = jnp.dot(a, w_ref[...].astype(BF16), preferred_element_type=F32) + b_ref[...]


def _adaln(c, w_ada, b_ada):
    n, d = c.shape
    nout = w_ada.shape[1]
    tn = 1024
    return pl.pallas_call(
        _adaln_kernel,
        grid=(nout // tn,),
        in_specs=[
            pl.BlockSpec((n, d), lambda j: (0, 0)),
            pl.BlockSpec((d, tn), lambda j: (0, j)),
            pl.BlockSpec((1, tn), lambda j: (0, j)),
        ],
        out_specs=pl.BlockSpec((n, tn), lambda j: (0, j)),
        out_shape=jax.ShapeDtypeStruct((n, nout), F32),
        compiler_params=_cparams(1),
        name="adaln",
    )(c, w_ada, b_ada.reshape(1, nout))


def _rms(x):
    return x * lax.rsqrt(jnp.mean(x * x, axis=-1, keepdims=True) + RMS_EPS)


def _inproj_phases(x_ref, mod_ref, g_ref, w_ref, mavg_ref, lng_ref,
                   q0_ref, q1_ref, q2_ref, u_ref, vn_ref, slab1_ref, slab2_ref):
    t = x_ref.shape[0]
    state = {}

    def norm():
        sh1 = mod_ref[0:1, :]
        sc1 = mod_ref[1:2, :]
        state["h"] = (_rms(x_ref[...]) * (g_ref[...] * (1.0 + sc1)) + sh1).astype(BF16)

    def qkv(gi):
        h = state["h"]
        return jnp.concatenate(
            [jnp.dot(h, w_ref[:, part * ATTN_W + gi * GROUP_W:part * ATTN_W + (gi + 1) * GROUP_W],
                     preferred_element_type=F32) for part in range(3)], axis=-1)

    def group0():
        q0_ref[...] = qkv(0).astype(BF16)

    def project(gi, slab_ref):
        def run():
            p = qkv(gi)
            for j in range(QKV_W // LANES):
                slab_ref[j] = p[:, j * LANES:(j + 1) * LANES]
        return run

    def permute(gi, slab_ref, out_ref):
        def run():
            d = DILATIONS[gi]
            for r in range(d):
                for j in range(QKV_W // LANES):
                    out_ref[r, :, j * LANES:(j + 1) * LANES] = (
                        slab_ref[j, pl.ds(r, t // d, stride=d), :].astype(BF16))
        return run

    def gmlp_dots():
        h = state["h"]
        base = 3 * ATTN_W
        state["gu"] = jnp.dot(h, w_ref[:, base:base + GMLP_W], preferred_element_type=F32)
        state["gv"] = jnp.dot(h, w_ref[:, base + GMLP_W:base + 2 * GMLP_W],
                              preferred_element_type=F32)

    def gmlp_tail():
        u_ref[...] = jax.nn.gelu(state["gu"]).astype(BF16)
        v = jax.nn.gelu(state["gv"])
        mu = jnp.dot(v.astype(BF16), mavg_ref[...], preferred_element_type=F32)
        vc = v - mu
        var = jnp.dot((vc * vc).astype(BF16), mavg_ref[...], preferred_element_type=F32)
        vn_ref[...] = (vc * lax.rsqrt(var + LN_EPS) * lng_ref[...]).astype(BF16)

    return dict(norm=norm, group0=group0, project1=project(1, slab1_ref),
                permute1=permute(1, slab1_ref, q1_ref), project2=project(2, slab2_ref),
                permute2=permute(2, slab2_ref, q2_ref), gmlp_dots=gmlp_dots, gmlp_tail=gmlp_tail)


FRONT_PLAN = (("norm", 0), ("gmlp_dots", 1), ("project2", 3), ("gmlp_tail", 1), ("project1", 3),
              ("permute2", 1), ("group0", 3), ("permute1", 0))


N_INPROJ_IN = 6
N_INPROJ_OUT = 5


def _inproj_kernel(*refs):
    phases = _inproj_phases(*refs)
    for name, _ in FRONT_PLAN:
        phases[name]()


def _inproj_specs(b, s, b_off, step_of):
    d = D_MODEL
    d1, d2 = DILATIONS[1], DILATIONS[2]
    tile = TOKEN_TILE

    def tok(w, off):
        return pl.BlockSpec((None, tile, w), lambda *g: (step_of(*g)[0] + off, step_of(*g)[1], 0))

    def cls(dil):
        return pl.BlockSpec((None, dil, tile // dil, QKV_W),
                            lambda *g: (step_of(*g)[0], 0, step_of(*g)[1], 0))

    const = lambda shape: pl.BlockSpec(shape, lambda *g: (0,) * len(shape))
    in_specs = [
        tok(d, b_off),
        pl.BlockSpec((None, N_MOD, d), lambda *g: (step_of(*g)[0] + b_off, 0, 0)),
        const((1, d)), const((d, IN_W)), const((GMLP_W, GMLP_W)), const((1, GMLP_W)),
    ]
    out_specs = [tok(QKV_W, 0), cls(d1), cls(d2), tok(GMLP_W, 0), tok(GMLP_W, 0)]
    out_shape = [
        jax.ShapeDtypeStruct((b, s, QKV_W), BF16),
        jax.ShapeDtypeStruct((b, d1, s // d1, QKV_W), BF16),
        jax.ShapeDtypeStruct((b, d2, s // d2, QKV_W), BF16),
        jax.ShapeDtypeStruct((b, s, GMLP_W), BF16),
        jax.ShapeDtypeStruct((b, s, GMLP_W), BF16),
    ]
    scratch = [pltpu.VMEM((QKV_W // LANES, tile, LANES), F32)] * 2
    return in_specs, out_specs, out_shape, scratch


def _inproj(x, mod3, g_pre, p, *, b_off, b):
    s = x.shape[1]
    in_specs, out_specs, out_shape, scratch = _inproj_specs(b, s, b_off, lambda bi, i: (bi, i))
    outs = pl.pallas_call(
        _inproj_kernel,
        grid=(b, s // TOKEN_TILE),
        in_specs=in_specs, out_specs=out_specs, out_shape=out_shape, scratch_shapes=scratch,
        compiler_params=_cparams(2),
        name="inproj",
    )(x, mod3, g_pre, p["w_in"], p["mavg"], p["lng"])
    return outs[:3], outs[3:]


def _attn_bias(dil, kw):
    group = DILATIONS.index(dil)
    n_heads = N_GROUPS * HEADS_PER_GROUP
    slopes = 2.0 ** (-8.0 * np.arange(1, n_heads + 1, dtype=np.float32) / n_heads)
    slopes = slopes[group * HEADS_PER_GROUP:(group + 1) * HEADS_PER_GROUP].astype(np.float32)
    n_var = 3 if kw == 2 * Q_BLOCK else 1
    i = np.arange(Q_BLOCK)[:, None]
    c = np.arange(kw)[None, :]
    out = np.empty((n_var, HEADS_PER_GROUP, Q_BLOCK, kw), np.float32)
    for v in range(n_var):
        rel = np.abs(c - N_SIDE * v - i)
        dist = (dil * rel).astype(np.float32)
        for h in range(HEADS_PER_GROUP):
            out[v, h] = np.where(rel <= N_SIDE, -slopes[h] * dist, np.float32(NEG_INF))
    return out


def _class_major(dil):
    return dil > FRONT_BLOCKS


def _attn_stages(qkv_ref, bias_ref, o_ref, l_ref, s_bufs, m_bufs, *, dil, seq, kw, coords):
    heads_per_tile = LANES // HEAD_DIM
    half = lax.broadcasted_iota(jnp.int32, (Q_BLOCK, LANES), 1) // HEAD_DIM

    def window(qs):
        return pl.multiple_of(jnp.clip(qs - N_SIDE, 0, seq - kw), N_SIDE)

    def scores(n, slot):
        r, qs, _ = coords(n)
        qs = pl.multiple_of(qs, Q_BLOCK)
        ks = window(qs)
        s_buf, m_buf = s_bufs[slot], m_bufs[slot]
        var = (qs - ks) // N_SIDE
        for h in range(HEADS_PER_GROUP):
            tile = slice((h // heads_per_tile) * LANES, (h // heads_per_tile + 1) * LANES)
            q = qkv_ref[r, pl.ds(qs, Q_BLOCK), tile]
            q = jnp.where(half == h % heads_per_tile, q, jnp.zeros_like(q))
            k = qkv_ref[r, pl.ds(ks, kw), GROUP_W + tile.start:GROUP_W + tile.stop]
            s = lax.dot_general(q, k, (((1,), (1,)), ((), ())), preferred_element_type=F32)
            bias = bias_ref[var, h]
            s = jnp.where(bias > 0.5 * NEG_INF, s + bias, NEG_INF)
            s_buf[h, :, 0:kw] = s
            m_buf[h] = jnp.broadcast_to(jnp.max(s, axis=-1, keepdims=True), (Q_BLOCK, LANES))

    def output(n, slot):
        r, qs, row0 = coords(n)
        ks = window(pl.multiple_of(qs, Q_BLOCK))
        s_buf, m_buf = s_bufs[slot], m_bufs[slot]
        if _class_major(dil):
            dst = lambda tl: (tl, row0)
        else:
            rows = pl.ds(row0, Q_BLOCK) if dil == 1 else pl.ds(row0, Q_BLOCK, stride=dil)
            dst = lambda tl: (tl, rows, slice(None))
        for tl in range(N_SLAB):
            o_pair, lse_pair = [], []
            for hp in range(heads_per_tile):
                h = tl * heads_per_tile + hp
                m = m_buf[h]
                p = [jnp.exp(s_buf[h, :, c * LANES:(c + 1) * LANES] - m)
                     for c in range(kw // LANES)]
                psum = p[0] if len(p) == 1 else p[0] + p[1]
                l = jnp.sum(psum, axis=-1, keepdims=True)
                pb = jnp.concatenate(p, axis=-1).astype(BF16)
                v = qkv_ref[r, pl.ds(ks, kw), 2 * GROUP_W + tl * LANES:2 * GROUP_W + (tl + 1) * LANES]
                o_pair.append(jnp.dot(pb, v, preferred_element_type=F32) * (1.0 / l))
                lse_pair.append(m + jnp.log(l))
            o_ref[dst(tl)] = jnp.where(half == 0, o_pair[0], o_pair[1])
            l_ref[dst(tl)] = jnp.where(half == 0, lse_pair[0], lse_pair[1])

    return scores, output


def _attn_scratch(kw):
    return [
        pltpu.VMEM((HEADS_PER_GROUP, Q_BLOCK, kw), F32),
        pltpu.VMEM((HEADS_PER_GROUP, Q_BLOCK, kw), F32),
        pltpu.VMEM((HEADS_PER_GROUP, Q_BLOCK, LANES), F32),
        pltpu.VMEM((HEADS_PER_GROUP, Q_BLOCK, LANES), F32),
    ]


def _attn_out(dil, b, s, batch_of, tile_of):
    if _class_major(dil):
        per_row = dil // FRONT_BLOCKS
        spec = pl.BlockSpec(
            (None, N_SLAB, FRONT_BLOCKS, Q_BLOCK, LANES),
            lambda g: (batch_of(g), 0, tile_of(g) % per_row, tile_of(g) // per_row, 0))
        return spec, jax.ShapeDtypeStruct((b, N_SLAB, dil, s // dil, LANES), F32)
    spec = pl.BlockSpec((None, N_SLAB, TOKEN_TILE, LANES),
                        lambda g: (batch_of(g), 0, tile_of(g), 0))
    return spec, jax.ShapeDtypeStruct((b, N_SLAB, s, LANES), F32)


def _front_kernel(*refs, seq_tokens, steps_per_seq, project):
    n_proj_in, n_proj_out = (N_INPROJ_IN, N_INPROJ_OUT) if project else (0, 0)
    n_in = n_proj_in + 2 * N_GROUPS
    n_out = n_proj_out + 2 * N_GROUPS
    ins, outs, scr = refs[:n_in], refs[n_in:n_in + n_out], refs[n_in + n_out:]
    qkv_refs = ins[n_proj_in:n_proj_in + N_GROUPS]
    bias_refs = ins[n_proj_in + N_GROUPS:]
    attn_outs = outs[n_proj_out:]
    slab_refs, s_bufs, m_bufs = scr[:-4], scr[-4:-2], scr[-2:]
    w = pl.program_id(0) % steps_per_seq

    units = []
    for gi, dil in enumerate(DILATIONS):
        seq = seq_tokens // dil
        kw = min(2 * Q_BLOCK, seq)

        def coords(n, dil=dil):
            m = w * FRONT_BLOCKS + n
            r, j = m % dil, m // dil
            if _class_major(dil):
                row0 = n
            else:
                row0 = (n // dil) * Q_BLOCK * dil + n % dil
            return r, j * Q_BLOCK, row0

        scores, output = _attn_stages(
            qkv_refs[gi], bias_refs[gi], attn_outs[2 * gi], attn_outs[2 * gi + 1],
            s_bufs, m_bufs, dil=dil, seq=seq, kw=kw, coords=coords)
        units += [(scores, output, n) for n in range(FRONT_BLOCKS)]

    n_units = len(units)
    if project:
        phases = _inproj_phases(*ins[:N_INPROJ_IN], *outs[:N_INPROJ_OUT], *slab_refs)
        plan = FRONT_PLAN
    else:
        phases = {"attention": lambda: None}
        plan = (("attention", n_units),)
    assert sum(cnt for _, cnt in plan) == n_units
    units[0][0](units[0][2], 0)
    k = 0
    for name, cnt in plan:
        phases[name]()
        for _ in range(cnt):
            if k + 1 < n_units:
                units[k + 1][0](units[k + 1][2], (k + 1) % 2)
            units[k][1](units[k][2], k % 2)
            k += 1


def _front(x, mod3, g_pre, p, *, b_off, b, qkvs):
    b_att = qkvs[0].shape[0]
    s_att = qkvs[0].shape[1] * qkvs[0].shape[2]
    steps_per_seq = s_att // TOKEN_TILE
    n_steps = b_att * steps_per_seq
    assert s_att % ATTN_TILE == 0
    project = x is not None
    in_specs, out_specs, out_shape, scratch, operands = [], [], [], [], []
    if project:
        s = x.shape[1]
        tiles_in = s // TOKEN_TILE
        assert n_steps == b * tiles_in
        step_in = lambda g: (g // tiles_in, g % tiles_in)
        in_specs, out_specs, out_shape, scratch = _inproj_specs(b, s, b_off, step_in)
        operands = [x, mod3, g_pre, p["w_in"], p["mavg"], p["lng"]]

    qkv_bytes = N_GROUPS * s_att * QKV_W * 2
    qkv_mode = pl.Buffered(1) if project and 2 * qkv_bytes > QKV_WINDOW_BUDGET else None
    biases = []
    for gi, dil in enumerate(DILATIONS):
        d, seq = qkvs[gi].shape[1:3]
        assert d == dil
        biases.append(jnp.asarray(_attn_bias(dil, min(2 * Q_BLOCK, seq))))
        in_specs.append(pl.BlockSpec((None, d, seq, QKV_W), lambda g: (g // steps_per_seq, 0, 0, 0),
                                     pipeline_mode=qkv_mode))
    in_specs += [pl.BlockSpec(bs.shape, lambda g: (0, 0, 0, 0)) for bs in biases]
    for dil in DILATIONS:
        spec, sds = _attn_out(dil, b_att, s_att, lambda g: g // steps_per_seq,
                              lambda g: g % steps_per_seq)
        out_specs += [spec, spec]
        out_shape += [sds, sds]
    scratch = scratch + _attn_scratch(2 * Q_BLOCK)
    outs = pl.pallas_call(
        functools.partial(_front_kernel, seq_tokens=s_att, steps_per_seq=steps_per_seq,
                          project=project),
        grid=(n_steps,),
        in_specs=in_specs, out_specs=out_specs, out_shape=out_shape, scratch_shapes=scratch,
        compiler_params=_cparams(1),
        name="front" if project else "attn",
    )(*operands, *qkvs, *biases)
    if not project:
        return list(outs)
    return outs[:3], outs[3:N_INPROJ_OUT], outs[N_INPROJ_OUT:]


def _mix_tile(x, gt1, o_refs, l_refs, u_ref, vn_ref, wcat_ref, bsm_ref, wout_ref, g_ref, perm_ref):
    t = x.shape[0]

    def slabs(ref, dil, base):
        if _class_major(dil):
            for r in range(dil):
                for sb in range(N_SLAB):
                    perm_ref[base + sb, pl.ds(r, t // dil, stride=dil), :] = ref[sb, r]
            ref, lo = perm_ref, base
        else:
            lo = 0
        return jnp.concatenate([ref[lo + sb] for sb in range(N_SLAB)], axis=-1)

    lses = [slabs(r, dil, 0) for r, dil in zip(l_refs, DILATIONS)]
    m = jnp.maximum(jnp.maximum(lses[0], lses[1]), lses[2])
    es = [jnp.exp(l - m) for l in lses]
    inv = 1.0 / (es[0] + es[1] + es[2])
    parts = [(es[gi] * inv * slabs(r, dil, N_SLAB)).astype(BF16)
             for gi, (r, dil) in enumerate(zip(o_refs, DILATIONS))]

    lane_group = lax.broadcasted_iota(jnp.int32, (GMLP_CHUNK, GMLP_W), 1) // GMLP_GROUP
    n_gm = GMLP_W // GMLP_GROUP
    gms = []
    for c in range(t // GMLP_CHUNK):
        rows = slice(c * GMLP_CHUNK, (c + 1) * GMLP_CHUNK)
        vn = vn_ref[rows, :]
        stacked = jnp.concatenate(
            [jnp.where(lane_group == g, vn, jnp.zeros_like(vn)) for g in range(n_gm)], axis=0)
        sv = jnp.dot(wcat_ref[...], stacked, preferred_element_type=F32) + bsm_ref[...]
        gms.append((u_ref[rows, :].astype(F32) * sv).astype(BF16))
    parts.append(jnp.concatenate(gms, axis=0))

    mix = jnp.dot(jnp.concatenate(parts, axis=-1), wout_ref[...], preferred_element_type=F32)
    return x + _rms(mix) * (gt1 * g_ref[...])


def _ffn_tile(x, sh2, sc2, gt2, gpre_ref, wgu_ref, wdown_ref, gpost_ref):
    h = (_rms(x) * (gpre_ref[...] * (1.0 + sc2)) + sh2).astype(BF16)
    acc = None
    for c in range(D_FF // FF_CHUNK):
        gate = jnp.dot(h, wgu_ref[:, FF_CHUNK * c:FF_CHUNK * (c + 1)], preferred_element_type=F32)
        up = jnp.dot(h, wgu_ref[:, D_FF + FF_CHUNK * c:D_FF + FF_CHUNK * (c + 1)],
                     preferred_element_type=F32)
        act = (gate * jax.nn.sigmoid(gate) * up).astype(BF16)
        part = jnp.dot(act, wdown_ref[FF_CHUNK * c:FF_CHUNK * (c + 1), :],
                       preferred_element_type=F32)
        acc = part if acc is None else acc + part
    return x + _rms(acc) * (gt2 * gpost_ref[...])


N_CHUNK_IN = 2 * N_GROUPS + 2


def _post_kernel(x_ref, mod_ref, *refs, bounds):
    n = len(bounds) * N_CHUNK_IN
    wcat_ref, bsm_ref, wout_ref, gmix_ref, gpre_ref, wgu_ref, wdown_ref, gpost_ref = refs[n:n + 8]
    y_ref, perm_ref = refs[n + 8:]
    bi = pl.program_id(0)

    def run(chunk):
        o0, l0, o1, l1, o2, l2, u_ref, vn_ref = chunk
        x1 = _mix_tile(x_ref[...], mod_ref[2:3, :], (o0, o1, o2), (l0, l1, l2), u_ref, vn_ref,
                       wcat_ref, bsm_ref, wout_ref, gmix_ref, perm_ref)
        y_ref[...] = _ffn_tile(x1, mod_ref[3:4, :], mod_ref[4:5, :], mod_ref[5:6, :],
                               gpre_ref, wgu_ref, wdown_ref, gpost_ref)

    for k, (off, cnt) in enumerate(bounds):
        chunk = refs[k * N_CHUNK_IN:(k + 1) * N_CHUNK_IN]
        if len(bounds) == 1:
            run(chunk)
        else:
            pl.when((bi >= off) & (bi < off + cnt))(functools.partial(run, chunk))


def _post(x, mod3, chunks, p, gains):
    b, s, d = x.shape
    tile = TOKEN_TILE
    tok = lambda w: pl.BlockSpec((None, tile, w), lambda bi, i: (bi, i, 0))
    const = lambda a: pl.BlockSpec(a.shape, lambda bi, i: (0,) * a.ndim)
    in_specs = [tok(d), pl.BlockSpec((None, N_MOD, d), lambda bi, i: (bi, 0, 0))]
    operands = [x, mod3]
    bounds = []
    for off, att, u, vn in chunks:
        cnt = u.shape[0]
        bounds.append((off, cnt))
        local = lambda bi, off=off, cnt=cnt: jnp.clip(bi - off, 0, cnt - 1)
        for dil in DILATIONS:
            if _class_major(dil):
                spec = pl.BlockSpec((None, N_SLAB, dil, tile // dil, LANES),
                                    lambda bi, i, local=local: (local(bi), 0, 0, i, 0))
            else:
                spec = pl.BlockSpec((None, N_SLAB, tile, LANES),
                                    lambda bi, i, local=local: (local(bi), 0, i, 0))
            in_specs += [spec, spec]
        in_specs += [pl.BlockSpec((None, tile, GMLP_W),
                                  lambda bi, i, local=local: (local(bi), i, 0))] * 2
        operands += [*att, u, vn]
    g_post_mix, g_pre_ffn, g_post_ffn = gains
    consts = (p["wcat"], p["bsm"], p["w_out"], g_post_mix, g_pre_ffn, p["w_gu"], p["w_down"],
              g_post_ffn)
    return pl.pallas_call(
        functools.partial(_post_kernel, bounds=tuple(bounds)),
        grid=(b, s // tile),
        in_specs=in_specs + [const(a) for a in consts],
        out_specs=tok(d),
        out_shape=jax.ShapeDtypeStruct((b, s, d), F32),
        scratch_shapes=[pltpu.VMEM((2 * N_SLAB, tile, LANES), F32)],
        compiler_params=_cparams(2),
        name="post",
    )(*operands, *consts)


def _prep_layer(w_in, w_s, b_s, g_gmlp, w_out, w_gu, w_down):
    col_scale = np.ones((1, IN_W), np.float32)
    col_scale[:, :ATTN_W] = HEAD_DIM ** -0.5
    n_gm = GMLP_W // GMLP_GROUP
    wcat = jnp.transpose(w_s, (1, 0, 2)).reshape(GMLP_CHUNK, n_gm * GMLP_CHUNK).astype(BF16)
    bsm = jnp.repeat(b_s.T, GMLP_GROUP, axis=1).astype(F32)
    grp = np.arange(GMLP_W) // GMLP_GROUP
    mavg = jnp.asarray((grp[:, None] == grp[None, :]).astype(np.float32) / GMLP_GROUP, BF16)
    return dict(w_in=(w_in * col_scale).astype(BF16), wcat=wcat, bsm=bsm, mavg=mavg,
                lng=g_gmlp.reshape(1, GMLP_W), w_out=w_out.astype(BF16),
                w_gu=w_gu.astype(BF16), w_down=w_down.astype(BF16))


def _as_classes(q):
    q0, q1, q2 = q
    return q0.reshape(q0.shape[0], 1, q0.shape[1], q0.shape[2]), q1, q2


def _attn_all(qkvs):
    return _front(None, None, None, None, b_off=0, b=0, qkvs=qkvs)


def _layer(x_p, x_s, mod_p, mod_s, p, g_pre_mix, g_post_mix, g_pre_ffn, g_post_ffn):
    row = lambda g: g.reshape(1, D_MODEL)
    g_pre = row(g_pre_mix)
    gains = (row(g_post_mix), row(g_pre_ffn), row(g_post_ffn))
    groups = ((x_s, mod_s), (x_p, mod_p))

    def chunked(x):
        b, s = x.shape[:2]
        return (CHUNK_TOKENS % s == 0 and b % (CHUNK_TOKENS // s) == 0 and s % ATTN_TILE == 0)

    fuse = all(chunked(g[0]) for g in groups)
    chunks = []
    for gid, (x, _) in enumerate(groups):
        per = CHUNK_TOKENS // x.shape[1] if fuse else x.shape[0]
        chunks += [(gid, off, per) for off in range(0, x.shape[0], per)]

    done = [[], []]
    pending = None
    for gid, off, per in chunks:
        x, mod = groups[gid]
        if pending is not None and fuse:
            qk, pg, poff, pu, pvn = pending
            q, (u, vn), att = _front(x, mod, g_pre, p, b_off=off, b=per, qkvs=qk)
            done[pg].append((poff, att, pu, pvn))
        else:
            if pending is not None:
                qk, pg, poff, pu, pvn = pending
                done[pg].append((poff, _attn_all(qk), pu, pvn))
            q, (u, vn) = _inproj(x, mod, g_pre, p, b_off=off, b=per)
        pending = (_as_classes(q), gid, off, u, vn)
    qk, pg, poff, pu, pvn = pending
    done[pg].append((poff, _attn_all(qk), pu, pvn))

    y_s, y_p = (_post(x, mod, done[gid], p, gains) for gid, (x, mod) in enumerate(groups))
    return y_p, y_s


def kernel(x_prompt, x_sample, c_prompt, c_sample, w_ada, b_ada, g_pre_mix, w_in, w_s, b_s, g_gmlp, w_out, g_post_mix, g_pre_ffn, w_gu, w_down, g_post_ffn):
    n_p = c_prompt.shape[0]
    c_all = jnp.concatenate([c_prompt, c_sample], axis=0)
    y_prompt, y_sample = x_prompt, x_sample
    for l in range(w_ada.shape[0]):
        mod3 = _adaln(c_all, w_ada[l], b_ada[l]).reshape(c_all.shape[0], N_MOD, D_MODEL)
        p = _prep_layer(w_in[l], w_s[l], b_s[l], g_gmlp[l], w_out[l], w_gu[l], w_down[l])
        y_prompt, y_sample = _layer(y_prompt, y_sample, mod3[:n_p], mod3[n_p:], p,
                                    g_pre_mix[l], g_post_mix[l], g_pre_ffn[l], g_post_ffn[l])
    return (y_prompt, y_sample)
```

```python
import functools

import numpy as np
import jax
import jax.numpy as jnp
from jax import lax
from jax.experimental import pallas as pl
from jax.experimental.pallas import tpu as pltpu

F32 = jnp.float32
BF16 = jnp.bfloat16

D_MODEL = 1024
HEAD_DIM = 64
HEADS_PER_GROUP = 4
GROUP_W = HEADS_PER_GROUP * HEAD_DIM
DILATIONS = (1, 4, 16)
N_SIDE = 64
N_GROUPS = len(DILATIONS)
ATTN_W = N_GROUPS * GROUP_W
QKV_W = 3 * GROUP_W
GMLP_W = 256
GMLP_GROUP = 64
GMLP_CHUNK = 128
IN_W = 3 * ATTN_W + 2 * GMLP_W
D_FF = 2816
N_MOD = 6
RMS_EPS = 1e-6
LN_EPS = 1e-5
NEG_INF = -1e30

LANES = 128
N_SLAB = GROUP_W // LANES
Q_BLOCK = 128
PERMUTE_STRIDE = 4
ATTN_TILE = Q_BLOCK * DILATIONS[-1]
TOKEN_TILE = 512
CHUNK_TOKENS = 16384
FRONT_BLOCKS = TOKEN_TILE // Q_BLOCK
FF_CHUNK = 256
VMEM_LIMIT = 60 * 1024 * 1024
QKV_WINDOW_BUDGET = 20 * 1024 * 1024


def _cparams(n_axes):
    return pltpu.CompilerParams(
        dimension_semantics=("arbitrary",) * n_axes, vmem_limit_bytes=VMEM_LIMIT)


def _adaln_kernel(c_ref, w_ref, b_ref, o_ref):
    c = c_ref[...]
    a = (c * jax.nn.sigmoid(c)).astype(BF16)
    o_ref[...] = jnp.dot(a, w_ref[...].astype(BF16), preferred_element_type=F32) + b_ref[...]


def _adaln(c, w_ada, b_ada):
    n, d = c.shape
    nout = w_ada.shape[1]
    tn = 1024
    return pl.pallas_call(
        _adaln_kernel,
        grid=(nout // tn,),
        in_specs=[
            pl.BlockSpec((n, d), lambda j: (0, 0)),
            pl.BlockSpec((d, tn), lambda j: (0, j)),
            pl.BlockSpec((1, tn), lambda j: (0, j)),
        ],
        out_specs=pl.BlockSpec((n, tn), lambda j: (0, j)),
        out_shape=jax.ShapeDtypeStruct((n, nout), F32),
        compiler_params=_cparams(1),
        name="adaln",
    )(c, w_ada, b_ada.reshape(1, nout))


def _rms(x):
    return x * lax.rsqrt(jnp.mean(x * x, axis=-1, keepdims=True) + RMS_EPS)


def _inproj_phases(x_ref, mod_ref, g_ref, w_ref, mavg_ref, lng_ref,
                   q0_ref, q1_ref, q2_ref, u_ref, vn_ref, slab1_ref, slab2_ref, spare_ref):
    t = x_ref.shape[0]
    state = {}

    def norm():
        sh1 = mod_ref[0:1, :]
        sc1 = mod_ref[1:2, :]
        state["h"] = (_rms(x_ref[...]) * (g_ref[...] * (1.0 + sc1)) + sh1).astype(BF16)

    def qkv(gi):
        h = state["h"]
        return jnp.concatenate(
            [jnp.dot(h, w_ref[:, part * ATTN_W + gi * GROUP_W:part * ATTN_W + (gi + 1) * GROUP_W],
                     preferred_element_type=F32) for part in range(3)], axis=-1)

    def group0():
        q0_ref[...] = qkv(0).astype(BF16)

    def project(gi, slab_ref):
        def run():
            p = qkv(gi)
            for j in range(QKV_W // LANES):
                slab_ref[j] = p[:, j * LANES:(j + 1) * LANES]
        return run

    def permute(gi, slab_ref, out_ref):
        def run():
            d = DILATIONS[gi]
            src, split = slab_ref, 1
            if d > PERMUTE_STRIDE:
                src, split = spare_ref, PERMUTE_STRIDE
                for a in range(split):
                    for j in range(QKV_W // LANES):
                        spare_ref[j, a * (t // split):(a + 1) * (t // split), :] = (
                            slab_ref[j, pl.ds(a, t // split, stride=split), :])
            for r in range(d):
                a, b = r % split, r // split
                rows = pl.ds(a * (t // split) + b, t // d, stride=d // split)
                for j in range(QKV_W // LANES):
                    out_ref[r, :, j * LANES:(j + 1) * LANES] = src[j, rows, :].astype(BF16)
        return run

    def gmlp_dots():
        h = state["h"]
        base = 3 * ATTN_W
        state["gu"] = jnp.dot(h, w_ref[:, base:base + GMLP_W], preferred_element_type=F32)
        state["gv"] = jnp.dot(h, w_ref[:, base + GMLP_W:base + 2 * GMLP_W],
                              preferred_element_type=F32)

    def gmlp_tail():
        u_ref[...] = jax.nn.gelu(state["gu"]).astype(BF16)
        v = jax.nn.gelu(state["gv"])
        mu = jnp.dot(v.astype(BF16), mavg_ref[...], preferred_element_type=F32)
        vc = v - mu
        var = jnp.dot((vc * vc).astype(BF16), mavg_ref[...], preferred_element_type=F32)
        vn_ref[...] = (vc * lax.rsqrt(var + LN_EPS) * lng_ref[...]).astype(BF16)

    return dict(norm=norm, group0=group0, project1=project(1, slab1_ref),
                permute1=permute(1, slab1_ref, q1_ref), project2=project(2, slab2_ref),
                permute2=permute(2, slab2_ref, q2_ref), gmlp_dots=gmlp_dots, gmlp_tail=gmlp_tail)


FRONT_PLAN = (("norm", 0), ("gmlp_dots", 1), ("project2", 3), ("gmlp_tail", 1), ("project1", 3),
              ("permute2", 1), ("group0", 3), ("permute1", 0))


N_INPROJ_IN = 6
N_INPROJ_OUT = 5


def _inproj_kernel(*refs):
    phases = _inproj_phases(*refs)
    for name, _ in FRONT_PLAN:
        phases[name]()


def _inproj_specs(b, s, b_off, step_of):
    d = D_MODEL
    d1, d2 = DILATIONS[1], DILATIONS[2]
    tile = TOKEN_TILE

    def tok(w, off):
        return pl.BlockSpec((None, tile, w), lambda *g: (step_of(*g)[0] + off, step_of(*g)[1], 0))

    def cls(dil):
        return pl.BlockSpec((None, dil, tile // dil, QKV_W),
                            lambda *g: (step_of(*g)[0], 0, step_of(*g)[1], 0))

    const = lambda shape: pl.BlockSpec(shape, lambda *g: (0,) * len(shape))
    in_specs = [
        tok(d, b_off),
        pl.BlockSpec((None, N_MOD, d), lambda *g: (step_of(*g)[0] + b_off, 0, 0)),
        const((1, d)), const((d, IN_W)), const((GMLP_W, GMLP_W)), const((1, GMLP_W)),
    ]
    out_specs = [tok(QKV_W, 0), cls(d1), cls(d2), tok(GMLP_W, 0), tok(GMLP_W, 0)]
    out_shape = [
        jax.ShapeDtypeStruct((b, s, QKV_W), BF16),
        jax.ShapeDtypeStruct((b, d1, s // d1, QKV_W), BF16),
        jax.ShapeDtypeStruct((b, d2, s // d2, QKV_W), BF16),
        jax.ShapeDtypeStruct((b, s, GMLP_W), BF16),
        jax.ShapeDtypeStruct((b, s, GMLP_W), BF16),
    ]
    scratch = [pltpu.VMEM((QKV_W // LANES, tile, LANES), F32)] * 3
    return in_specs, out_specs, out_shape, scratch


def _inproj(x, mod3, g_pre, p, *, b_off, b):
    s = x.shape[1]
    in_specs, out_specs, out_shape, scratch = _inproj_specs(b, s, b_off, lambda bi, i: (bi, i))
    outs = pl.pallas_call(
        _inproj_kernel,
        grid=(b, s // TOKEN_TILE),
        in_specs=in_specs, out_specs=out_specs, out_shape=out_shape, scratch_shapes=scratch,
        compiler_params=_cparams(2),
        name="inproj",
    )(x, mod3, g_pre, p["w_in"], p["mavg"], p["lng"])
    return outs[:3], outs[3:]


def _attn_bias(dil, kw):
    group = DILATIONS.index(dil)
    n_heads = N_GROUPS * HEADS_PER_GROUP
    slopes = 2.0 ** (-8.0 * np.arange(1, n_heads + 1, dtype=np.float32) / n_heads)
    slopes = slopes[group * HEADS_PER_GROUP:(group + 1) * HEADS_PER_GROUP].astype(np.float32)
    n_var = 3 if kw == 2 * Q_BLOCK else 1
    i = np.arange(Q_BLOCK)[:, None]
    c = np.arange(kw)[None, :]
    out = np.empty((n_var, HEADS_PER_GROUP, Q_BLOCK, kw), np.float32)
    for v in range(n_var):
        rel = np.abs(c - N_SIDE * v - i)
        dist = (dil * rel).astype(np.float32)
        for h in range(HEADS_PER_GROUP):
            out[v, h] = np.where(rel <= N_SIDE, -slopes[h] * dist, np.float32(NEG_INF))
    return out


def _class_major(dil):
    return dil > FRONT_BLOCKS


def _attn_stages(qkv_ref, bias_ref, o_ref, l_ref, s_bufs, m_bufs, *, dil, seq, kw, coords):
    heads_per_tile = LANES // HEAD_DIM
    half = lax.broadcasted_iota(jnp.int32, (Q_BLOCK, LANES), 1) // HEAD_DIM

    def window(qs):
        return pl.multiple_of(jnp.clip(qs - N_SIDE, 0, seq - kw), N_SIDE)

    def scores(n, slot):
        r, qs, _ = coords(n)
        qs = pl.multiple_of(qs, Q_BLOCK)
        ks = window(qs)
        s_buf, m_buf = s_bufs[slot], m_bufs[slot]
        var = (qs - ks) // N_SIDE
        for h in range(HEADS_PER_GROUP):
            tile = slice((h // heads_per_tile) * LANES, (h // heads_per_tile + 1) * LANES)
            q = qkv_ref[r, pl.ds(qs, Q_BLOCK), tile]
            q = jnp.where(half == h % heads_per_tile, q, jnp.zeros_like(q))
            k = qkv_ref[r, pl.ds(ks, kw), GROUP_W + tile.start:GROUP_W + tile.stop]
            s = lax.dot_general(q, k, (((1,), (1,)), ((), ())), preferred_element_type=F32)
            bias = bias_ref[var, h]
            s = jnp.where(bias > 0.5 * NEG_INF, s + bias, NEG_INF)
            s_buf[h, :, 0:kw] = s
            m_buf[h] = jnp.broadcast_to(jnp.max(s, axis=-1, keepdims=True), (Q_BLOCK, LANES))

    def output(n, slot):
        r, qs, row0 = coords(n)
        ks = window(pl.multiple_of(qs, Q_BLOCK))
        s_buf, m_buf = s_bufs[slot], m_bufs[slot]
        if _class_major(dil):
            dst = lambda tl: (tl, row0)
        else:
            rows = pl.ds(row0, Q_BLOCK) if dil == 1 else pl.ds(row0, Q_BLOCK, stride=dil)
            dst = lambda tl: (tl, rows, slice(None))
        for tl in range(N_SLAB):
            o_pair, lse_pair = [], []
            for hp in range(heads_per_tile):
                h = tl * heads_per_tile + hp
                m = m_buf[h]
                p = [jnp.exp(s_buf[h, :, c * LANES:(c + 1) * LANES] - m)
                     for c in range(kw // LANES)]
                psum = p[0] if len(p) == 1 else p[0] + p[1]
                l = jnp.sum(psum, axis=-1, keepdims=True)
                pb = jnp.concatenate(p, axis=-1).astype(BF16)
                v = qkv_ref[r, pl.ds(ks, kw), 2 * GROUP_W + tl * LANES:2 * GROUP_W + (tl + 1) * LANES]
                o_pair.append(jnp.dot(pb, v, preferred_element_type=F32) * (1.0 / l))
                lse_pair.append(m + jnp.log(l))
            o_ref[dst(tl)] = jnp.where(half == 0, o_pair[0], o_pair[1])
            l_ref[dst(tl)] = jnp.where(half == 0, lse_pair[0], lse_pair[1])

    return scores, output


def _attn_scratch(kw):
    return [
        pltpu.VMEM((HEADS_PER_GROUP, Q_BLOCK, kw), F32),
        pltpu.VMEM((HEADS_PER_GROUP, Q_BLOCK, kw), F32),
        pltpu.VMEM((HEADS_PER_GROUP, Q_BLOCK, LANES), F32),
        pltpu.VMEM((HEADS_PER_GROUP, Q_BLOCK, LANES), F32),
    ]


def _attn_out(dil, b, s, batch_of, tile_of):
    if _class_major(dil):
        per_row = dil // FRONT_BLOCKS
        spec = pl.BlockSpec(
            (None, N_SLAB, FRONT_BLOCKS, Q_BLOCK, LANES),
            lambda g: (batch_of(g), 0, tile_of(g) % per_row, tile_of(g) // per_row, 0))
        return spec, jax.ShapeDtypeStruct((b, N_SLAB, dil, s // dil, LANES), F32)
    spec = pl.BlockSpec((None, N_SLAB, TOKEN_TILE, LANES),
                        lambda g: (batch_of(g), 0, tile_of(g), 0))
    return spec, jax.ShapeDtypeStruct((b, N_SLAB, s, LANES), F32)


def _front_kernel(*refs, seq_tokens, steps_per_seq, project):
    n_proj_in, n_proj_out = (N_INPROJ_IN, N_INPROJ_OUT) if project else (0, 0)
    n_in = n_proj_in + 2 * N_GROUPS
    n_out = n_proj_out + 2 * N_GROUPS
    ins, outs, scr = refs[:n_in], refs[n_in:n_in + n_out], refs[n_in + n_out:]
    qkv_refs = ins[n_proj_in:n_proj_in + N_GROUPS]
    bias_refs = ins[n_proj_in + N_GROUPS:]
    attn_outs = outs[n_proj_out:]
    slab_refs, s_bufs, m_bufs = scr[:-4], scr[-4:-2], scr[-2:]
    w = pl.program_id(0) % steps_per_seq

    units = []
    for gi, dil in enumerate(DILATIONS):
        seq = seq_tokens // dil
        kw = min(2 * Q_BLOCK, seq)

        def coords(n, dil=dil):
            m = w * FRONT_BLOCKS + n
            r, j = m % dil, m // dil
            if _class_major(dil):
                row0 = n
            else:
                row0 = (n // dil) * Q_BLOCK * dil + n % dil
            return r, j * Q_BLOCK, row0

        scores, output = _attn_stages(
            qkv_refs[gi], bias_refs[gi], attn_outs[2 * gi], attn_outs[2 * gi + 1],
            s_bufs, m_bufs, dil=dil, seq=seq, kw=kw, coords=coords)
        units += [(scores, output, n) for n in range(FRONT_BLOCKS)]

    n_units = len(units)
    if project:
        phases = _inproj_phases(*ins[:N_INPROJ_IN], *outs[:N_INPROJ_OUT], *slab_refs)
        plan = FRONT_PLAN
    else:
        phases = {"attention": lambda: None}
        plan = (("attention", n_units),)
    assert sum(cnt for _, cnt in plan) == n_units
    units[0][0](units[0][2], 0)
    k = 0
    for name, cnt in plan:
        phases[name]()
        for _ in range(cnt):
            if k + 1 < n_units:
                units[k + 1][0](units[k + 1][2], (k + 1) % 2)
            units[k][1](units[k][2], k % 2)
            k += 1


def _front(x, mod3, g_pre, p, *, b_off, b, qkvs):
    b_att = qkvs[0].shape[0]
    s_att = qkvs[0].shape[1] * qkvs[0].shape[2]
    steps_per_seq = s_att // TOKEN_TILE
    n_steps = b_att * steps_per_seq
    assert s_att % ATTN_TILE == 0
    project = x is not None
    in_specs, out_specs, out_shape, scratch, operands = [], [], [], [], []
    if project:
        s = x.shape[1]
        tiles_in = s // TOKEN_TILE
        assert n_steps == b * tiles_in
        step_in = lambda g: (g // tiles_in, g % tiles_in)
        in_specs, out_specs, out_shape, scratch = _inproj_specs(b, s, b_off, step_in)
        operands = [x, mod3, g_pre, p["w_in"], p["mavg"], p["lng"]]

    qkv_bytes = N_GROUPS * s_att * QKV_W * 2
    qkv_mode = pl.Buffered(1) if project and 2 * qkv_bytes > QKV_WINDOW_BUDGET else None
    biases = []
    for gi, dil in enumerate(DILATIONS):
        d, seq = qkvs[gi].shape[1:3]
        assert d == dil
        biases.append(jnp.asarray(_attn_bias(dil, min(2 * Q_BLOCK, seq))))
        in_specs.append(pl.BlockSpec((None, d, seq, QKV_W), lambda g: (g // steps_per_seq, 0, 0, 0),
                                     pipeline_mode=qkv_mode))
    in_specs += [pl.BlockSpec(bs.shape, lambda g: (0, 0, 0, 0)) for bs in biases]
    for dil in DILATIONS:
        spec, sds = _attn_out(dil, b_att, s_att, lambda g: g // steps_per_seq,
                              lambda g: g % steps_per_seq)
        out_specs += [spec, spec]
        out_shape += [sds, sds]
    scratch = scratch + _attn_scratch(2 * Q_BLOCK)
    outs = pl.pallas_call(
        functools.partial(_front_kernel, seq_tokens=s_att, steps_per_seq=steps_per_seq,
                          project=project),
        grid=(n_steps,),
        in_specs=in_specs, out_specs=out_specs, out_shape=out_shape, scratch_shapes=scratch,
        compiler_params=_cparams(1),
        name="front" if project else "attn",
    )(*operands, *qkvs, *biases)
    if not project:
        return list(outs)
    return outs[:3], outs[3:N_INPROJ_OUT], outs[N_INPROJ_OUT:]


def _mix_tile(x, gt1, o_refs, l_refs, u_ref, vn_ref, wcat_ref, bsm_ref, wout_ref, g_ref, perm_ref):
    t = x.shape[0]

    def slabs(ref, dil, base):
        if _class_major(dil):
            for r in range(dil):
                for sb in range(N_SLAB):
                    perm_ref[base + sb, pl.ds(r, t // dil, stride=dil), :] = ref[sb, r]
            ref, lo = perm_ref, base
        else:
            lo = 0
        return jnp.concatenate([ref[lo + sb] for sb in range(N_SLAB)], axis=-1)

    lses = [slabs(r, dil, 0) for r, dil in zip(l_refs, DILATIONS)]
    m = jnp.maximum(jnp.maximum(lses[0], lses[1]), lses[2])
    es = [jnp.exp(l - m) for l in lses]
    inv = 1.0 / (es[0] + es[1] + es[2])
    parts = [(es[gi] * inv * slabs(r, dil, N_SLAB)).astype(BF16)
             for gi, (r, dil) in enumerate(zip(o_refs, DILATIONS))]

    lane_group = lax.broadcasted_iota(jnp.int32, (GMLP_CHUNK, GMLP_W), 1) // GMLP_GROUP
    n_gm = GMLP_W // GMLP_GROUP
    gms = []
    for c in range(t // GMLP_CHUNK):
        rows = slice(c * GMLP_CHUNK, (c + 1) * GMLP_CHUNK)
        vn = vn_ref[rows, :]
        stacked = jnp.concatenate(
            [jnp.where(lane_group == g, vn, jnp.zeros_like(vn)) for g in range(n_gm)], axis=0)
        sv = jnp.dot(wcat_ref[...], stacked, preferred_element_type=F32) + bsm_ref[...]
        gms.append((u_ref[rows, :].astype(F32) * sv).astype(BF16))
    parts.append(jnp.concatenate(gms, axis=0))

    mix = jnp.dot(jnp.concatenate(parts, axis=-1), wout_ref[...], preferred_element_type=F32)
    return x + _rms(mix) * (gt1 * g_ref[...])


def _ffn_tile(x, sh2, sc2, gt2, gpre_ref, wgu_ref, wdown_ref, gpost_ref):
    h = (_rms(x) * (gpre_ref[...] * (1.0 + sc2)) + sh2).astype(BF16)
    acc = None
    for c in range(D_FF // FF_CHUNK):
        gate = jnp.dot(h, wgu_ref[:, FF_CHUNK * c:FF_CHUNK * (c + 1)], preferred_element_type=F32)
        up = jnp.dot(h, wgu_ref[:, D_FF + FF_CHUNK * c:D_FF + FF_CHUNK * (c + 1)],
                     preferred_element_type=F32)
        act = (gate * jax.nn.sigmoid(gate) * up).astype(BF16)
        part = jnp.dot(act, wdown_ref[FF_CHUNK * c:FF_CHUNK * (c + 1), :],
                       preferred_element_type=F32)
        acc = part if acc is None else acc + part
    return x + _rms(acc) * (gt2 * gpost_ref[...])


N_CHUNK_IN = 2 * N_GROUPS + 2


def _post_kernel(x_ref, mod_ref, *refs, bounds):
    n = len(bounds) * N_CHUNK_IN
    wcat_ref, bsm_ref, wout_ref, gmix_ref, gpre_ref, wgu_ref, wdown_ref, gpost_ref = refs[n:n + 8]
    y_ref, perm_ref = refs[n + 8:]
    bi = pl.program_id(0)

    def run(chunk):
        o0, l0, o1, l1, o2, l2, u_ref, vn_ref = chunk
        x1 = _mix_tile(x_ref[...], mod_ref[2:3, :], (o0, o1, o2), (l0, l1, l2), u_ref, vn_ref,
                       wcat_ref, bsm_ref, wout_ref, gmix_ref, perm_ref)
        y_ref[...] = _ffn_tile(x1, mod_ref[3:4, :], mod_ref[4:5, :], mod_ref[5:6, :],
                               gpre_ref, wgu_ref, wdown_ref, gpost_ref)

    for k, (off, cnt) in enumerate(bounds):
        chunk = refs[k * N_CHUNK_IN:(k + 1) * N_CHUNK_IN]
        if len(bounds) == 1:
            run(chunk)
        else:
            pl.when((bi >= off) & (bi < off + cnt))(functools.partial(run, chunk))


def _post(x, mod3, chunks, p, gains):
    b, s, d = x.shape
    tile = TOKEN_TILE
    tok = lambda w: pl.BlockSpec((None, tile, w), lambda bi, i: (bi, i, 0))
    const = lambda a: pl.BlockSpec(a.shape, lambda bi, i: (0,) * a.ndim)
    in_specs = [tok(d), pl.BlockSpec((None, N_MOD, d), lambda bi, i: (bi, 0, 0))]
    operands = [x, mod3]
    bounds = []
    for off, att, u, vn in chunks:
        cnt = u.shape[0]
        bounds.append((off, cnt))
        local = lambda bi, off=off, cnt=cnt: jnp.clip(bi - off, 0, cnt - 1)
        for dil in DILATIONS:
            if _class_major(dil):
                spec = pl.BlockSpec((None, N_SLAB, dil, tile // dil, LANES),
                                    lambda bi, i, local=local: (local(bi), 0, 0, i, 0))
            else:
                spec = pl.BlockSpec((None, N_SLAB, tile, LANES),
                                    lambda bi, i, local=local: (local(bi), 0, i, 0))
            in_specs += [spec, spec]
        in_specs += [pl.BlockSpec((None, tile, GMLP_W),
                                  lambda bi, i, local=local: (local(bi), i, 0))] * 2
        operands += [*att, u, vn]
    g_post_mix, g_pre_ffn, g_post_ffn = gains
    consts = (p["wcat"], p["bsm"], p["w_out"], g_post_mix, g_pre_ffn, p["w_gu"], p["w_down"],
              g_post_ffn)
    return pl.pallas_call(
        functools.partial(_post_kernel, bounds=tuple(bounds)),
        grid=(b, s // tile),
        in_specs=in_specs + [const(a) for a in consts],
        out_specs=tok(d),
        out_shape=jax.ShapeDtypeStruct((b, s, d), F32),
        scratch_shapes=[pltpu.VMEM((2 * N_SLAB, tile, LANES), F32)],
        compiler_params=_cparams(2),
        name="post",
    )(*operands, *consts)


def _prep_layer(w_in, w_s, b_s, g_gmlp, w_out, w_gu, w_down):
    col_scale = np.ones((1, IN_W), np.float32)
    col_scale[:, :ATTN_W] = HEAD_DIM ** -0.5
    n_gm = GMLP_W // GMLP_GROUP
    wcat = jnp.transpose(w_s, (1, 0, 2)).reshape(GMLP_CHUNK, n_gm * GMLP_CHUNK).astype(BF16)
    bsm = jnp.repeat(b_s.T, GMLP_GROUP, axis=1).astype(F32)
    grp = np.arange(GMLP_W) // GMLP_GROUP
    mavg = jnp.asarray((grp[:, None] == grp[None, :]).astype(np.float32) / GMLP_GROUP, BF16)
    return dict(w_in=(w_in * col_scale).astype(BF16), wcat=wcat, bsm=bsm, mavg=mavg,
                lng=g_gmlp.reshape(1, GMLP_W), w_out=w_out.astype(BF16),
                w_gu=w_gu.astype(BF16), w_down=w_down.astype(BF16))


def _as_classes(q):
    q0, q1, q2 = q
    return q0.reshape(q0.shape[0], 1, q0.shape[1], q0.shape[2]), q1, q2


def _attn_all(qkvs):
    return _front(None, None, None, None, b_off=0, b=0, qkvs=qkvs)


def _layer(x_p, x_s, mod_p, mod_s, p, g_pre_mix, g_post_mix, g_pre_ffn, g_post_ffn):
    row = lambda g: g.reshape(1, D_MODEL)
    g_pre = row(g_pre_mix)
    gains = (row(g_post_mix), row(g_pre_ffn), row(g_post_ffn))
    groups = ((x_s, mod_s), (x_p, mod_p))

    def chunked(x):
        b, s = x.shape[:2]
        return (CHUNK_TOKENS % s == 0 and b % (CHUNK_TOKENS // s) == 0 and s % ATTN_TILE == 0)

    fuse = all(chunked(g[0]) for g in groups)
    chunks = []
    for gid, (x, _) in enumerate(groups):
        per = CHUNK_TOKENS // x.shape[1] if fuse else x.shape[0]
        chunks += [(gid, off, per) for off in range(0, x.shape[0], per)]

    done = [[], []]
    pending = None
    for gid, off, per in chunks:
        x, mod = groups[gid]
        if pending is not None and fuse:
            qk, pg, poff, pu, pvn = pending
            q, (u, vn), att = _front(x, mod, g_pre, p, b_off=off, b=per, qkvs=qk)
            done[pg].append((poff, att, pu, pvn))
        else:
            if pending is not None:
                qk, pg, poff, pu, pvn = pending
                done[pg].append((poff, _attn_all(qk), pu, pvn))
            q, (u, vn) = _inproj(x, mod, g_pre, p, b_off=off, b=per)
        pending = (_as_classes(q), gid, off, u, vn)
    qk, pg, poff, pu, pvn = pending
    done[pg].append((poff, _attn_all(qk), pu, pvn))

    y_s, y_p = (_post(x, mod, done[gid], p, gains) for gid, (x, mod) in enumerate(groups))
    return y_p, y_s


def kernel(x_prompt, x_sample, c_prompt, c_sample, w_ada, b_ada, g_pre_mix, w_in, w_s, b_s, g_gmlp, w_out, g_post_mix, g_pre_ffn, w_gu, w_down, g_post_ffn):
    n_p = c_prompt.shape[0]
    c_all = jnp.concatenate([c_prompt, c_sample], axis=0)
    y_prompt, y_sample = x_prompt, x_sample
    for l in range(w_ada.shape[0]):
        mod3 = _adaln(c_all, w_ada[l], b_ada[l]).reshape(c_all.shape[0], N_MOD, D_MODEL)
        p = _prep_layer(w_in[l], w_s[l], b_s[l], g_gmlp[l], w_out[l], w_gu[l], w_down[l])
        y_prompt, y_sample = _layer(y_prompt, y_sample, mod3[:n_p], mod3[n_p:], p,
                                    g_pre_mix[l], g_post_mix[l], g_pre_ffn[l], g_post_ffn[l])
    return (y_prompt, y_sample)
```

```python
import functools

import numpy as np
import jax
import jax.numpy as jnp
from jax import lax
from jax.experimental import pallas as pl
from jax.experimental.pallas import tpu as pltpu

F32 = jnp.float32
BF16 = jnp.bfloat16

D_MODEL = 1024
HEAD_DIM = 64
HEADS_PER_GROUP = 4
GROUP_W = HEADS_PER_GROUP * HEAD_DIM
DILATIONS = (1, 4, 16)
N_SIDE = 64
N_GROUPS = len(DILATIONS)
ATTN_W = N_GROUPS * GROUP_W
QKV_W = 3 * GROUP_W
GMLP_W = 256
GMLP_GROUP = 64
GMLP_CHUNK = 128
IN_W = 3 * ATTN_W + 2 * GMLP_W
D_FF = 2816
N_MOD = 6
RMS_EPS = 1e-6
LN_EPS = 1e-5
NEG_INF = -1e30

LANES = 128
N_SLAB = GROUP_W // LANES
Q_BLOCK = 128
PERMUTE_STRIDE = 4
BF16_ROWS = 16
ATTN_TILE = Q_BLOCK * DILATIONS[-1]
TOKEN_TILE = 512
CHUNK_TOKENS = 16384
FRONT_BLOCKS = TOKEN_TILE // Q_BLOCK
FF_CHUNK = 256
VMEM_LIMIT = 60 * 1024 * 1024
QKV_WINDOW_BUDGET = 20 * 1024 * 1024


def _cparams(n_axes):
    return pltpu.CompilerParams(
        dimension_semantics=("arbitrary",) * n_axes, vmem_limit_bytes=VMEM_LIMIT)


def _adaln_kernel(c_ref, w_ref, b_ref, o_ref):
    c = c_ref[...]
    a = (c * jax.nn.sigmoid(c)).astype(BF16)
    o_ref[...] = jnp.dot(a, w_ref[...].astype(BF16), preferred_element_type=F32) + b_ref[...]


def _adaln(c, w_ada, b_ada):
    n, d = c.shape
    nout = w_ada.shape[1]
    tn = 1024
    return pl.pallas_call(
        _adaln_kernel,
        grid=(nout // tn,),
        in_specs=[
            pl.BlockSpec((n, d), lambda j: (0, 0)),
            pl.BlockSpec((d, tn), lambda j: (0, j)),
            pl.BlockSpec((1, tn), lambda j: (0, j)),
        ],
        out_specs=pl.BlockSpec((n, tn), lambda j: (0, j)),
        out_shape=jax.ShapeDtypeStruct((n, nout), F32),
        compiler_params=_cparams(1),
        name="adaln",
    )(c, w_ada, b_ada.reshape(1, nout))


def _rms(x):
    return x * lax.rsqrt(jnp.mean(x * x, axis=-1, keepdims=True) + RMS_EPS)


def _inproj_phases(x_ref, mod_ref, g_ref, w_ref, mavg_ref, lng_ref,
                   q0_ref, q1_ref, q2_ref, u_ref, vn_ref, slab1_ref, slab2_ref, spare_ref):
    t = x_ref.shape[0]
    state = {}

    def norm():
        sh1 = mod_ref[0:1, :]
        sc1 = mod_ref[1:2, :]
        state["h"] = (_rms(x_ref[...]) * (g_ref[...] * (1.0 + sc1)) + sh1).astype(BF16)

    def qkv(gi):
        h = state["h"]
        return jnp.concatenate(
            [jnp.dot(h, w_ref[:, part * ATTN_W + gi * GROUP_W:part * ATTN_W + (gi + 1) * GROUP_W],
                     preferred_element_type=F32) for part in range(3)], axis=-1)

    def group0():
        q0_ref[...] = qkv(0).astype(BF16)

    def project(gi, slab_ref):
        def run():
            p = qkv(gi)
            for j in range(QKV_W // LANES):
                slab_ref[j] = p[:, j * LANES:(j + 1) * LANES]
        return run

    def permute(gi, slab_ref, out_ref):
        def run():
            d = DILATIONS[gi]
            src, split = slab_ref, 1
            if d > PERMUTE_STRIDE:
                src, split = spare_ref, PERMUTE_STRIDE
                for a in range(split):
                    for j in range(QKV_W // LANES):
                        spare_ref[j, a * (t // split):(a + 1) * (t // split), :] = (
                            slab_ref[j, pl.ds(a, t // split, stride=split), :])
            for r in range(d):
                a, b = r % split, r // split
                rows = pl.ds(a * (t // split) + b, t // d, stride=d // split)
                for j in range(QKV_W // LANES):
                    out_ref[r, :, j * LANES:(j + 1) * LANES] = src[j, rows, :].astype(BF16)
        return run

    def gmlp_dots():
        h = state["h"]
        base = 3 * ATTN_W
        state["gu"] = jnp.dot(h, w_ref[:, base:base + GMLP_W], preferred_element_type=F32)
        state["gv"] = jnp.dot(h, w_ref[:, base + GMLP_W:base + 2 * GMLP_W],
                              preferred_element_type=F32)

    def gmlp_tail():
        u_ref[...] = jax.nn.gelu(state["gu"]).astype(BF16)
        v = jax.nn.gelu(state["gv"])
        mu = jnp.dot(v.astype(BF16), mavg_ref[...], preferred_element_type=F32)
        vc = v - mu
        var = jnp.dot((vc * vc).astype(BF16), mavg_ref[...], preferred_element_type=F32)
        vn_ref[...] = (vc * lax.rsqrt(var + LN_EPS) * lng_ref[...]).astype(BF16)

    return dict(norm=norm, group0=group0, project1=project(1, slab1_ref),
                permute1=permute(1, slab1_ref, q1_ref), project2=project(2, slab2_ref),
                permute2=permute(2, slab2_ref, q2_ref), gmlp_dots=gmlp_dots, gmlp_tail=gmlp_tail)


FRONT_PLAN = (("norm", 0), ("gmlp_dots", 1), ("project2", 3), ("gmlp_tail", 1), ("project1", 3),
              ("permute2", 1), ("group0", 3), ("permute1", 0))


POST_WEIGHTS = ("w_gu", "w_down", "w_out")
N_INPROJ_IN = 6
N_INPROJ_OUT = 5


def _inproj_kernel(*refs, n_cast):
    ins, rest = refs[:N_INPROJ_IN], refs[N_INPROJ_IN:]
    cast_in, rest = rest[:n_cast], rest[n_cast:]
    outs, rest = rest[:N_INPROJ_OUT], rest[N_INPROJ_OUT:]
    cast_out, scratch = rest[:n_cast], rest[n_cast:]
    phases = _inproj_phases(*ins, *outs, *scratch)
    for k, (name, _) in enumerate(FRONT_PLAN):
        phases[name]()
        if k < n_cast:
            cast_out[k][...] = cast_in[k][...].astype(BF16)


def _inproj_specs(b, s, b_off, step_of):
    d = D_MODEL
    d1, d2 = DILATIONS[1], DILATIONS[2]
    tile = TOKEN_TILE

    def tok(w, off):
        return pl.BlockSpec((None, tile, w), lambda *g: (step_of(*g)[0] + off, step_of(*g)[1], 0))

    def cls(dil):
        return pl.BlockSpec((None, dil, tile // dil, QKV_W),
                            lambda *g: (step_of(*g)[0], 0, step_of(*g)[1], 0))

    const = lambda shape: pl.BlockSpec(shape, lambda *g: (0,) * len(shape))
    in_specs = [
        tok(d, b_off),
        pl.BlockSpec((None, N_MOD, d), lambda *g: (step_of(*g)[0] + b_off, 0, 0)),
        const((1, d)), const((d, IN_W)), const((GMLP_W, GMLP_W)), const((1, GMLP_W)),
    ]
    out_specs = [tok(QKV_W, 0), cls(d1), cls(d2), tok(GMLP_W, 0), tok(GMLP_W, 0)]
    out_shape = [
        jax.ShapeDtypeStruct((b, s, QKV_W), BF16),
        jax.ShapeDtypeStruct((b, d1, s // d1, QKV_W), BF16),
        jax.ShapeDtypeStruct((b, d2, s // d2, QKV_W), BF16),
        jax.ShapeDtypeStruct((b, s, GMLP_W), BF16),
        jax.ShapeDtypeStruct((b, s, GMLP_W), BF16),
    ]
    scratch = [pltpu.VMEM((QKV_W // LANES, tile, LANES), F32)] * 3
    return in_specs, out_specs, out_shape, scratch


def _cast_spec(w, n_steps, tiles):
    rows_total = w.shape[0]
    repeat = 1
    while (rows_total * repeat) % n_steps or (rows_total * repeat // n_steps) % BF16_ROWS:
        repeat *= 2
        assert repeat <= n_steps
    rows = rows_total * repeat // n_steps
    return pl.BlockSpec((rows, w.shape[1]), lambda bi, i: ((bi * tiles + i) // repeat, 0))


def _inproj(x, mod3, g_pre, p, *, b_off, b, cast=()):
    s = x.shape[1]
    tiles = s // TOKEN_TILE
    in_specs, out_specs, out_shape, scratch = _inproj_specs(b, s, b_off, lambda bi, i: (bi, i))
    cast_specs = [_cast_spec(w, b * tiles, tiles) for w in cast]
    outs = pl.pallas_call(
        functools.partial(_inproj_kernel, n_cast=len(cast)),
        grid=(b, tiles),
        in_specs=in_specs + cast_specs,
        out_specs=out_specs + cast_specs,
        out_shape=out_shape + [jax.ShapeDtypeStruct(w.shape, BF16) for w in cast],
        scratch_shapes=scratch,
        compiler_params=_cparams(2),
        name="inproj",
    )(x, mod3, g_pre, p["w_in"], p["mavg"], p["lng"], *cast)
    return outs[:3], outs[3:N_INPROJ_OUT], outs[N_INPROJ_OUT:]


def _attn_bias(dil, kw):
    group = DILATIONS.index(dil)
    n_heads = N_GROUPS * HEADS_PER_GROUP
    slopes = 2.0 ** (-8.0 * np.arange(1, n_heads + 1, dtype=np.float32) / n_heads)
    slopes = slopes[group * HEADS_PER_GROUP:(group + 1) * HEADS_PER_GROUP].astype(np.float32)
    n_var = 3 if kw == 2 * Q_BLOCK else 1
    i = np.arange(Q_BLOCK)[:, None]
    c = np.arange(kw)[None, :]
    out = np.empty((n_var, HEADS_PER_GROUP, Q_BLOCK, kw), np.float32)
    for v in range(n_var):
        rel = np.abs(c - N_SIDE * v - i)
        dist = (dil * rel).astype(np.float32)
        for h in range(HEADS_PER_GROUP):
            out[v, h] = np.where(rel <= N_SIDE, -slopes[h] * dist, np.float32(NEG_INF))
    return out


def _class_major(dil):
    return dil > FRONT_BLOCKS


def _attn_stages(qkv_ref, bias_ref, o_ref, l_ref, s_bufs, m_bufs, *, dil, seq, kw, coords):
    heads_per_tile = LANES // HEAD_DIM
    half = lax.broadcasted_iota(jnp.int32, (Q_BLOCK, LANES), 1) // HEAD_DIM

    def window(qs):
        return pl.multiple_of(jnp.clip(qs - N_SIDE, 0, seq - kw), N_SIDE)

    def scores(n, slot):
        r, qs, _ = coords(n)
        qs = pl.multiple_of(qs, Q_BLOCK)
        ks = window(qs)
        s_buf, m_buf = s_bufs[slot], m_bufs[slot]
        var = (qs - ks) // N_SIDE
        for h in range(HEADS_PER_GROUP):
            tile = slice((h // heads_per_tile) * LANES, (h // heads_per_tile + 1) * LANES)
            q = qkv_ref[r, pl.ds(qs, Q_BLOCK), tile]
            q = jnp.where(half == h % heads_per_tile, q, jnp.zeros_like(q))
            k = qkv_ref[r, pl.ds(ks, kw), GROUP_W + tile.start:GROUP_W + tile.stop]
            s = lax.dot_general(q, k, (((1,), (1,)), ((), ())), preferred_element_type=F32)
            bias = bias_ref[var, h]
            s = jnp.where(bias > 0.5 * NEG_INF, s + bias, NEG_INF)
            s_buf[h, :, 0:kw] = s
            m_buf[h] = jnp.broadcast_to(jnp.max(s, axis=-1, keepdims=True), (Q_BLOCK, LANES))

    def output(n, slot):
        r, qs, row0 = coords(n)
        ks = window(pl.multiple_of(qs, Q_BLOCK))
        s_buf, m_buf = s_bufs[slot], m_bufs[slot]
        if _class_major(dil):
            dst = lambda tl: (tl, row0)
        else:
            rows = pl.ds(row0, Q_BLOCK) if dil == 1 else pl.ds(row0, Q_BLOCK, stride=dil)
            dst = lambda tl: (tl, rows, slice(None))
        for tl in range(N_SLAB):
            o_pair, lse_pair = [], []
            for hp in range(heads_per_tile):
                h = tl * heads_per_tile + hp
                m = m_buf[h]
                p = [jnp.exp(s_buf[h, :, c * LANES:(c + 1) * LANES] - m)
                     for c in range(kw // LANES)]
                psum = p[0] if len(p) == 1 else p[0] + p[1]
                l = jnp.sum(psum, axis=-1, keepdims=True)
                pb = jnp.concatenate(p, axis=-1).astype(BF16)
                v = qkv_ref[r, pl.ds(ks, kw), 2 * GROUP_W + tl * LANES:2 * GROUP_W + (tl + 1) * LANES]
                o_pair.append(jnp.dot(pb, v, preferred_element_type=F32) * (1.0 / l))
                lse_pair.append(m + jnp.log(l))
            o_ref[dst(tl)] = jnp.where(half == 0, o_pair[0], o_pair[1])
            l_ref[dst(tl)] = jnp.where(half == 0, lse_pair[0], lse_pair[1])

    return scores, output


def _attn_scratch(kw):
    return [
        pltpu.VMEM((HEADS_PER_GROUP, Q_BLOCK, kw), F32),
        pltpu.VMEM((HEADS_PER_GROUP, Q_BLOCK, kw), F32),
        pltpu.VMEM((HEADS_PER_GROUP, Q_BLOCK, LANES), F32),
        pltpu.VMEM((HEADS_PER_GROUP, Q_BLOCK, LANES), F32),
    ]


def _attn_out(dil, b, s, batch_of, tile_of):
    if _class_major(dil):
        per_row = dil // FRONT_BLOCKS
        spec = pl.BlockSpec(
            (None, N_SLAB, FRONT_BLOCKS, Q_BLOCK, LANES),
            lambda g: (batch_of(g), 0, tile_of(g) % per_row, tile_of(g) // per_row, 0))
        return spec, jax.ShapeDtypeStruct((b, N_SLAB, dil, s // dil, LANES), F32)
    spec = pl.BlockSpec((None, N_SLAB, TOKEN_TILE, LANES),
                        lambda g: (batch_of(g), 0, tile_of(g), 0))
    return spec, jax.ShapeDtypeStruct((b, N_SLAB, s, LANES), F32)


def _front_kernel(*refs, seq_tokens, steps_per_seq, project):
    n_proj_in, n_proj_out = (N_INPROJ_IN, N_INPROJ_OUT) if project else (0, 0)
    n_in = n_proj_in + 2 * N_GROUPS
    n_out = n_proj_out + 2 * N_GROUPS
    ins, outs, scr = refs[:n_in], refs[n_in:n_in + n_out], refs[n_in + n_out:]
    qkv_refs = ins[n_proj_in:n_proj_in + N_GROUPS]
    bias_refs = ins[n_proj_in + N_GROUPS:]
    attn_outs = outs[n_proj_out:]
    slab_refs, s_bufs, m_bufs = scr[:-4], scr[-4:-2], scr[-2:]
    w = pl.program_id(0) % steps_per_seq

    units = []
    for gi, dil in enumerate(DILATIONS):
        seq = seq_tokens // dil
        kw = min(2 * Q_BLOCK, seq)

        def coords(n, dil=dil):
            m = w * FRONT_BLOCKS + n
            r, j = m % dil, m // dil
            if _class_major(dil):
                row0 = n
            else:
                row0 = (n // dil) * Q_BLOCK * dil + n % dil
            return r, j * Q_BLOCK, row0

        scores, output = _attn_stages(
            qkv_refs[gi], bias_refs[gi], attn_outs[2 * gi], attn_outs[2 * gi + 1],
            s_bufs, m_bufs, dil=dil, seq=seq, kw=kw, coords=coords)
        units += [(scores, output, n) for n in range(FRONT_BLOCKS)]

    n_units = len(units)
    if project:
        phases = _inproj_phases(*ins[:N_INPROJ_IN], *outs[:N_INPROJ_OUT], *slab_refs)
        plan = FRONT_PLAN
    else:
        phases = {"attention": lambda: None}
        plan = (("attention", n_units),)
    assert sum(cnt for _, cnt in plan) == n_units
    units[0][0](units[0][2], 0)
    k = 0
    for name, cnt in plan:
        phases[name]()
        for _ in range(cnt):
            if k + 1 < n_units:
                units[k + 1][0](units[k + 1][2], (k + 1) % 2)
            units[k][1](units[k][2], k % 2)
            k += 1


def _front(x, mod3, g_pre, p, *, b_off, b, qkvs):
    b_att = qkvs[0].shape[0]
    s_att = qkvs[0].shape[1] * qkvs[0].shape[2]
    steps_per_seq = s_att // TOKEN_TILE
    n_steps = b_att * steps_per_seq
    assert s_att % ATTN_TILE == 0
    project = x is not None
    in_specs, out_specs, out_shape, scratch, operands = [], [], [], [], []
    if project:
        s = x.shape[1]
        tiles_in = s // TOKEN_TILE
        assert n_steps == b * tiles_in
        step_in = lambda g: (g // tiles_in, g % tiles_in)
        in_specs, out_specs, out_shape, scratch = _inproj_specs(b, s, b_off, step_in)
        operands = [x, mod3, g_pre, p["w_in"], p["mavg"], p["lng"]]

    qkv_bytes = N_GROUPS * s_att * QKV_W * 2
    qkv_mode = pl.Buffered(1) if project and 2 * qkv_bytes > QKV_WINDOW_BUDGET else None
    biases = []
    for gi, dil in enumerate(DILATIONS):
        d, seq = qkvs[gi].shape[1:3]
        assert d == dil
        biases.append(jnp.asarray(_attn_bias(dil, min(2 * Q_BLOCK, seq))))
        in_specs.append(pl.BlockSpec((None, d, seq, QKV_W), lambda g: (g // steps_per_seq, 0, 0, 0),
                                     pipeline_mode=qkv_mode))
    in_specs += [pl.BlockSpec(bs.shape, lambda g: (0, 0, 0, 0)) for bs in biases]
    for dil in DILATIONS:
        spec, sds = _attn_out(dil, b_att, s_att, lambda g: g // steps_per_seq,
                              lambda g: g % steps_per_seq)
        out_specs += [spec, spec]
        out_shape += [sds, sds]
    scratch = scratch + _attn_scratch(2 * Q_BLOCK)
    outs = pl.pallas_call(
        functools.partial(_front_kernel, seq_tokens=s_att, steps_per_seq=steps_per_seq,
                          project=project),
        grid=(n_steps,),
        in_specs=in_specs, out_specs=out_specs, out_shape=out_shape, scratch_shapes=scratch,
        compiler_params=_cparams(1),
        name="front" if project else "attn",
    )(*operands, *qkvs, *biases)
    if not project:
        return list(outs)
    return outs[:3], outs[3:N_INPROJ_OUT], outs[N_INPROJ_OUT:]


def _mix_tile(x, gt1, o_refs, l_refs, u_ref, vn_ref, wcat_ref, bsm_ref, wout_ref, g_ref, perm_ref):
    t = x.shape[0]

    def slabs(ref, dil, base):
        if _class_major(dil):
            for r in range(dil):
                for sb in range(N_SLAB):
                    perm_ref[base + sb, pl.ds(r, t // dil, stride=dil), :] = ref[sb, r]
            ref, lo = perm_ref, base
        else:
            lo = 0
        return jnp.concatenate([ref[lo + sb] for sb in range(N_SLAB)], axis=-1)

    lses = [slabs(r, dil, 0) for r, dil in zip(l_refs, DILATIONS)]
    m = jnp.maximum(jnp.maximum(lses[0], lses[1]), lses[2])
    es = [jnp.exp(l - m) for l in lses]
    inv = 1.0 / (es[0] + es[1] + es[2])
    parts = [(es[gi] * inv * slabs(r, dil, N_SLAB)).astype(BF16)
             for gi, (r, dil) in enumerate(zip(o_refs, DILATIONS))]

    lane_group = lax.broadcasted_iota(jnp.int32, (GMLP_CHUNK, GMLP_W), 1) // GMLP_GROUP
    n_gm = GMLP_W // GMLP_GROUP
    gms = []
    for c in range(t // GMLP_CHUNK):
        rows = slice(c * GMLP_CHUNK, (c + 1) * GMLP_CHUNK)
        vn = vn_ref[rows, :]
        stacked = jnp.concatenate(
            [jnp.where(lane_group == g, vn, jnp.zeros_like(vn)) for g in range(n_gm)], axis=0)
        sv = jnp.dot(wcat_ref[...], stacked, preferred_element_type=F32) + bsm_ref[...]
        gms.append((u_ref[rows, :].astype(F32) * sv).astype(BF16))
    parts.append(jnp.concatenate(gms, axis=0))

    mix = jnp.dot(jnp.concatenate(parts, axis=-1), wout_ref[...], preferred_element_type=F32)
    return x + _rms(mix) * (gt1 * g_ref[...])


def _ffn_tile(x, sh2, sc2, gt2, gpre_ref, wgu_ref, wdown_ref, gpost_ref):
    h = (_rms(x) * (gpre_ref[...] * (1.0 + sc2)) + sh2).astype(BF16)
    acc = None
    for c in range(D_FF // FF_CHUNK):
        gate = jnp.dot(h, wgu_ref[:, FF_CHUNK * c:FF_CHUNK * (c + 1)], preferred_element_type=F32)
        up = jnp.dot(h, wgu_ref[:, D_FF + FF_CHUNK * c:D_FF + FF_CHUNK * (c + 1)],
                     preferred_element_type=F32)
        act = (gate * jax.nn.sigmoid(gate) * up).astype(BF16)
        part = jnp.dot(act, wdown_ref[FF_CHUNK * c:FF_CHUNK * (c + 1), :],
                       preferred_element_type=F32)
        acc = part if acc is None else acc + part
    return x + _rms(acc) * (gt2 * gpost_ref[...])


N_CHUNK_IN = 2 * N_GROUPS + 2


def _post_kernel(x_ref, mod_ref, *refs, bounds):
    n = len(bounds) * N_CHUNK_IN
    wcat_ref, bsm_ref, wout_ref, gmix_ref, gpre_ref, wgu_ref, wdown_ref, gpost_ref = refs[n:n + 8]
    y_ref, perm_ref = refs[n + 8:]
    bi = pl.program_id(0)

    def run(chunk):
        o0, l0, o1, l1, o2, l2, u_ref, vn_ref = chunk
        x1 = _mix_tile(x_ref[...], mod_ref[2:3, :], (o0, o1, o2), (l0, l1, l2), u_ref, vn_ref,
                       wcat_ref, bsm_ref, wout_ref, gmix_ref, perm_ref)
        y_ref[...] = _ffn_tile(x1, mod_ref[3:4, :], mod_ref[4:5, :], mod_ref[5:6, :],
                               gpre_ref, wgu_ref, wdown_ref, gpost_ref)

    for k, (off, cnt) in enumerate(bounds):
        chunk = refs[k * N_CHUNK_IN:(k + 1) * N_CHUNK_IN]
        if len(bounds) == 1:
            run(chunk)
        else:
            pl.when((bi >= off) & (bi < off + cnt))(functools.partial(run, chunk))


def _post(x, mod3, chunks, p, gains):
    b, s, d = x.shape
    tile = TOKEN_TILE
    tok = lambda w: pl.BlockSpec((None, tile, w), lambda bi, i: (bi, i, 0))
    const = lambda a: pl.BlockSpec(a.shape, lambda bi, i: (0,) * a.ndim)
    in_specs = [tok(d), pl.BlockSpec((None, N_MOD, d), lambda bi, i: (bi, 0, 0))]
    operands = [x, mod3]
    bounds = []
    for off, att, u, vn in chunks:
        cnt = u.shape[0]
        bounds.append((off, cnt))
        local = lambda bi, off=off, cnt=cnt: jnp.clip(bi - off, 0, cnt - 1)
        for dil in DILATIONS:
            if _class_major(dil):
                spec = pl.BlockSpec((None, N_SLAB, dil, tile // dil, LANES),
                                    lambda bi, i, local=local: (local(bi), 0, 0, i, 0))
            else:
                spec = pl.BlockSpec((None, N_SLAB, tile, LANES),
                                    lambda bi, i, local=local: (local(bi), 0, i, 0))
            in_specs += [spec, spec]
        in_specs += [pl.BlockSpec((None, tile, GMLP_W),
                                  lambda bi, i, local=local: (local(bi), i, 0))] * 2
        operands += [*att, u, vn]
    g_post_mix, g_pre_ffn, g_post_ffn = gains
    consts = (p["wcat"], p["bsm"], p["w_out"], g_post_mix, g_pre_ffn, p["w_gu"], p["w_down"],
              g_post_ffn)
    return pl.pallas_call(
        functools.partial(_post_kernel, bounds=tuple(bounds)),
        grid=(b, s // tile),
        in_specs=in_specs + [const(a) for a in consts],
        out_specs=tok(d),
        out_shape=jax.ShapeDtypeStruct((b, s, d), F32),
        scratch_shapes=[pltpu.VMEM((2 * N_SLAB, tile, LANES), F32)],
        compiler_params=_cparams(2),
        name="post",
    )(*operands, *consts)


def _prep_layer(w_in, w_s, b_s, g_gmlp, w_out, w_gu, w_down):
    col_scale = np.ones((1, IN_W), np.float32)
    col_scale[:, :ATTN_W] = HEAD_DIM ** -0.5
    n_gm = GMLP_W // GMLP_GROUP
    wcat = jnp.transpose(w_s, (1, 0, 2)).reshape(GMLP_CHUNK, n_gm * GMLP_CHUNK).astype(BF16)
    bsm = jnp.repeat(b_s.T, GMLP_GROUP, axis=1).astype(F32)
    grp = np.arange(GMLP_W) // GMLP_GROUP
    mavg = jnp.asarray((grp[:, None] == grp[None, :]).astype(np.float32) / GMLP_GROUP, BF16)
    return dict(w_in=(w_in * col_scale).astype(BF16), wcat=wcat, bsm=bsm, mavg=mavg,
                lng=g_gmlp.reshape(1, GMLP_W), w_out=w_out, w_gu=w_gu, w_down=w_down)


def _as_classes(q):
    q0, q1, q2 = q
    return q0.reshape(q0.shape[0], 1, q0.shape[1], q0.shape[2]), q1, q2


def _attn_all(qkvs):
    return _front(None, None, None, None, b_off=0, b=0, qkvs=qkvs)


def _layer(x_p, x_s, mod_p, mod_s, p, g_pre_mix, g_post_mix, g_pre_ffn, g_post_ffn):
    row = lambda g: g.reshape(1, D_MODEL)
    g_pre = row(g_pre_mix)
    gains = (row(g_post_mix), row(g_pre_ffn), row(g_post_ffn))
    groups = ((x_s, mod_s), (x_p, mod_p))

    def chunked(x):
        b, s = x.shape[:2]
        return (CHUNK_TOKENS % s == 0 and b % (CHUNK_TOKENS // s) == 0 and s % ATTN_TILE == 0)

    fuse = all(chunked(g[0]) for g in groups)
    chunks = []
    for gid, (x, _) in enumerate(groups):
        per = CHUNK_TOKENS // x.shape[1] if fuse else x.shape[0]
        chunks += [(gid, off, per) for off in range(0, x.shape[0], per)]

    done = [[], []]
    pending = None
    for gid, off, per in chunks:
        x, mod = groups[gid]
        if pending is not None and fuse:
            qk, pg, poff, pu, pvn = pending
            q, (u, vn), att = _front(x, mod, g_pre, p, b_off=off, b=per, qkvs=qk)
            done[pg].append((poff, att, pu, pvn))
        else:
            if pending is not None:
                qk, pg, poff, pu, pvn = pending
                done[pg].append((poff, _attn_all(qk), pu, pvn))
            late = [k for k in POST_WEIGHTS if p[k].dtype != BF16]
            q, (u, vn), cast = _inproj(x, mod, g_pre, p, b_off=off, b=per,
                                       cast=[p[k] for k in late])
            p = dict(p, **dict(zip(late, cast)))
        pending = (_as_classes(q), gid, off, u, vn)
    qk, pg, poff, pu, pvn = pending
    done[pg].append((poff, _attn_all(qk), pu, pvn))

    y_s, y_p = (_post(x, mod, done[gid], p, gains) for gid, (x, mod) in enumerate(groups))
    return y_p, y_s


def kernel(x_prompt, x_sample, c_prompt, c_sample, w_ada, b_ada, g_pre_mix, w_in, w_s, b_s, g_gmlp, w_out, g_post_mix, g_pre_ffn, w_gu, w_down, g_post_ffn):
    n_p = c_prompt.shape[0]
    c_all = jnp.concatenate([c_prompt, c_sample], axis=0)
    y_prompt, y_sample = x_prompt, x_sample
    for l in range(w_ada.shape[0]):
        mod3 = _adaln(c_all, w_ada[l], b_ada[l]).reshape(c_all.shape[0], N_MOD, D_MODEL)
        p = _prep_layer(w_in[l], w_s[l], b_s[l], g_gmlp[l], w_out[l], w_gu[l], w_down[l])
        y_prompt, y_sample = _layer(y_prompt, y_sample, mod3[:n_p], mod3[n_p:], p,
                                    g_pre_mix[l], g_post_mix[l], g_pre_ffn[l], g_post_ffn[l])
    return (y_prompt, y_sample)
```

```python
import functools

import numpy as np
import jax
import jax.numpy as jnp
from jax import lax
from jax.experimental import pallas as pl
from jax.experimental.pallas import tpu as pltpu

F32 = jnp.float32
BF16 = jnp.bfloat16

D_MODEL = 1024
HEAD_DIM = 64
HEADS_PER_GROUP = 4
GROUP_W = HEADS_PER_GROUP * HEAD_DIM
DILATIONS = (1, 4, 16)
N_SIDE = 64
N_GROUPS = len(DILATIONS)
ATTN_W = N_GROUPS * GROUP_W
QKV_W = 3 * GROUP_W
GMLP_W = 256
GMLP_GROUP = 64
GMLP_CHUNK = 128
IN_W = 3 * ATTN_W + 2 * GMLP_W
D_FF = 2816
N_MOD = 6
RMS_EPS = 1e-6
LN_EPS = 1e-5
NEG_INF = -1e30

LANES = 128
N_SLAB = GROUP_W // LANES
Q_BLOCK = 128
PERMUTE_STRIDE = 4
BF16_ROWS = 16
ATTN_TILE = Q_BLOCK * DILATIONS[-1]
TOKEN_TILE = 512
INPROJ_TILE = 1024
FFN_TILE = 1024
CHUNK_TOKENS = 16384
FRONT_BLOCKS = TOKEN_TILE // Q_BLOCK
FF_CHUNK = 256
VMEM_LIMIT = 60 * 1024 * 1024
QKV_WINDOW_BUDGET = 20 * 1024 * 1024


def _cparams(n_axes):
    return pltpu.CompilerParams(
        dimension_semantics=("arbitrary",) * n_axes, vmem_limit_bytes=VMEM_LIMIT)


def _adaln_kernel(c_ref, w_ref, b_ref, o_ref):
    c = c_ref[...]
    a = (c * jax.nn.sigmoid(c)).astype(BF16)
    o_ref[...] = jnp.dot(a, w_ref[...].astype(BF16), preferred_element_type=F32) + b_ref[...]


def _adaln(c, w_ada, b_ada):
    n, d = c.shape
    nout = w_ada.shape[1]
    tn = 1024
    return pl.pallas_call(
        _adaln_kernel,
        grid=(nout // tn,),
        in_specs=[
            pl.BlockSpec((n, d), lambda j: (0, 0)),
            pl.BlockSpec((d, tn), lambda j: (0, j)),
            pl.BlockSpec((1, tn), lambda j: (0, j)),
        ],
        out_specs=pl.BlockSpec((n, tn), lambda j: (0, j)),
        out_shape=jax.ShapeDtypeStruct((n, nout), F32),
        compiler_params=_cparams(1),
        name="adaln",
    )(c, w_ada, b_ada.reshape(1, nout))


def _rms(x):
    return x * lax.rsqrt(jnp.mean(x * x, axis=-1, keepdims=True) + RMS_EPS)


def _inproj_phases(x_ref, mod_ref, g_ref, w_ref, mavg_ref, lng_ref,
                   q0_ref, q1_ref, q2_ref, u_ref, vn_ref, slab1_ref, slab2_ref, spare_ref):
    t = x_ref.shape[0]
    state = {}

    def norm():
        sh1 = mod_ref[0:1, :]
        sc1 = mod_ref[1:2, :]
        state["h"] = (_rms(x_ref[...]) * (g_ref[...] * (1.0 + sc1)) + sh1).astype(BF16)

    def qkv(gi):
        h = state["h"]
        return jnp.concatenate(
            [jnp.dot(h, w_ref[:, part * ATTN_W + gi * GROUP_W:part * ATTN_W + (gi + 1) * GROUP_W],
                     preferred_element_type=F32) for part in range(3)], axis=-1)

    def group0():
        q0_ref[...] = qkv(0).astype(BF16)

    def project(gi, slab_ref):
        def run():
            p = qkv(gi)
            for j in range(QKV_W // LANES):
                slab_ref[j] = p[:, j * LANES:(j + 1) * LANES]
        return run

    def permute(gi, slab_ref, out_ref):
        def run():
            d = DILATIONS[gi]
            src, split = slab_ref, 1
            if d > PERMUTE_STRIDE:
                src, split = spare_ref, PERMUTE_STRIDE
                for a in range(split):
                    for j in range(QKV_W // LANES):
                        spare_ref[j, a * (t // split):(a + 1) * (t // split), :] = (
                            slab_ref[j, pl.ds(a, t // split, stride=split), :])
            for r in range(d):
                a, b = r % split, r // split
                rows = pl.ds(a * (t // split) + b, t // d, stride=d // split)
                for j in range(QKV_W // LANES):
                    out_ref[r, :, j * LANES:(j + 1) * LANES] = src[j, rows, :].astype(BF16)
        return run

    def gmlp_dots():
        h = state["h"]
        base = 3 * ATTN_W
        state["gu"] = jnp.dot(h, w_ref[:, base:base + GMLP_W], preferred_element_type=F32)
        state["gv"] = jnp.dot(h, w_ref[:, base + GMLP_W:base + 2 * GMLP_W],
                              preferred_element_type=F32)

    def gmlp_tail():
        u_ref[...] = jax.nn.gelu(state["gu"]).astype(BF16)
        v = jax.nn.gelu(state["gv"])
        mu = jnp.dot(v.astype(BF16), mavg_ref[...], preferred_element_type=F32)
        vc = v - mu
        var = jnp.dot((vc * vc).astype(BF16), mavg_ref[...], preferred_element_type=F32)
        vn_ref[...] = (vc * lax.rsqrt(var + LN_EPS) * lng_ref[...]).astype(BF16)

    return dict(norm=norm, group0=group0, project1=project(1, slab1_ref),
                permute1=permute(1, slab1_ref, q1_ref), project2=project(2, slab2_ref),
                permute2=permute(2, slab2_ref, q2_ref), gmlp_dots=gmlp_dots, gmlp_tail=gmlp_tail)


FRONT_PLAN = (("norm", 0), ("gmlp_dots", 1), ("project2", 3), ("gmlp_tail", 1), ("project1", 3),
              ("permute2", 1), ("group0", 3), ("permute1", 0))


POST_WEIGHTS = ("w_gu", "w_down", "w_out")
N_INPROJ_IN = 6
N_INPROJ_OUT = 5


def _inproj_kernel(*refs, n_cast):
    ins, rest = refs[:N_INPROJ_IN], refs[N_INPROJ_IN:]
    cast_in, rest = rest[:n_cast], rest[n_cast:]
    outs, rest = rest[:N_INPROJ_OUT], rest[N_INPROJ_OUT:]
    cast_out, scratch = rest[:n_cast], rest[n_cast:]
    phases = _inproj_phases(*ins, *outs, *scratch)
    for k, (name, _) in enumerate(FRONT_PLAN):
        phases[name]()
        if k < n_cast:
            cast_out[k][...] = cast_in[k][...].astype(BF16)


def _inproj_specs(b, s, b_off, step_of, tile=None):
    d = D_MODEL
    d1, d2 = DILATIONS[1], DILATIONS[2]
    tile = tile or TOKEN_TILE

    def tok(w, off):
        return pl.BlockSpec((None, tile, w), lambda *g: (step_of(*g)[0] + off, step_of(*g)[1], 0))

    def cls(dil):
        return pl.BlockSpec((None, dil, tile // dil, QKV_W),
                            lambda *g: (step_of(*g)[0], 0, step_of(*g)[1], 0))

    const = lambda shape: pl.BlockSpec(shape, lambda *g: (0,) * len(shape))
    in_specs = [
        tok(d, b_off),
        pl.BlockSpec((None, N_MOD, d), lambda *g: (step_of(*g)[0] + b_off, 0, 0)),
        const((1, d)), const((d, IN_W)), const((GMLP_W, GMLP_W)), const((1, GMLP_W)),
    ]
    out_specs = [tok(QKV_W, 0), cls(d1), cls(d2), tok(GMLP_W, 0), tok(GMLP_W, 0)]
    out_shape = [
        jax.ShapeDtypeStruct((b, s, QKV_W), BF16),
        jax.ShapeDtypeStruct((b, d1, s // d1, QKV_W), BF16),
        jax.ShapeDtypeStruct((b, d2, s // d2, QKV_W), BF16),
        jax.ShapeDtypeStruct((b, s, GMLP_W), BF16),
        jax.ShapeDtypeStruct((b, s, GMLP_W), BF16),
    ]
    scratch = [pltpu.VMEM((QKV_W // LANES, tile, LANES), F32)] * 3
    return in_specs, out_specs, out_shape, scratch


def _cast_spec(w, n_steps, tiles):
    rows_total = w.shape[0]
    repeat = 1
    while (rows_total * repeat) % n_steps or (rows_total * repeat // n_steps) % BF16_ROWS:
        repeat *= 2
        assert repeat <= n_steps
    rows = rows_total * repeat // n_steps
    return pl.BlockSpec((rows, w.shape[1]), lambda bi, i: ((bi * tiles + i) // repeat, 0))


def _inproj(x, mod3, g_pre, p, *, b_off, b, cast=()):
    s = x.shape[1]
    tiles = s // INPROJ_TILE
    in_specs, out_specs, out_shape, scratch = _inproj_specs(b, s, b_off, lambda bi, i: (bi, i),
                                                            INPROJ_TILE)
    cast_specs = [_cast_spec(w, b * tiles, tiles) for w in cast]
    outs = pl.pallas_call(
        functools.partial(_inproj_kernel, n_cast=len(cast)),
        grid=(b, tiles),
        in_specs=in_specs + cast_specs,
        out_specs=out_specs + cast_specs,
        out_shape=out_shape + [jax.ShapeDtypeStruct(w.shape, BF16) for w in cast],
        scratch_shapes=scratch,
        compiler_params=_cparams(2),
        name="inproj",
    )(x, mod3, g_pre, p["w_in"], p["mavg"], p["lng"], *cast)
    return outs[:3], outs[3:N_INPROJ_OUT], outs[N_INPROJ_OUT:]


def _attn_bias(dil, kw):
    group = DILATIONS.index(dil)
    n_heads = N_GROUPS * HEADS_PER_GROUP
    slopes = 2.0 ** (-8.0 * np.arange(1, n_heads + 1, dtype=np.float32) / n_heads)
    slopes = slopes[group * HEADS_PER_GROUP:(group + 1) * HEADS_PER_GROUP].astype(np.float32)
    n_var = 3 if kw == 2 * Q_BLOCK else 1
    i = np.arange(Q_BLOCK)[:, None]
    c = np.arange(kw)[None, :]
    out = np.empty((n_var, HEADS_PER_GROUP, Q_BLOCK, kw), np.float32)
    for v in range(n_var):
        rel = np.abs(c - N_SIDE * v - i)
        dist = (dil * rel).astype(np.float32)
        for h in range(HEADS_PER_GROUP):
            out[v, h] = np.where(rel <= N_SIDE, -slopes[h] * dist, np.float32(NEG_INF))
    return out


def _class_major(dil):
    return dil > FRONT_BLOCKS


def _attn_stages(qkv_ref, bias_ref, o_ref, l_ref, s_bufs, m_bufs, *, dil, seq, kw, coords):
    heads_per_tile = LANES // HEAD_DIM
    half = lax.broadcasted_iota(jnp.int32, (Q_BLOCK, LANES), 1) // HEAD_DIM

    def window(qs):
        return pl.multiple_of(jnp.clip(qs - N_SIDE, 0, seq - kw), N_SIDE)

    def scores(n, slot):
        r, qs, _ = coords(n)
        qs = pl.multiple_of(qs, Q_BLOCK)
        ks = window(qs)
        s_buf, m_buf = s_bufs[slot], m_bufs[slot]
        var = (qs - ks) // N_SIDE
        for h in range(HEADS_PER_GROUP):
            tile = slice((h // heads_per_tile) * LANES, (h // heads_per_tile + 1) * LANES)
            q = qkv_ref[r, pl.ds(qs, Q_BLOCK), tile]
            q = jnp.where(half == h % heads_per_tile, q, jnp.zeros_like(q))
            k = qkv_ref[r, pl.ds(ks, kw), GROUP_W + tile.start:GROUP_W + tile.stop]
            s = lax.dot_general(q, k, (((1,), (1,)), ((), ())), preferred_element_type=F32)
            bias = bias_ref[var, h]
            s = jnp.where(bias > 0.5 * NEG_INF, s + bias, NEG_INF)
            s_buf[h, :, 0:kw] = s
            m_buf[h] = jnp.broadcast_to(jnp.max(s, axis=-1, keepdims=True), (Q_BLOCK, LANES))

    def output(n, slot):
        r, qs, row0 = coords(n)
        ks = window(pl.multiple_of(qs, Q_BLOCK))
        s_buf, m_buf = s_bufs[slot], m_bufs[slot]
        if _class_major(dil):
            dst = lambda tl: (tl, row0)
        else:
            rows = pl.ds(row0, Q_BLOCK) if dil == 1 else pl.ds(row0, Q_BLOCK, stride=dil)
            dst = lambda tl: (tl, rows, slice(None))
        for tl in range(N_SLAB):
            o_pair, lse_pair = [], []
            for hp in range(heads_per_tile):
                h = tl * heads_per_tile + hp
                m = m_buf[h]
                p = [jnp.exp(s_buf[h, :, c * LANES:(c + 1) * LANES] - m)
                     for c in range(kw // LANES)]
                psum = p[0] if len(p) == 1 else p[0] + p[1]
                l = jnp.sum(psum, axis=-1, keepdims=True)
                pb = jnp.concatenate(p, axis=-1).astype(BF16)
                v = qkv_ref[r, pl.ds(ks, kw), 2 * GROUP_W + tl * LANES:2 * GROUP_W + (tl + 1) * LANES]
                o_pair.append(jnp.dot(pb, v, preferred_element_type=F32) * (1.0 / l))
                lse_pair.append(m + jnp.log(l))
            o_ref[dst(tl)] = jnp.where(half == 0, o_pair[0], o_pair[1])
            l_ref[dst(tl)] = jnp.where(half == 0, lse_pair[0], lse_pair[1])

    return scores, output


def _attn_scratch(kw):
    return [
        pltpu.VMEM((HEADS_PER_GROUP, Q_BLOCK, kw), F32),
        pltpu.VMEM((HEADS_PER_GROUP, Q_BLOCK, kw), F32),
        pltpu.VMEM((HEADS_PER_GROUP, Q_BLOCK, LANES), F32),
        pltpu.VMEM((HEADS_PER_GROUP, Q_BLOCK, LANES), F32),
    ]


def _attn_out(dil, b, s, batch_of, tile_of):
    if _class_major(dil):
        per_row = dil // FRONT_BLOCKS
        spec = pl.BlockSpec(
            (None, N_SLAB, FRONT_BLOCKS, Q_BLOCK, LANES),
            lambda g: (batch_of(g), 0, tile_of(g) % per_row, tile_of(g) // per_row, 0))
        return spec, jax.ShapeDtypeStruct((b, N_SLAB, dil, s // dil, LANES), F32)
    spec = pl.BlockSpec((None, N_SLAB, TOKEN_TILE, LANES),
                        lambda g: (batch_of(g), 0, tile_of(g), 0))
    return spec, jax.ShapeDtypeStruct((b, N_SLAB, s, LANES), F32)


def _front_kernel(*refs, seq_tokens, steps_per_seq, project):
    n_proj_in, n_proj_out = (N_INPROJ_IN, N_INPROJ_OUT) if project else (0, 0)
    n_in = n_proj_in + 2 * N_GROUPS
    n_out = n_proj_out + 2 * N_GROUPS
    ins, outs, scr = refs[:n_in], refs[n_in:n_in + n_out], refs[n_in + n_out:]
    qkv_refs = ins[n_proj_in:n_proj_in + N_GROUPS]
    bias_refs = ins[n_proj_in + N_GROUPS:]
    attn_outs = outs[n_proj_out:]
    slab_refs, s_bufs, m_bufs = scr[:-4], scr[-4:-2], scr[-2:]
    w = pl.program_id(0) % steps_per_seq

    units = []
    for gi, dil in enumerate(DILATIONS):
        seq = seq_tokens // dil
        kw = min(2 * Q_BLOCK, seq)

        def coords(n, dil=dil):
            m = w * FRONT_BLOCKS + n
            r, j = m % dil, m // dil
            if _class_major(dil):
                row0 = n
            else:
                row0 = (n // dil) * Q_BLOCK * dil + n % dil
            return r, j * Q_BLOCK, row0

        scores, output = _attn_stages(
            qkv_refs[gi], bias_refs[gi], attn_outs[2 * gi], attn_outs[2 * gi + 1],
            s_bufs, m_bufs, dil=dil, seq=seq, kw=kw, coords=coords)
        units += [(scores, output, n) for n in range(FRONT_BLOCKS)]

    n_units = len(units)
    if project:
        phases = _inproj_phases(*ins[:N_INPROJ_IN], *outs[:N_INPROJ_OUT], *slab_refs)
        plan = FRONT_PLAN
    else:
        phases = {"attention": lambda: None}
        plan = (("attention", n_units),)
    assert sum(cnt for _, cnt in plan) == n_units
    units[0][0](units[0][2], 0)
    k = 0
    for name, cnt in plan:
        phases[name]()
        for _ in range(cnt):
            if k + 1 < n_units:
                units[k + 1][0](units[k + 1][2], (k + 1) % 2)
            units[k][1](units[k][2], k % 2)
            k += 1


def _front(x, mod3, g_pre, p, *, b_off, b, qkvs):
    b_att = qkvs[0].shape[0]
    s_att = qkvs[0].shape[1] * qkvs[0].shape[2]
    steps_per_seq = s_att // TOKEN_TILE
    n_steps = b_att * steps_per_seq
    assert s_att % ATTN_TILE == 0
    project = x is not None
    in_specs, out_specs, out_shape, scratch, operands = [], [], [], [], []
    if project:
        s = x.shape[1]
        tiles_in = s // TOKEN_TILE
        assert n_steps == b * tiles_in
        step_in = lambda g: (g // tiles_in, g % tiles_in)
        in_specs, out_specs, out_shape, scratch = _inproj_specs(b, s, b_off, step_in)
        operands = [x, mod3, g_pre, p["w_in"], p["mavg"], p["lng"]]

    qkv_bytes = N_GROUPS * s_att * QKV_W * 2
    qkv_mode = pl.Buffered(1) if project and 2 * qkv_bytes > QKV_WINDOW_BUDGET else None
    biases = []
    for gi, dil in enumerate(DILATIONS):
        d, seq = qkvs[gi].shape[1:3]
        assert d == dil
        biases.append(jnp.asarray(_attn_bias(dil, min(2 * Q_BLOCK, seq))))
        in_specs.append(pl.BlockSpec((None, d, seq, QKV_W), lambda g: (g // steps_per_seq, 0, 0, 0),
                                     pipeline_mode=qkv_mode))
    in_specs += [pl.BlockSpec(bs.shape, lambda g: (0, 0, 0, 0)) for bs in biases]
    for dil in DILATIONS:
        spec, sds = _attn_out(dil, b_att, s_att, lambda g: g // steps_per_seq,
                              lambda g: g % steps_per_seq)
        out_specs += [spec, spec]
        out_shape += [sds, sds]
    scratch = scratch + _attn_scratch(2 * Q_BLOCK)
    outs = pl.pallas_call(
        functools.partial(_front_kernel, seq_tokens=s_att, steps_per_seq=steps_per_seq,
                          project=project),
        grid=(n_steps,),
        in_specs=in_specs, out_specs=out_specs, out_shape=out_shape, scratch_shapes=scratch,
        compiler_params=_cparams(1),
        name="front" if project else "attn",
    )(*operands, *qkvs, *biases)
    if not project:
        return list(outs)
    return outs[:3], outs[3:N_INPROJ_OUT], outs[N_INPROJ_OUT:]


def _mix_tile(x, gt1, o_refs, l_refs, u_ref, vn_ref, wcat_ref, bsm_ref, wout_ref, g_ref, perm_ref):
    t = x.shape[0]

    def slabs(ref, dil, base):
        if _class_major(dil):
            for r in range(dil):
                for sb in range(N_SLAB):
                    perm_ref[base + sb, pl.ds(r, t // dil, stride=dil), :] = ref[sb, r]
            ref, lo = perm_ref, base
        else:
            lo = 0
        return jnp.concatenate([ref[lo + sb] for sb in range(N_SLAB)], axis=-1)

    lses = [slabs(r, dil, 0) for r, dil in zip(l_refs, DILATIONS)]
    m = jnp.maximum(jnp.maximum(lses[0], lses[1]), lses[2])
    es = [jnp.exp(l - m) for l in lses]
    inv = 1.0 / (es[0] + es[1] + es[2])
    parts = [(es[gi] * inv * slabs(r, dil, N_SLAB)).astype(BF16)
             for gi, (r, dil) in enumerate(zip(o_refs, DILATIONS))]

    lane_group = lax.broadcasted_iota(jnp.int32, (GMLP_CHUNK, GMLP_W), 1) // GMLP_GROUP
    n_gm = GMLP_W // GMLP_GROUP
    gms = []
    for c in range(t // GMLP_CHUNK):
        rows = slice(c * GMLP_CHUNK, (c + 1) * GMLP_CHUNK)
        vn = vn_ref[rows, :]
        stacked = jnp.concatenate(
            [jnp.where(lane_group == g, vn, jnp.zeros_like(vn)) for g in range(n_gm)], axis=0)
        sv = jnp.dot(wcat_ref[...], stacked, preferred_element_type=F32) + bsm_ref[...]
        gms.append((u_ref[rows, :].astype(F32) * sv).astype(BF16))
    parts.append(jnp.concatenate(gms, axis=0))

    mix = jnp.dot(jnp.concatenate(parts, axis=-1), wout_ref[...], preferred_element_type=F32)
    return x + _rms(mix) * (gt1 * g_ref[...])


def _ffn_tile(x, sh2, sc2, gt2, gpre_ref, wgu_ref, wdown_ref, gpost_ref):
    h = (_rms(x) * (gpre_ref[...] * (1.0 + sc2)) + sh2).astype(BF16)
    acc = None
    for c in range(D_FF // FF_CHUNK):
        gate = jnp.dot(h, wgu_ref[:, FF_CHUNK * c:FF_CHUNK * (c + 1)], preferred_element_type=F32)
        up = jnp.dot(h, wgu_ref[:, D_FF + FF_CHUNK * c:D_FF + FF_CHUNK * (c + 1)],
                     preferred_element_type=F32)
        act = (gate * jax.nn.sigmoid(gate) * up).astype(BF16)
        part = jnp.dot(act, wdown_ref[FF_CHUNK * c:FF_CHUNK * (c + 1), :],
                       preferred_element_type=F32)
        acc = part if acc is None else acc + part
    return x + _rms(acc) * (gt2 * gpost_ref[...])


def _ffn_kernel(x_ref, mod_ref, gpre_ref, wgu_ref, wdown_ref, gpost_ref, y_ref):
    y_ref[...] = _ffn_tile(x_ref[...], mod_ref[3:4, :], mod_ref[4:5, :], mod_ref[5:6, :],
                           gpre_ref, wgu_ref, wdown_ref, gpost_ref)


N_CHUNK_IN = 2 * N_GROUPS + 2


def _post_kernel(x_ref, mod_ref, *refs, bounds):
    n = len(bounds) * N_CHUNK_IN
    wcat_ref, bsm_ref, wout_ref, gmix_ref = refs[n:n + 4]
    y_ref, perm_ref = refs[n + 4:]
    bi = pl.program_id(0)

    def run(chunk):
        o0, l0, o1, l1, o2, l2, u_ref, vn_ref = chunk
        y_ref[...] = _mix_tile(x_ref[...], mod_ref[2:3, :], (o0, o1, o2), (l0, l1, l2), u_ref,
                               vn_ref, wcat_ref, bsm_ref, wout_ref, gmix_ref, perm_ref)

    for k, (off, cnt) in enumerate(bounds):
        chunk = refs[k * N_CHUNK_IN:(k + 1) * N_CHUNK_IN]
        if len(bounds) == 1:
            run(chunk)
        else:
            pl.when((bi >= off) & (bi < off + cnt))(functools.partial(run, chunk))


def _post(x, mod3, chunks, p, gains):
    b, s, d = x.shape
    tile = TOKEN_TILE
    tok = lambda w: pl.BlockSpec((None, tile, w), lambda bi, i: (bi, i, 0))
    const = lambda a: pl.BlockSpec(a.shape, lambda bi, i: (0,) * a.ndim)
    in_specs = [tok(d), pl.BlockSpec((None, N_MOD, d), lambda bi, i: (bi, 0, 0))]
    operands = [x, mod3]
    bounds = []
    for off, att, u, vn in chunks:
        cnt = u.shape[0]
        bounds.append((off, cnt))
        local = lambda bi, off=off, cnt=cnt: jnp.clip(bi - off, 0, cnt - 1)
        for dil in DILATIONS:
            if _class_major(dil):
                spec = pl.BlockSpec((None, N_SLAB, dil, tile // dil, LANES),
                                    lambda bi, i, local=local: (local(bi), 0, 0, i, 0))
            else:
                spec = pl.BlockSpec((None, N_SLAB, tile, LANES),
                                    lambda bi, i, local=local: (local(bi), 0, i, 0))
            in_specs += [spec, spec]
        in_specs += [pl.BlockSpec((None, tile, GMLP_W),
                                  lambda bi, i, local=local: (local(bi), i, 0))] * 2
        operands += [*att, u, vn]
    g_post_mix, g_pre_ffn, g_post_ffn = gains
    consts = (p["wcat"], p["bsm"], p["w_out"], g_post_mix)
    x1 = pl.pallas_call(
        functools.partial(_post_kernel, bounds=tuple(bounds)),
        grid=(b, s // tile),
        in_specs=in_specs + [const(a) for a in consts],
        out_specs=tok(d),
        out_shape=jax.ShapeDtypeStruct((b, s, d), F32),
        scratch_shapes=[pltpu.VMEM((2 * N_SLAB, tile, LANES), F32)],
        compiler_params=_cparams(2),
        name="post",
    )(*operands, *consts)
    ffn_tok = pl.BlockSpec((None, FFN_TILE, d), lambda bi, i: (bi, i, 0))
    ffn_consts = (g_pre_ffn, p["w_gu"], p["w_down"], g_post_ffn)
    return pl.pallas_call(
        _ffn_kernel,
        grid=(b, s // FFN_TILE),
        in_specs=[ffn_tok, in_specs[1]] + [const(a) for a in ffn_consts],
        out_specs=ffn_tok,
        out_shape=jax.ShapeDtypeStruct((b, s, d), F32),
        compiler_params=_cparams(2),
        name="ffn",
    )(x1, mod3, *ffn_consts)


def _prep_layer(w_in, w_s, b_s, g_gmlp, w_out, w_gu, w_down):
    col_scale = np.ones((1, IN_W), np.float32)
    col_scale[:, :ATTN_W] = HEAD_DIM ** -0.5
    n_gm = GMLP_W // GMLP_GROUP
    wcat = jnp.transpose(w_s, (1, 0, 2)).reshape(GMLP_CHUNK, n_gm * GMLP_CHUNK).astype(BF16)
    bsm = jnp.repeat(b_s.T, GMLP_GROUP, axis=1).astype(F32)
    grp = np.arange(GMLP_W) // GMLP_GROUP
    mavg = jnp.asarray((grp[:, None] == grp[None, :]).astype(np.float32) / GMLP_GROUP, BF16)
    return dict(w_in=(w_in * col_scale).astype(BF16), wcat=wcat, bsm=bsm, mavg=mavg,
                lng=g_gmlp.reshape(1, GMLP_W), w_out=w_out, w_gu=w_gu, w_down=w_down)


def _as_classes(q):
    q0, q1, q2 = q
    return q0.reshape(q0.shape[0], 1, q0.shape[1], q0.shape[2]), q1, q2


def _attn_all(qkvs):
    return _front(None, None, None, None, b_off=0, b=0, qkvs=qkvs)


def _layer(x_p, x_s, mod_p, mod_s, p, g_pre_mix, g_post_mix, g_pre_ffn, g_post_ffn):
    row = lambda g: g.reshape(1, D_MODEL)
    g_pre = row(g_pre_mix)
    gains = (row(g_post_mix), row(g_pre_ffn), row(g_post_ffn))
    groups = ((x_s, mod_s), (x_p, mod_p))

    def chunked(x):
        b, s = x.shape[:2]
        return (CHUNK_TOKENS % s == 0 and b % (CHUNK_TOKENS // s) == 0 and s % ATTN_TILE == 0)

    fuse = all(chunked(g[0]) for g in groups)
    chunks = []
    for gid, (x, _) in enumerate(groups):
        per = CHUNK_TOKENS // x.shape[1] if fuse else x.shape[0]
        chunks += [(gid, off, per) for off in range(0, x.shape[0], per)]

    done = [[], []]
    pending = None
    for gid, off, per in chunks:
        x, mod = groups[gid]
        if pending is not None and fuse:
            qk, pg, poff, pu, pvn = pending
            q, (u, vn), att = _front(x, mod, g_pre, p, b_off=off, b=per, qkvs=qk)
            done[pg].append((poff, att, pu, pvn))
        else:
            if pending is not None:
                qk, pg, poff, pu, pvn = pending
                done[pg].append((poff, _attn_all(qk), pu, pvn))
            late = [k for k in POST_WEIGHTS if p[k].dtype != BF16]
            q, (u, vn), cast = _inproj(x, mod, g_pre, p, b_off=off, b=per,
                                       cast=[p[k] for k in late])
            p = dict(p, **dict(zip(late, cast)))
        pending = (_as_classes(q), gid, off, u, vn)
    qk, pg, poff, pu, pvn = pending
    done[pg].append((poff, _attn_all(qk), pu, pvn))

    y_s, y_p = (_post(x, mod, done[gid], p, gains) for gid, (x, mod) in enumerate(groups))
    return y_p, y_s


def kernel(x_prompt, x_sample, c_prompt, c_sample, w_ada, b_ada, g_pre_mix, w_in, w_s, b_s, g_gmlp, w_out, g_post_mix, g_pre_ffn, w_gu, w_down, g_post_ffn):
    n_p = c_prompt.shape[0]
    c_all = jnp.concatenate([c_prompt, c_sample], axis=0)
    y_prompt, y_sample = x_prompt, x_sample
    for l in range(w_ada.shape[0]):
        mod3 = _adaln(c_all, w_ada[l], b_ada[l]).reshape(c_all.shape[0], N_MOD, D_MODEL)
        p = _prep_layer(w_in[l], w_s[l], b_s[l], g_gmlp[l], w_out[l], w_gu[l], w_down[l])
        y_prompt, y_sample = _layer(y_prompt, y_sample, mod3[:n_p], mod3[n_p:], p,
                                    g_pre_mix[l], g_post_mix[l], g_pre_ffn[l], g_post_ffn[l])
    return (y_prompt, y_sample)
```
